```python
import jax, jax.numpy as jnp
from jax import lax
import numpy as np

D_MODEL = 1024
BATCH = 8
SEQ = 4096
DEPTH = 4

CTX_LEN = 256
GRID_W = 64
D_MIX = D_MODEL
N_GROUPS = 4
GROUP_W = D_MIX // N_GROUPS
HEAD_DIM = 64
GROUP_HEADS = GROUP_W // HEAD_DIM
NORM_EPS = 1e-6

CONV_W = GROUP_W
MLA_HEADS = GROUP_HEADS
MLA_NOPE = HEAD_DIM
MLA_ROPE = HEAD_DIM // 2
MLA_QK = MLA_NOPE + MLA_ROPE
MLA_V = HEAD_DIM
MLA_W = MLA_HEADS * MLA_V
KV_RANK = D_MODEL // 8
ROPE_BASE = 10000.0
Q_BLOCK = 128
RWKV_HEADS = GROUP_HEADS
RWKV_N = HEAD_DIM
RWKV_W = GROUP_W
DECAY_LORA = 64
ICLR_LORA = 64
GATE_LORA = 128
RWKV_GN_EPS = 64e-5
RET_HEADS = GROUP_HEADS
RET_DK = HEAD_DIM
RET_DV = HEAD_DIM
RET_W = GROUP_W
RET_CHUNK = 128
RET_THETA_BASE = 10000.0
N_EXP = 16
CAP_FACTOR = 2
EXP_FF = D_MODEL

COLS_CONV = 3 * CONV_W
COLS_MLA = MLA_HEADS * MLA_QK + KV_RANK + MLA_ROPE
COLS_RWKV = 3 * RWKV_W + DECAY_LORA + ICLR_LORA + GATE_LORA
COLS_RET = 4 * RET_W
IN_COLS = COLS_CONV + COLS_MLA + COLS_RWKV + COLS_RET
MIX_SPLITS = (COLS_CONV, COLS_CONV + COLS_MLA, COLS_CONV + COLS_MLA + COLS_RWKV)
MLA_SPLITS = (MLA_HEADS * MLA_NOPE, MLA_HEADS * MLA_QK, MLA_HEADS * MLA_QK + KV_RANK)
RWKV_SPLITS = (RWKV_W, 2 * RWKV_W, 3 * RWKV_W, 3 * RWKV_W + DECAY_LORA, 3 * RWKV_W + DECAY_LORA + ICLR_LORA)

kernel_name = 'hybrid_diffusion_parallel_groups_ec_moe'


def rms_norm(t, g):
    tf = t.astype(jnp.float32)
    y = tf * lax.rsqrt(jnp.mean(tf * tf, axis=-1, keepdims=True) + NORM_EPS)
    return y.astype(t.dtype) * g


def modulate(h, shift, scale):
    return h * (1.0 + scale) + shift


def split_heads(t, n):
    return t.reshape(t.shape[:-1] + (n, t.shape[-1] // n))


def merge_heads(t):
    return t.reshape(t.shape[:-2] + (-1,))


def shift_prev(t):
    return jnp.pad(t, ((0, 0), (1, 0), (0, 0)))[:, :-1]


def shift_next(t):
    return jnp.pad(t, ((0, 0), (0, 1), (0, 0)))[:, 1:]


def cos_sin(ang):
    return jnp.cos(ang)[:, None, :], jnp.sin(ang)[:, None, :]


def rotate(t, cos, sin):
    t1, t2 = jnp.split(t, 2, axis=-1)
    return jnp.concatenate([t1 * cos - t2 * sin, t1 * sin + t2 * cos], axis=-1)


def short_conv_mixer(z, conv_w, g_out):
    b, cg, h = jnp.split(z, 3, axis=-1)
    u = cg * h
    y = conv_w[0] * shift_prev(u) + conv_w[1] * u + conv_w[2] * shift_next(u)
    return rms_norm(b * y, g_out)


def rope_2d_tail(t, rot_row, rot_col):
    nope, rr, rc = jnp.split(t, [MLA_NOPE, MLA_NOPE + MLA_ROPE // 2], axis=-1)
    return jnp.concatenate([nope, rotate(rr, *rot_row), rotate(rc, *rot_col)], axis=-1)


def mla_queries(z, g_q):
    q_nope, q_rope, _, _ = jnp.split(z, MLA_SPLITS, axis=-1)
    q = jnp.concatenate([split_heads(q_nope, MLA_HEADS), split_heads(q_rope, MLA_HEADS)], axis=-1)
    return rms_norm(q, g_q)


def mla_keys_values(z, g_kv, w_uk, w_uv, g_k):
    _, _, ckv, k_rope = jnp.split(z, MLA_SPLITS, axis=-1)
    ckv = rms_norm(ckv, g_kv)
    k_nope = split_heads(ckv @ w_uk, MLA_HEADS)
    v = split_heads(ckv @ w_uv, MLA_HEADS)
    k_rope = jnp.broadcast_to(k_rope[:, :, None, :], k_nope.shape[:-1] + (MLA_ROPE,))
    k = rms_norm(jnp.concatenate([k_nope, k_rope], axis=-1), g_k)
    return k, v


def softmax_attend(q, k, v):
    s = jnp.einsum('bqhd,bkhd->bhqk', q, k).astype(jnp.float32) * (MLA_QK ** -0.5)
    p = jax.nn.softmax(s, axis=-1).astype(v.dtype)
    return jnp.einsum('bhqk,bkhd->bqhd', p, v)


def blocked_attend(q, k, v):
    B, L, H, dq = q.shape
    qb = q.reshape(B, L // Q_BLOCK, Q_BLOCK, H, dq).swapaxes(0, 1)
    out = lax.map(lambda blk: softmax_attend(blk, k, v), qb)
    return out.swapaxes(0, 1).reshape(B, L, H, -1)


def mla_mixer(z_x, z_c, g_kv, w_uk, w_uv, g_q, g_k, g_out, rot_row, rot_col, need_ctx):
    k_c, v_c = mla_keys_values(z_c, g_kv, w_uk, w_uv, g_k)
    k_x, v_x = mla_keys_values(z_x, g_kv, w_uk, w_uv, g_k)
    k_x = rope_2d_tail(k_x, rot_row, rot_col)
    q_x = rope_2d_tail(mla_queries(z_x, g_q), rot_row, rot_col)
    k_all = jnp.concatenate([k_c, k_x], axis=1)
    v_all = jnp.concatenate([v_c, v_x], axis=1)
    y_x = rms_norm(merge_heads(blocked_attend(q_x, k_all, v_all)), g_out)
    y_c = rms_norm(merge_heads(softmax_attend(mla_queries(z_c, g_q), k_c, v_c)), g_out) if need_ctx else None
    return y_x, y_c


def rwkv_project(z, mu, w0, w2, a0, a2, k_k, k_a):
    z = z.astype(jnp.float32)
    z = z + mu * (0.5 * (shift_prev(z) + shift_next(z)) - z)
    r, k, v, wl, al, gl = jnp.split(z, RWKV_SPLITS, axis=-1)
    kk = split_heads(k * k_k, RWKV_HEADS)
    kk = kk * lax.rsqrt(jnp.sum(kk * kk, axis=-1, keepdims=True) + 1e-12)
    dirs = []
    for d in range(2):
        log_w = -jax.nn.softplus(-(w0[d] + jnp.tanh(wl) @ w2[d])) - 0.5
        w = jnp.exp(-jnp.exp(log_w))
        a = jax.nn.sigmoid(a0[d] + al @ a2[d])
        kt = k * (1.0 + (a - 1.0) * k_a)
        dirs.append((split_heads(w, RWKV_HEADS), split_heads(a, RWKV_HEADS), split_heads(kt, RWKV_HEADS)))
    return split_heads(r, RWKV_HEADS), split_heads(v, RWKV_HEADS), kk, gl, dirs


def wkv_scan(state, w, a, k, v, kk, r, reverse, emit):
    xs = (w, a, k, v, kk) + ((r,) if emit else ())
    xs = tuple(jnp.moveaxis(t, 1, 0) for t in xs)

    def step(s, inp):
        w_t, a_t, k_t, v_t, kk_t = inp[:5]
        s = (s * w_t[:, :, None, :]
             - jnp.einsum('bhvk,bhk->bhv', s, kk_t)[..., None] * (kk_t * a_t)[:, :, None, :]
             + v_t[..., None] * k_t[:, :, None, :])
        y = jnp.einsum('bhvk,bhk->bhv', s, inp[5]) if emit else None
        return s, y

    state, ys = lax.scan(step, state, xs, reverse=reverse)
    return state, (jnp.moveaxis(ys, 0, 1) if emit else None)


def rwkv_out(y, r, v, k_sum, gl, g2, r_k, g_ln):
    mean = jnp.mean(y, axis=-1, keepdims=True)
    var = jnp.mean(jnp.square(y - mean), axis=-1, keepdims=True)
    y = (y - mean) * lax.rsqrt(var + RWKV_GN_EPS) * g_ln.reshape(RWKV_HEADS, RWKV_N)
    y = y + jnp.sum(r * k_sum * r_k.reshape(RWKV_HEADS, RWKV_N), axis=-1, keepdims=True) * v
    return merge_heads(y) * (jax.nn.sigmoid(gl) @ g2)


def rwkv_mixer(z_x, z_c, mu, w0, w2, a0, a2, g2, k_k, k_a, r_k, g_ln, need_ctx):
    s0 = jnp.zeros((z_x.shape[0], RWKV_HEADS, RWKV_N, RWKV_N), jnp.float32)
    rx, vx, kkx, glx, dx = rwkv_project(z_x, mu, w0, w2, a0, a2, k_k, k_a)
    rc, vc, kkc, glc, dc = rwkv_project(z_c, mu, w0, w2, a0, a2, k_k, k_a)
    s_fw, yc_fw = wkv_scan(s0, *dc[0], vc, kkc, rc, False, need_ctx)
    s_bw, yc_bw = wkv_scan(s0, *dc[1], vc, kkc, rc, True, need_ctx)
    _, yx_fw = wkv_scan(s_fw, *dx[0], vx, kkx, rx, False, True)
    _, yx_bw = wkv_scan(s_bw, *dx[1], vx, kkx, rx, True, True)
    y_x = rwkv_out(yx_fw + yx_bw, rx, vx, dx[0][2] + dx[1][2], glx, g2, r_k, g_ln)
    y_c = rwkv_out(yc_fw + yc_bw, rc, vc, dc[0][2] + dc[1][2], glc, g2, r_k, g_ln) if need_ctx else None
    return y_x, y_c


def ret_project(z, rot):
    q, k, v, g = jnp.split(z.astype(jnp.float32), 4, axis=-1)
    q = split_heads(q, RET_HEADS) * (RET_DK ** -0.5)
    k = split_heads(k, RET_HEADS)
    if rot is not None:
        q = rotate(q, *rot)
        k = rotate(k, *rot)
    return q, k, split_heads(v, RET_HEADS), g


def ret_chunkwise(q, k, v, r0, log_gamma, strict, emit):
    B, L, H, dk = k.shape
    nc = L // RET_CHUNK
    kc = k.reshape(B, nc, RET_CHUNK, H, dk)
    vc = v.reshape(B, nc, RET_CHUNK, H, -1)
    j = jnp.arange(RET_CHUNK, dtype=jnp.float32)
    zeta = jnp.exp((RET_CHUNK - 1 - j)[:, None] * log_gamma[None, :])
    d_r = jnp.einsum('bnchk,bnchv->bnhkv', kc * zeta[None, None, :, :, None], vc)
    dec_c = jnp.exp(RET_CHUNK * log_gamma)[None, :, None, None]

    def step(r_state, dr_i):
        return r_state * dec_c + dr_i, (r_state if emit else None)

    r_fin, r_prev = lax.scan(step, r0, d_r.swapaxes(0, 1))
    if not emit:
        return None, r_fin
    qc = q.reshape(B, nc, RET_CHUNK, H, dk)
    xi = jnp.exp((j + 1.0)[:, None] * log_gamma[None, :])
    cross = jnp.einsum('bnchk,nbhkv->bnchv', qc * xi[None, None, :, :, None], r_prev)
    diff = j[:, None] - j[None, :]
    mask = diff > 0 if strict else diff >= 0
    dmat = jnp.where(mask[None], jnp.exp(jnp.where(mask, diff, 0.0)[None] * log_gamma[:, None, None]), 0.0)
    s = jnp.einsum('bnqhd,bnkhd->bnhqk', qc, kc) * dmat[None, None]
    inner = jnp.einsum('bnhqk,bnkhv->bnqhv', s, vc)
    return (inner + cross).reshape(B, L, H, -1), r_fin


def ret_out(y, g, g_norm):
    y = rms_norm(y, g_norm.reshape(RET_HEADS, RET_DV))
    return jax.nn.silu(g) * merge_heads(y)


def retention_mixer(z_x, z_c, g_norm, rot, log_gamma, need_ctx):
    r0 = jnp.zeros((z_x.shape[0], RET_HEADS, RET_DK, RET_DV), jnp.float32)
    flip = lambda t: jnp.flip(t, axis=1)
    qx, kx, vx, gx = ret_project(z_x, rot)
    qc, kc, vc, gc = ret_project(z_c, None)
    yc_fw, r_fw = ret_chunkwise(qc, kc, vc, r0, log_gamma, False, need_ctx)
    yc_bw, r_bw = ret_chunkwise(flip(qc), flip(kc), flip(vc), r0, log_gamma, True, need_ctx)
    yx_fw, _ = ret_chunkwise(qx, kx, vx, r_fw, log_gamma, False, True)
    yx_bw, _ = ret_chunkwise(flip(qx), flip(kx), flip(vx), r_bw, log_gamma, True, True)
    y_x = ret_out(yx_fw + flip(yx_bw), gx, g_norm)
    y_c = ret_out(yc_fw + flip(yc_bw), gc, g_norm) if need_ctx else None
    return y_x, y_c


def ec_ffn(h, w_router, b_router, w_gate, w_up, w_down):
    n_tok = h.shape[1]
    cap = CAP_FACTOR * n_tok // N_EXP

    def route_set(hs):
        aff = jax.nn.softmax((hs @ w_router + b_router).astype(jnp.float32), axis=-1)
        gate, idx = lax.top_k(aff.T, cap)
        xe = hs[idx]
        hid = jax.nn.silu(jnp.einsum('ecd,edf->ecf', xe, w_gate)) * jnp.einsum('ecd,edf->ecf', xe, w_up)
        ye = jnp.einsum('ecf,efd->ecd', hid, w_down) * gate[..., None].astype(hid.dtype)
        return jnp.zeros(hs.shape, ye.dtype).at[idx.reshape(-1)].add(ye.reshape(-1, hs.shape[-1]))

    return jax.vmap(route_set)(h)


def setup_inputs(seed: int = 0) -> dict:
    key = jax.random.key(seed)
    ks = iter(jax.random.split(key, 40))
    f32 = jnp.float32
    D = D_MODEL

    def nrm(shape, scale):
        return scale * jax.random.normal(next(ks), shape, f32)

    def gain(shape):
        return 1.0 + nrm(shape, 0.02)

    return {
        'x': nrm((BATCH, SEQ, D), 1.0),
        'c': nrm((BATCH, D), 1.0),
        'ctx': nrm((BATCH, CTX_LEN, D), 1.0),
        'c_ctx': nrm((D,), 1.0),
        'w_ada': nrm((DEPTH, D, 6 * D), 0.5 * D ** -0.5),
        'b_ada': nrm((DEPTH, 6 * D), 0.02),
        'g_norm1': gain((DEPTH, D)),
        'g_norm2': gain((DEPTH, D)),
        'w_in': nrm((DEPTH, D, IN_COLS), D ** -0.5),
        'w_out': nrm((DEPTH, D_MIX, D), D_MIX ** -0.5),
        'conv_w': nrm((DEPTH, 3, CONV_W), 0.5),
        'g_out_conv': gain((DEPTH, CONV_W)),
        'g_kv_norm': gain((DEPTH, KV_RANK)),
        'w_uk': nrm((DEPTH, KV_RANK, MLA_HEADS * MLA_NOPE), KV_RANK ** -0.5),
        'w_uv': nrm((DEPTH, KV_RANK, MLA_HEADS * MLA_V), KV_RANK ** -0.5),
        'g_q_norm': gain((DEPTH, MLA_QK)),
        'g_k_norm': gain((DEPTH, MLA_QK)),
        'g_out_mla': gain((DEPTH, MLA_W)),
        'rwkv_mu': 0.5 + nrm((DEPTH, COLS_RWKV), 0.1),
        'rwkv_w0': jax.random.uniform(next(ks), (DEPTH, 2, RWKV_W), f32, -6.0, -1.0),
        'rwkv_w2': nrm((DEPTH, 2, DECAY_LORA, RWKV_W), 0.1),
        'rwkv_a0': nrm((DEPTH, 2, RWKV_W), 0.1),
        'rwkv_a2': nrm((DEPTH, 2, ICLR_LORA, RWKV_W), 0.1),
        'rwkv_g2': nrm((DEPTH, GATE_LORA, RWKV_W), GATE_LORA ** -0.5),
        'rwkv_k_k': 0.85 + nrm((DEPTH, RWKV_W), 0.05),
        'rwkv_k_a': 1.0 + nrm((DEPTH, RWKV_W), 0.05),
        'rwkv_r_k': nrm((DEPTH, RWKV_W), 0.1),
        'g_ln_x': gain((DEPTH, RWKV_W)),
        'g_ret_norm': gain((DEPTH, RET_W)),
        'w_router': nrm((DEPTH, D, N_EXP), D ** -0.5),
        'b_router': nrm((DEPTH, N_EXP), 0.01),
        'w_gate': nrm((DEPTH, N_EXP, D, EXP_FF), D ** -0.5),
        'w_up': nrm((DEPTH, N_EXP, D, EXP_FF), D ** -0.5),
        'w_down': nrm((DEPTH, N_EXP, EXP_FF, D), EXP_FF ** -0.5),
    }


def reference(x, c, ctx, c_ctx, w_ada, b_ada, g_norm1, g_norm2, w_in, w_out, conv_w, g_out_conv,
              g_kv_norm, w_uk, w_uv, g_q_norm, g_k_norm, g_out_mla, rwkv_mu, rwkv_w0, rwkv_w2,
              rwkv_a0, rwkv_a2, rwkv_g2, rwkv_k_k, rwkv_k_a, rwkv_r_k, g_ln_x, g_ret_norm,
              w_router, b_router, w_gate, w_up, w_down):
    f32 = jnp.float32
    L = x.shape[1]
    ROWS = L // GRID_W
    row = jnp.repeat(jnp.arange(ROWS, dtype=f32), GRID_W)
    col = jnp.tile(jnp.arange(GRID_W, dtype=f32), ROWS)
    axis_dim = MLA_ROPE // 2
    inv_axis = ROPE_BASE ** (-jnp.arange(0, axis_dim, 2, dtype=f32) / axis_dim)
    rot_row = cos_sin(row[:, None] * inv_axis[None, :])
    rot_col = cos_sin(col[:, None] * inv_axis[None, :])
    theta = 1.0 / (RET_THETA_BASE ** jnp.linspace(0.0, 1.0, RET_DK // 2, dtype=f32))
    rot_ret = cos_sin(jnp.arange(L, dtype=f32)[:, None] * theta[None, :])
    log_gamma = jnp.log1p(-jnp.exp2(-5.0 - jnp.arange(RET_HEADS, dtype=f32)))

    xc = ctx
    for l in range(DEPTH):
        need_ctx = l < DEPTH - 1
        mod_x = (jax.nn.silu(c) @ w_ada[l] + b_ada[l])[:, None, :]
        mod_c = jax.nn.silu(c_ctx) @ w_ada[l] + b_ada[l]
        sh1, sc1, ga1, sh2, sc2, ga2 = jnp.split(mod_x, 6, axis=-1)
        csh1, csc1, cga1, csh2, csc2, cga2 = jnp.split(mod_c, 6, axis=-1)

        z_x = modulate(rms_norm(x, g_norm1[l]), sh1, sc1) @ w_in[l]
        z_c = modulate(rms_norm(xc, g_norm1[l]), csh1, csc1) @ w_in[l]
        zx_conv, zx_mla, zx_rwkv, zx_ret = jnp.split(z_x, MIX_SPLITS, axis=-1)
        zc_conv, zc_mla, zc_rwkv, zc_ret = jnp.split(z_c, MIX_SPLITS, axis=-1)
        y_mla_x, y_mla_c = mla_mixer(zx_mla, zc_mla, g_kv_norm[l], w_uk[l], w_uv[l], g_q_norm[l],
                                     g_k_norm[l], g_out_mla[l], rot_row, rot_col, need_ctx)
        y_rwkv_x, y_rwkv_c = rwkv_mixer(zx_rwkv, zc_rwkv, rwkv_mu[l], rwkv_w0[l], rwkv_w2[l], rwkv_a0[l],
                                        rwkv_a2[l], rwkv_g2[l], rwkv_k_k[l], rwkv_k_a[l], rwkv_r_k[l],
                                        g_ln_x[l], need_ctx)
        y_ret_x, y_ret_c = retention_mixer(zx_ret, zc_ret, g_ret_norm[l], rot_ret, log_gamma, need_ctx)
        y_x = jnp.concatenate([short_conv_mixer(zx_conv, conv_w[l], g_out_conv[l]),
                               y_mla_x, y_rwkv_x, y_ret_x], axis=-1).astype(x.dtype)
        x = x + ga1 * (y_x @ w_out[l])
        if need_ctx:
            y_c = jnp.concatenate([short_conv_mixer(zc_conv, conv_w[l], g_out_conv[l]),
                                   y_mla_c, y_rwkv_c, y_ret_c], axis=-1).astype(xc.dtype)
            xc = xc + cga1 * (y_c @ w_out[l])

        h_x = modulate(rms_norm(x, g_norm2[l]), sh2, sc2)
        x = x + ga2 * ec_ffn(h_x, w_router[l], b_router[l], w_gate[l], w_up[l], w_down[l])
        if need_ctx:
            h_c = modulate(rms_norm(xc, g_norm2[l]), csh2, csc2)
            xc = xc + cga2 * ec_ffn(h_c, w_router[l], b_router[l], w_gate[l], w_up[l], w_down[l])
    return x
```

```python
import functools

import jax
import jax.numpy as jnp
from jax import lax
from jax.experimental import pallas as pl
from jax.experimental.pallas import tpu as pltpu

F32 = jnp.float32
BF16 = jnp.bfloat16
HI = lax.Precision.HIGHEST

D_MODEL = 1024
N_HEADS = 4
HEAD_DIM = 64
GROUP_W = N_HEADS * HEAD_DIM
NORM_EPS = 1e-6
MLA_NOPE = 64
MLA_ROPE = 32
MLA_QK = MLA_NOPE + MLA_ROPE
MLA_SLOT = 128
KV_RANK = 128
ROPE_BASE = 10000.0
DECAY_LORA = 64
ICLR_LORA = 64
GATE_LORA = 128
RWKV_GN_EPS = 64e-5
RWKV_CHUNK = 64
RWKV_INV_BASE = 16
RET_CHUNK = 256
RET_THETA_BASE = 10000.0
N_EXP = 16
CAP_FACTOR = 2
GRID_W = 64

COLS_CONV = 3 * GROUP_W
COLS_MLA_IN = N_HEADS * MLA_QK + KV_RANK + MLA_ROPE
COLS_MLA = N_HEADS * MLA_SLOT + MLA_SLOT + KV_RANK
COLS_RWKV = 3 * GROUP_W + DECAY_LORA + ICLR_LORA + GATE_LORA
COLS_RET = 4 * GROUP_W

TOKEN_TILE = 256
MOE_TOKEN_SLICE = 1024
VMEM_LIMIT = 56 * 1024 * 1024

NN = (((1,), (0,)), ((), ()))
NT = (((1,), (1,)), ((), ()))
TN = (((0,), (0,)), ((), ()))


def _cparams(*sem):
    return pltpu.CompilerParams(dimension_semantics=sem, vmem_limit_bytes=VMEM_LIMIT)


def _dg(a, b, dims=NN, precision=None):
    return lax.dot_general(a, b, dims, precision=precision, preferred_element_type=F32)


def _split(a):
    hi = a.astype(BF16)
    return hi, (a - hi.astype(F32)).astype(BF16)


def _mm(a, b, dims=NN):
    ah, al = a if isinstance(a, tuple) else _split(a)
    bh, bl = b if isinstance(b, tuple) else _split(b)
    return _dg(ah, bh, dims) + _dg(ah, bl, dims) + _dg(al, bh, dims)


def _head_ones(n, width, scale=1.0):
    r = lax.broadcasted_iota(jnp.int32, (n, n), 0) // width
    c = lax.broadcasted_iota(jnp.int32, (n, n), 1) // width
    return jnp.where(r == c, scale, 0.0).astype(F32)


def _lane_head(shape, width):
    return lax.broadcasted_iota(jnp.int32, shape, len(shape) - 1) // width


def _sigmoid(x):
    return 1.0 / (1.0 + jnp.exp(-x))


def _pair_specs(shape, index_map):
    return [pl.BlockSpec(shape, index_map), pl.BlockSpec(shape, index_map)]


def _mods_kernel(cc_ref, w_ref, b_ref, o_ref):
    cc = cc_ref[...]
    o_ref[0] = _dg(cc * _sigmoid(cc), w_ref[0], NN, HI) + b_ref[0]


def _mods(cc, w_ada, b_ada):
    depth, d, n = w_ada.shape
    tn = 1536
    return pl.pallas_call(
        _mods_kernel,
        grid=(depth, n // tn),
        in_specs=[pl.BlockSpec(cc.shape, lambda l, j: (0, 0)),
                  pl.BlockSpec((1, d, tn), lambda l, j: (l, 0, j)),
                  pl.BlockSpec((1, 1, tn), lambda l, j: (l, 0, j))],
        out_specs=pl.BlockSpec((1, cc.shape[0], tn), lambda l, j: (l, 0, j)),
        out_shape=jax.ShapeDtypeStruct((depth, cc.shape[0], n), F32),
        compiler_params=_cparams("parallel", "parallel"),
        name="adaln_mods",
    )(cc, w_ada, b_ada.reshape(depth, 1, n))


def _inproj_kernel(x_ref, mod_ref, g_ref, wh_ref, wl_ref, zc_ref, zm_ref, zr_ref, zt_ref):
    x = x_ref[0]
    m = mod_ref[0]
    xn = x * lax.rsqrt(jnp.mean(x * x, axis=-1, keepdims=True) + NORM_EPS) * g_ref[...]
    h = _split(xn * (1.0 + m[1:2]) + m[0:1])
    o = 0
    for ref, n in ((zc_ref, COLS_CONV), (zm_ref, COLS_MLA), (zr_ref, COLS_RWKV), (zt_ref, COLS_RET)):
        ref[0] = _mm(h, (wh_ref[:, o:o + n], wl_ref[:, o:o + n]))
        o += n


def _inproj(x, mod, g, w):
    b, lq, d = x.shape
    tm = min(TOKEN_TILE, lq)
    widths = (COLS_CONV, COLS_MLA, COLS_RWKV, COLS_RET)
    return pl.pallas_call(
        _inproj_kernel,
        grid=(b, lq // tm),
        in_specs=[pl.BlockSpec((1, tm, d), lambda i, j: (i, j, 0)),
                  pl.BlockSpec((1, 6, d), lambda i, j: (i, 0, 0)),
                  pl.BlockSpec((1, d), lambda i, j: (0, 0))] + _pair_specs(w[0].shape, lambda i, j: (0, 0)),
        out_specs=[pl.BlockSpec((1, tm, n), lambda i, j: (i, j, 0)) for n in widths],
        out_shape=[jax.ShapeDtypeStruct((b, lq, n), F32) for n in widths],
        compiler_params=_cparams("parallel", "parallel"),
        name="inproj",
    )(x, mod, g.reshape(1, d), *w)


def _halo_specs(tm, lq, width):
    r = tm // 8
    last = lq // 8 - 1
    return [pl.BlockSpec((1, tm, width), lambda i, j: (i, j, 0)),
            pl.BlockSpec((1, 8, width), lambda i, j: (i, jnp.maximum(j * r - 1, 0), 0)),
            pl.BlockSpec((1, 8, width), lambda i, j: (i, jnp.minimum((j + 1) * r, last), 0))]


def _shifted(t, prev_row, next_row):
    n = t.shape[0]
    rows = lax.broadcasted_iota(jnp.int32, t.shape, 0)
    t_prev = jnp.where(rows == 0, prev_row, pltpu.roll(t, 1, 0))
    t_next = jnp.where(rows == n - 1, next_row, pltpu.roll(t, n - 1, 0))
    return t_prev, t_next


def _conv_kernel(z_ref, zp_ref, zn_ref, cw_ref, g_ref, o_ref, *, nt):
    j = pl.program_id(1)
    w = GROUP_W
    z = z_ref[0]
    bgate = z[:, :w]
    u = z[:, w:2 * w] * z[:, 2 * w:]
    zp = zp_ref[0][7:8]
    zn = zn_ref[0][0:1]
    up = jnp.where(j > 0, zp[:, w:2 * w] * zp[:, 2 * w:], 0.0)
    un = jnp.where(j < nt - 1, zn[:, w:2 * w] * zn[:, 2 * w:], 0.0)
    u_prev, u_next = _shifted(u, up, un)
    cw = cw_ref[...]
    t = bgate * (cw[0:1] * u_prev + cw[1:2] * u + cw[2:3] * u_next)
    o_ref[0] = t * lax.rsqrt(jnp.mean(t * t, axis=-1, keepdims=True) + NORM_EPS) * g_ref[...]


def _conv_mixer(zc, conv_w, g_out):
    b, lq, width = zc.shape
    tm = min(TOKEN_TILE, lq)
    return pl.pallas_call(
        functools.partial(_conv_kernel, nt=lq // tm),
        grid=(b, lq // tm),
        in_specs=_halo_specs(tm, lq, width) + [
            pl.BlockSpec((3, GROUP_W), lambda i, j: (0, 0)),
            pl.BlockSpec((1, GROUP_W), lambda i, j: (0, 0))],
        out_specs=pl.BlockSpec((1, tm, GROUP_W), lambda i, j: (i, j, 0)),
        out_shape=jax.ShapeDtypeStruct((b, lq, GROUP_W), F32),
        compiler_params=_cparams("parallel", "parallel"),
        name="conv_mixer",
    )(zc, zc, zc, conv_w, g_out.reshape(1, GROUP_W))


def _tile4(t):
    return jnp.concatenate([t, t, t, t], axis=-1)


def _slot_norm_rope(t, g, cos, sin):
    parts = []
    for h in range(N_HEADS):
        s = t[:, h * MLA_SLOT:(h + 1) * MLA_SLOT]
        ms = jnp.sum(s * s, axis=-1, keepdims=True) * (1.0 / MLA_QK)
        parts.append(s * lax.rsqrt(ms + NORM_EPS) * g)
    tn = jnp.concatenate(parts, axis=-1)
    n = tn.shape[-1]
    lane = lax.broadcasted_iota(jnp.int32, tn.shape, 1)
    half = MLA_ROPE // 4
    partner = jnp.where(lane % (2 * half) < half, pltpu.roll(tn, n - half, 1), pltpu.roll(tn, half, 1))
    return tn * _tile4(cos) + partner * _tile4(sin)


def _mla_prep_kernel(z_ref, cos_ref, sin_ref, gkv_ref, wukh_ref, wukl_ref, wuvh_ref, wuvl_ref, gq_ref, gk_ref,
                     qh_ref, ql_ref, kh_ref, kl_ref, vh_ref, vl_ref):
    z = z_ref[0]
    nq = N_HEADS * MLA_SLOT
    q_in = z[:, :nq]
    k_rope = z[:, nq:nq + MLA_SLOT]
    ckv = z[:, nq + MLA_SLOT:]
    ckv = _split(ckv * lax.rsqrt(jnp.mean(ckv * ckv, axis=-1, keepdims=True) + NORM_EPS) * gkv_ref[...])
    k_in = _mm(ckv, (wukh_ref[...], wukl_ref[...])) + _tile4(k_rope)
    cos = cos_ref[...]
    sin = sin_ref[...]
    qh_ref[0], ql_ref[0] = _split(_slot_norm_rope(q_in, gq_ref[...], cos, sin) * (MLA_QK ** -0.5))
    kh_ref[0], kl_ref[0] = _split(_slot_norm_rope(k_in, gk_ref[...], cos, sin))
    vh_ref[0], vl_ref[0] = _split(_mm(ckv, (wuvh_ref[...], wuvl_ref[...])))


def _mla_prep(zm, cos, sin, g_kv, wuk, wuv, g_q, g_k):
    b, lq, width = zm.shape
    tm = min(TOKEN_TILE, lq)
    nq = N_HEADS * MLA_SLOT
    const = lambda shape: pl.BlockSpec(shape, lambda i, j: (0, 0))
    tok = lambda i, j: (i, j, 0)
    return pl.pallas_call(
        _mla_prep_kernel,
        grid=(b, lq // tm),
        in_specs=[pl.BlockSpec((1, tm, width), tok),
                  pl.BlockSpec((tm, MLA_SLOT), lambda i, j: (j, 0)),
                  pl.BlockSpec((tm, MLA_SLOT), lambda i, j: (j, 0)),
                  const((1, KV_RANK)), const(wuk[0].shape), const(wuk[0].shape),
                  const(wuv[0].shape), const(wuv[0].shape),
                  const((1, MLA_SLOT)), const((1, MLA_SLOT))],
        out_specs=_pair_specs((1, tm, nq), tok) + _pair_specs((1, tm, nq), tok) + _pair_specs((1, tm, GROUP_W), tok),
        out_shape=[jax.ShapeDtypeStruct((b, lq, nq), BF16)] * 4 + [jax.ShapeDtypeStruct((b, lq, GROUP_W), BF16)] * 2,
        compiler_params=_cparams("parallel", "parallel"),
        name="mla_prep",
    )(zm, cos, sin, g_kv.reshape(1, KV_RANK), *wuk, *wuv, g_q, g_k)


def _attn_kernel(qh_ref, ql_ref, kh_ref, kl_ref, vh_ref, vl_ref, g_ref, o_ref):
    v = (vh_ref[0], vl_ref[0])
    tq = qh_ref.shape[1]
    head = _lane_head((tq, GROUP_W), HEAD_DIM)
    out = jnp.zeros((tq, GROUP_W), F32)
    for h in range(N_HEADS):
        sl = slice(h * MLA_SLOT, (h + 1) * MLA_SLOT)
        s = _mm((qh_ref[0, :, sl], ql_ref[0, :, sl]), (kh_ref[0, :, sl], kl_ref[0, :, sl]), NT)
        p = jnp.exp(s - jnp.max(s, axis=-1, keepdims=True))
        inv = 1.0 / jnp.sum(p, axis=-1, keepdims=True)
        out = jnp.where(head == h, _mm(p, v) * inv, out)
    o_ref[0] = out * lax.rsqrt(jnp.mean(out * out, axis=-1, keepdims=True) + NORM_EPS) * g_ref[...]


def _attention(q, k, v, g_out):
    b, lq, nq = q[0].shape
    lk = k[0].shape[1]
    tq = min(TOKEN_TILE, lq)
    return pl.pallas_call(
        _attn_kernel,
        grid=(b, lq // tq),
        in_specs=_pair_specs((1, tq, nq), lambda i, j: (i, j, 0))
        + _pair_specs((1, lk, nq), lambda i, j: (i, 0, 0))
        + _pair_specs((1, lk, GROUP_W), lambda i, j: (i, 0, 0))
        + [pl.BlockSpec((1, GROUP_W), lambda i, j: (0, 0))],
        out_specs=pl.BlockSpec((1, tq, GROUP_W), lambda i, j: (i, j, 0)),
        out_shape=jax.ShapeDtypeStruct((b, lq, GROUP_W), F32),
        compiler_params=_cparams("parallel", "arbitrary"),
        name="mla_attention",
    )(*q, *k, *v, g_out.reshape(1, GROUP_W))


def _rwkv_proj_kernel(z_ref, zp_ref, zn_ref, mu_ref, kk_ref, ka_ref, w0_ref, a0_ref, lora_ref,
                      r_ref, v_ref, kkn_ref, gl_ref, ks_ref,
                      lw0_ref, b0_ref, kt0_ref, lw1_ref, b1_ref, kt1_ref, *, nt):
    j = pl.program_id(1)
    w = GROUP_W
    z = z_ref[0]
    zp = jnp.where(j > 0, zp_ref[0][7:8], 0.0)
    zn = jnp.where(j < nt - 1, zn_ref[0][0:1], 0.0)
    z_prev, z_next = _shifted(z, zp, zn)
    z = z + mu_ref[...] * (0.5 * (z_prev + z_next) - z)
    r = z[:, :w]
    k = z[:, w:2 * w]
    v = z[:, 2 * w:3 * w]
    lora_in = z[:, 3 * w:3 * w + DECAY_LORA + ICLR_LORA]
    gl = z[:, 3 * w + DECAY_LORA + ICLR_LORA:]
    kk = k * kk_ref[...]
    kk = kk * lax.rsqrt(_dg(kk * kk, _head_ones(w, HEAD_DIM), NN, HI) + 1e-12)
    lane = lax.broadcasted_iota(jnp.int32, lora_in.shape, 1)
    lora_in = jnp.where(lane < DECAY_LORA, jnp.tanh(lora_in), lora_in)
    r_ref[0] = r
    v_ref[0] = v
    kkn_ref[0] = kk
    gl_ref[0] = gl
    ksum = jnp.zeros_like(k)
    for d, (lw_ref, b_ref, kt_ref) in enumerate(((lw0_ref, b0_ref, kt0_ref), (lw1_ref, b1_ref, kt1_ref))):
        lo = _dg(lora_in, lora_ref[d], NN, HI)
        t = -(w0_ref[d:d + 1] + lo[:, :w])
        softplus = jnp.maximum(t, 0.0) + jnp.log1p(jnp.exp(-jnp.abs(t)))
        lw_ref[0] = -jnp.exp(-softplus - 0.5)
        a = _sigmoid(a0_ref[d:d + 1] + lo[:, w:])
        kt = k * (1.0 + (a - 1.0) * ka_ref[...])
        b_ref[0] = kk * a
        kt_ref[0] = kt
        ksum = ksum + kt
    ks_ref[0] = ksum


def _rwkv_proj(zr, mu, k_k, k_a, w0, a0, lora):
    b, lq, width = zr.shape
    tm = min(TOKEN_TILE, lq)
    w = GROUP_W
    const2 = lambda shape: pl.BlockSpec(shape, lambda i, j: (0,) * len(shape))
    out_w = (w, w, w, GATE_LORA, w, w, w, w, w, w, w)
    return pl.pallas_call(
        functools.partial(_rwkv_proj_kernel, nt=lq // tm),
        grid=(b, lq // tm),
        in_specs=_halo_specs(tm, lq, width) + [
            const2((1, width)), const2((1, w)), const2((1, w)), const2((2, w)), const2((2, w)),
            const2(lora.shape)],
        out_specs=[pl.BlockSpec((1, tm, n), lambda i, j: (i, j, 0)) for n in out_w],
        out_shape=[jax.ShapeDtypeStruct((b, lq, n), F32) for n in out_w],
        compiler_params=_cparams("parallel", "parallel"),
        name="rwkv_proj",
    )(zr, zr, zr, mu.reshape(1, width), k_k.reshape(1, w), k_a.reshape(1, w), w0, a0, lora)


def _stack_heads(t):
    head = _lane_head(t.shape, HEAD_DIM)
    return jnp.concatenate([jnp.where(head == h, t, 0.0) for h in range(N_HEADS)], axis=0)


def _unit_lower_inverse(n_mat, p_i, q_i, size):
    base = RWKV_INV_BASE
    same = lambda w: (p_i // w) == (q_i // w)
    m = jnp.where(same(base), -n_mat, 0.0)
    inv = jnp.where(p_i == q_i, 1.0, 0.0) + m
    pw = m
    span = 1
    while 2 * span < base:
        pws = _split(pw)
        pw = _mm(pws, pws)
        inv = inv + _mm(inv, pw)
        span *= 2
    w = base
    while w < size:
        off = jnp.where(same(2 * w) & jnp.logical_not(same(w)), n_mat, 0.0)
        invs = _split(inv)
        inv = inv - _mm(_mm(invs, off), invs)
        w *= 2
    return inv


def _rwkv_scan_kernel(lw_ref, kk_ref, b_ref, kt_ref, v_ref, r_ref, s0_ref, y_ref, sout_ref, s_ref,
                      *, reverse, nc):
    i = pl.program_id(1)

    @pl.when(i == 0)
    def _():
        s_ref[...] = s0_ref[0]

    c = RWKV_CHUNK
    n = N_HEADS * c
    lw = lw_ref[0]
    t_i = lax.broadcasted_iota(jnp.int32, (c, c), 0)
    s_i = lax.broadcasted_iota(jnp.int32, (c, c), 1)
    seen = (s_i >= t_i) if reverse else (s_i <= t_i)
    g = _dg(jnp.where(seen, 1.0, 0.0).astype(F32), lw, NN, HI)
    g_tot = g[0:1] if reverse else g[c - 1:c]
    e_neg = jnp.exp(-g)
    e_tot = jnp.exp(g_tot)
    kkd = _split(_stack_heads(kk_ref[0] * jnp.exp(g - lw)))
    rd = _split(_stack_heads(r_ref[0] * jnp.exp(g)))
    bd = _stack_heads(b_ref[0] * e_neg)
    ktd = _stack_heads(kt_ref[0] * e_neg)
    vs = _split(_stack_heads(v_ref[0]))
    bds = _split(bd)
    ktds = _split(ktd)

    p_i = lax.broadcasted_iota(jnp.int32, (n, n), 0)
    q_i = lax.broadcasted_iota(jnp.int32, (n, n), 1)
    strict = (q_i > p_i) if reverse else (q_i < p_i)
    incl = (q_i >= p_i) if reverse else (q_i <= p_i)
    n_ab = jnp.where(strict, _mm(kkd, bds, NT), 0.0)
    n_ak = jnp.where(strict, _mm(kkd, ktds, NT), 0.0)
    a_rb = jnp.where(incl, _mm(rd, bds, NT), 0.0)
    a_rk = jnp.where(incl, _mm(rd, ktds, NT), 0.0)
    t_inv = _unit_lower_inverse(n_ab, p_i, q_i, c)

    s = s_ref[...]
    ss = _split(s)
    u = _split(-_mm(t_inv, _mm(kkd, ss, NT) + _mm(n_ak, vs)))
    y = _mm(rd, ss, NT) + _mm(a_rb, u) + _mm(a_rk, vs)
    y_ref[0] = y[0:c] + y[c:2 * c] + y[2 * c:3 * c] + y[3 * c:]
    s_new = s * e_tot + _mm(u, bd * e_tot, TN) + _mm(vs, ktd * e_tot, TN)
    s_ref[...] = s_new

    @pl.when(i == nc - 1)
    def _():
        sout_ref[0] = s_new


def _rwkv_scan(lw, kk, bb, kt, v, r, s0, reverse):
    b, lq, w = lw.shape
    c = RWKV_CHUNK
    nc = lq // c
    tok = (lambda i, j: (i, nc - 1 - j, 0)) if reverse else (lambda i, j: (i, j, 0))
    tspec = pl.BlockSpec((1, c, w), tok)
    sspec = pl.BlockSpec((1, w, w), lambda i, j: (i, 0, 0))
    return pl.pallas_call(
        functools.partial(_rwkv_scan_kernel, reverse=reverse, nc=nc),
        grid=(b, nc),
        in_specs=[tspec] * 6 + [sspec],
        out_specs=[tspec, sspec],
        out_shape=[jax.ShapeDtypeStruct((b, lq, w), F32), jax.ShapeDtypeStruct((b, w, w), F32)],
        scratch_shapes=[pltpu.VMEM((w, w), F32)],
        compiler_params=_cparams("parallel", "arbitrary"),
        name="rwkv_scan_bw" if reverse else "rwkv_scan_fw",
    )(lw, kk, bb, kt, v, r, s0)


def _rwkv_out_kernel(yf_ref, yb_ref, r_ref, v_ref, ks_ref, gl_ref, g2h_ref, g2l_ref, rk_ref, gln_ref, o_ref):
    y = yf_ref[0] + yb_ref[0]
    avg = _head_ones(GROUP_W, HEAD_DIM, 1.0 / HEAD_DIM)
    dlt = y - _dg(y, avg, NN, HI)
    var = _dg(dlt * dlt, avg, NN, HI)
    yn = dlt * lax.rsqrt(var + RWKV_GN_EPS) * gln_ref[...]
    bonus = _dg(r_ref[0] * ks_ref[0] * rk_ref[...], _head_ones(GROUP_W, HEAD_DIM), NN, HI)
    yn = yn + bonus * v_ref[0]
    o_ref[0] = yn * _mm(_sigmoid(gl_ref[0]), (g2h_ref[...], g2l_ref[...]))


def _rwkv_out(yf, yb, r, v, ks, gl, g2, r_k, g_ln):
    b, lq, w = yf.shape
    tm = min(TOKEN_TILE, lq)
    tspec = pl.BlockSpec((1, tm, w), lambda i, j: (i, j, 0))
    const = lambda shape: pl.BlockSpec(shape, lambda i, j: (0, 0))
    return pl.pallas_call(
        _rwkv_out_kernel,
        grid=(b, lq // tm),
        in_specs=[tspec] * 5 + [pl.BlockSpec((1, tm, GATE_LORA), lambda i, j: (i, j, 0)),
                                const(g2[0].shape), const(g2[0].shape), const((1, w)), const((1, w))],
        out_specs=tspec,
        out_shape=jax.ShapeDtypeStruct((b, lq, w), F32),
        compiler_params=_cparams("parallel", "parallel"),
        name="rwkv_out",
    )(yf, yb, r, v, ks, gl, *g2, r_k.reshape(1, w), g_ln.reshape(1, w))


def _half_rotate(t, cos, sin):
    n = t.shape[-1]
    lane = lax.broadcasted_iota(jnp.int32, t.shape, 1)
    half = HEAD_DIM // 2
    partner = jnp.where(lane % HEAD_DIM < half, pltpu.roll(t, n - half, 1), pltpu.roll(t, half, 1))
    return t * cos + partner * sin


def _ret_scan_kernel(z_ref, cos_ref, sin_ref, xi_ref, zeta_ref, dmat_ref, dec_ref, r0_ref,
                     y_ref, rout_ref, st_ref, *, nc):
    i = pl.program_id(1)

    @pl.when(i == 0)
    def _():
        st_ref[...] = r0_ref[0]

    w = GROUP_W
    z = z_ref[0]
    cos = cos_ref[...]
    sin = sin_ref[...]
    q = _half_rotate(z[:, :w] * (HEAD_DIM ** -0.5), cos, sin)
    k = _half_rotate(z[:, w:2 * w], cos, sin)
    vs = _split(z[:, 2 * w:3 * w])
    ks = _split(k)
    head = _lane_head(q.shape, HEAD_DIM)
    st = st_ref[...]
    y = _mm(q * xi_ref[...], st)
    for h in range(N_HEADS):
        s = _mm(jnp.where(head == h, q, 0.0), ks, NT) * dmat_ref[h]
        y = y + jnp.where(head == h, _mm(s, vs), 0.0)
    y_ref[0] = y
    st_new = st * dec_ref[...] + _mm(k * zeta_ref[...], vs, TN) * _head_ones(w, HEAD_DIM)
    st_ref[...] = st_new

    @pl.when(i == nc - 1)
    def _():
        rout_ref[0] = st_new


def _ret_scan(zt, cos, sin, xi, zeta, dmat, dec, r0, reverse):
    b, lq, width = zt.shape
    c = min(RET_CHUNK, lq)
    nc = lq // c
    w = GROUP_W
    tok3 = (lambda i, j: (i, nc - 1 - j, 0)) if reverse else (lambda i, j: (i, j, 0))
    tok2 = (lambda i, j: (nc - 1 - j, 0)) if reverse else (lambda i, j: (j, 0))
    const = lambda shape: pl.BlockSpec(shape, lambda i, j: (0,) * len(shape))
    sspec = pl.BlockSpec((1, w, w), lambda i, j: (i, 0, 0))
    return pl.pallas_call(
        functools.partial(_ret_scan_kernel, nc=nc),
        grid=(b, nc),
        in_specs=[pl.BlockSpec((1, c, width), tok3),
                  pl.BlockSpec((c, w), tok2), pl.BlockSpec((c, w), tok2),
                  const((c, w)), const((c, w)), const((N_HEADS, c, c)), const((1, w)), sspec],
        out_specs=[pl.BlockSpec((1, c, w), tok3), sspec],
        out_shape=[jax.ShapeDtypeStruct((b, lq, w), F32), jax.ShapeDtypeStruct((b, w, w), F32)],
        scratch_shapes=[pltpu.VMEM((w, w), F32)],
        compiler_params=_cparams("parallel", "arbitrary"),
        name="ret_scan_bw" if reverse else "ret_scan_fw",
    )(zt, cos, sin, xi, zeta, dmat, dec, r0)


def _ret_out_kernel(yf_ref, yb_ref, z_ref, g_ref, o_ref):
    y = yf_ref[0] + yb_ref[0]
    ms = _dg(y * y, _head_ones(GROUP_W, HEAD_DIM, 1.0 / HEAD_DIM), NN, HI)
    gate = z_ref[0]
    o_ref[0] = gate * _sigmoid(gate) * (y * lax.rsqrt(ms + NORM_EPS) * g_ref[...])


def _ret_out(yf, yb, zt, g_norm):
    b, lq, w = yf.shape
    tm = min(TOKEN_TILE, lq)
    tspec = pl.BlockSpec((1, tm, w), lambda i, j: (i, j, 0))
    return pl.pallas_call(
        _ret_out_kernel,
        grid=(b, lq // tm),
        in_specs=[tspec, tspec, pl.BlockSpec((1, tm, w), lambda i, j: (i, j, 3)),
                  pl.BlockSpec((1, w), lambda i, j: (0, 0))],
        out_specs=tspec,
        out_shape=jax.ShapeDtypeStruct((b, lq, w), F32),
        compiler_params=_cparams("parallel", "parallel"),
        name="ret_out",
    )(yf, yb, zt, g_norm.reshape(1, w))


def _outproj_kernel(yc_ref, ym_ref, yr_ref, yt_ref, x_ref, mod_ref, wh_ref, wl_ref, g_ref, wr_ref, br_ref,
                    xo_ref, hh_ref, hl_ref, lg_ref):
    w = GROUP_W
    acc = None
    for n, ref in enumerate((yc_ref, ym_ref, yr_ref, yt_ref)):
        rows = slice(n * w, (n + 1) * w)
        part = _mm(ref[0], (wh_ref[rows, :], wl_ref[rows, :]))
        acc = part if acc is None else acc + part
    m = mod_ref[0]
    x = x_ref[0] + m[2:3] * acc
    xo_ref[0] = x
    xn = x * lax.rsqrt(jnp.mean(x * x, axis=-1, keepdims=True) + NORM_EPS) * g_ref[...]
    h = xn * (1.0 + m[4:5]) + m[3:4]
    hh_ref[0], hl_ref[0] = _split(h)
    lg_ref[0] = _dg(wr_ref[...], h, NT, HI) + br_ref[...]


def _outproj(ys, x, mod, w_out, g2, wr_t, b_r):
    b, lq, d = x.shape
    tm = min(TOKEN_TILE, lq)
    yspec = pl.BlockSpec((1, tm, GROUP_W), lambda i, j: (i, j, 0))
    xspec = pl.BlockSpec((1, tm, d), lambda i, j: (i, j, 0))
    const = lambda shape: pl.BlockSpec(shape, lambda i, j: (0, 0))
    return pl.pallas_call(
        _outproj_kernel,
        grid=(b, lq // tm),
        in_specs=[yspec] * 4 + [xspec, pl.BlockSpec((1, 6, d), lambda i, j: (i, 0, 0)),
                                const(w_out[0].shape), const(w_out[0].shape), const((1, d)),
                                const(wr_t.shape), const((N_EXP, 1))],
        out_specs=[xspec, xspec, xspec, pl.BlockSpec((1, N_EXP, tm), lambda i, j: (i, 0, j))],
        out_shape=[jax.ShapeDtypeStruct((b, lq, d), F32), jax.ShapeDtypeStruct((b, lq, d), BF16),
                   jax.ShapeDtypeStruct((b, lq, d), BF16), jax.ShapeDtypeStruct((b, N_EXP, lq), F32)],
        compiler_params=_cparams("parallel", "parallel"),
        name="outproj",
    )(*ys, x, mod, *w_out, g2.reshape(1, d), wr_t, b_r.reshape(N_EXP, 1))


def _lane_cumsum(m):
    n = m.shape[-1]
    blk = 128
    r_i = lax.broadcasted_iota(jnp.int32, (blk, blk), 0)
    c_i = lax.broadcasted_iota(jnp.int32, (blk, blk), 1)
    tri = jnp.where(r_i <= c_i, 1.0, 0.0).astype(BF16)
    run = jnp.zeros((m.shape[0], 1), F32)
    parts = []
    for j in range(n // blk):
        cs = _dg(m[:, j * blk:(j + 1) * blk].astype(BF16), tri) + run
        parts.append(cs)
        run = cs[:, blk - 1:blk]
    return jnp.concatenate(parts, axis=-1)


def _route_kernel(lg_ref, rank_ref, gate_ref, *, cap):
    lg = lg_ref[0]
    e = jnp.exp(lg - jnp.max(lg, axis=0, keepdims=True))
    aff = e / jnp.sum(e, axis=0, keepdims=True)
    bits = pltpu.bitcast(aff, jnp.int32)
    thr = jnp.zeros((aff.shape[0], 1), jnp.int32)
    for bit in range(30, -1, -1):
        cand = thr | (1 << bit)
        cnt = jnp.sum(jnp.where(bits >= cand, 1.0, 0.0), axis=-1, keepdims=True)
        thr = jnp.where(cnt >= cap, cand, thr)
    above = bits > thr
    tied = bits == thr
    n_above = jnp.sum(jnp.where(above, 1.0, 0.0), axis=-1, keepdims=True)
    tied_f = jnp.where(tied, 1.0, 0.0)
    tied_rank = _lane_cumsum(tied_f) - tied_f
    sel = above | (tied & (tied_rank < cap - n_above))
    sel_f = jnp.where(sel, 1.0, 0.0)
    rank = _lane_cumsum(sel_f) - sel_f
    rank_ref[0] = jnp.where(sel, rank, -1.0).astype(jnp.int32)
    gate_ref[0] = jnp.where(sel, aff, 0.0)


def _route(logits_t, cap):
    b, ne, lq = logits_t.shape
    spec = pl.BlockSpec((1, ne, lq), lambda i: (i, 0, 0))
    return pl.pallas_call(
        functools.partial(_route_kernel, cap=cap),
        grid=(b,),
        in_specs=[spec],
        out_specs=[spec, spec],
        out_shape=[jax.ShapeDtypeStruct((b, ne, lq), jnp.int32), jax.ShapeDtypeStruct((b, ne, lq), F32)],
        compiler_params=_cparams("parallel"),
        name="moe_route",
    )(logits_t)


def _onehot_rows(rank_row, cap):
    r_i = lax.broadcasted_iota(jnp.int32, (cap, rank_row.shape[-1]), 0)
    return r_i == rank_row


def _gather_kernel(hh_ref, hl_ref, rank_ref, gate_ref, xh_ref, xl_ref, gr_ref, *, cap):
    lq = hh_ref.shape[1]
    ts = min(MOE_TOKEN_SLICE, lq)
    xh = xl = gr = None
    for t in range(lq // ts):
        tok = slice(t * ts, (t + 1) * ts)
        sel = _onehot_rows(rank_ref[0, 0, :, tok], cap)
        p = jnp.where(sel, 1.0, 0.0).astype(BF16)
        ph = _dg(p, hh_ref[0, tok, :])
        pl_ = _dg(p, hl_ref[0, tok, :])
        pg = jnp.sum(jnp.where(sel, gate_ref[0, 0, :, tok], 0.0), axis=-1, keepdims=True)
        xh, xl, gr = (ph, pl_, pg) if xh is None else (xh + ph, xl + pl_, gr + pg)
    xh_ref[0, 0] = xh.astype(BF16)
    xl_ref[0, 0] = xl.astype(BF16)
    gr_ref[0, 0] = gr


def _gather(hh, hl, rank, gate, cap):
    b, lq, d = hh.shape
    ne = rank.shape[1]
    rspec = pl.BlockSpec((1, 1, 1, lq), lambda i, e: (i, e, 0, 0))
    xspec = pl.BlockSpec((1, 1, cap, d), lambda i, e: (i, e, 0, 0))
    return pl.pallas_call(
        functools.partial(_gather_kernel, cap=cap),
        grid=(b, ne),
        in_specs=_pair_specs((1, lq, d), lambda i, e: (i, 0, 0)) + [rspec, rspec],
        out_specs=[xspec, xspec, pl.BlockSpec((1, 1, cap, 1), lambda i, e: (i, e, 0, 0))],
        out_shape=[jax.ShapeDtypeStruct((b, ne, cap, d), BF16), jax.ShapeDtypeStruct((b, ne, cap, d), BF16),
                   jax.ShapeDtypeStruct((b, ne, cap, 1), F32)],
        compiler_params=_cparams("parallel", "arbitrary"),
        name="moe_gather",
    )(hh, hl, rank.reshape(b, ne, 1, lq), gate.reshape(b, ne, 1, lq))


def _ffn_kernel(xh_ref, xl_ref, gr_ref, wgh_ref, wgl_ref, wuh_ref, wul_ref, wdh_ref, wdl_ref, yh_ref, yl_ref):
    xe = (xh_ref[0, 0], xl_ref[0, 0])
    a = _mm(xe, (wgh_ref[0], wgl_ref[0]))
    hid = a * _sigmoid(a) * _mm(xe, (wuh_ref[0], wul_ref[0]))
    ye = _mm(hid, (wdh_ref[0], wdl_ref[0])) * gr_ref[0, 0]
    yh_ref[0, 0], yl_ref[0, 0] = _split(ye)


def _expert_ffn(xh, xl, gr, w_gate, w_up, w_down):
    b, ne, cap, d = xh.shape
    ff = w_gate[0].shape[2]
    xspec = pl.BlockSpec((1, 1, cap, d), lambda e, i: (i, e, 0, 0))
    wspec = lambda shape: _pair_specs((1,) + shape, lambda e, i: (e, 0, 0))
    return pl.pallas_call(
        _ffn_kernel,
        grid=(ne, b),
        in_specs=[xspec, xspec, pl.BlockSpec((1, 1, cap, 1), lambda e, i: (i, e, 0, 0))]
        + wspec((d, ff)) + wspec((d, ff)) + wspec((ff, d)),
        out_specs=[xspec, xspec],
        out_shape=[jax.ShapeDtypeStruct((b, ne, cap, d), BF16)] * 2,
        compiler_params=_cparams("parallel", "arbitrary"),
        name="moe_ffn",
    )(xh, xl, gr, *w_gate, *w_up, *w_down)


def _combine_kernel(x_ref, ga_ref, rank_ref, yh_ref, yl_ref, o_ref, *, cap):
    e = pl.program_id(2)
    lq = x_ref.shape[1]
    ts = min(MOE_TOKEN_SLICE, lq)
    for t in range(lq // ts):
        tok = slice(t * ts, (t + 1) * ts)
        p = jnp.where(_onehot_rows(rank_ref[0, 0, :, tok], cap), 1.0, 0.0).astype(BF16)
        part = _dg(p, yh_ref[0, 0], TN) + _dg(p, yl_ref[0, 0], TN)

        @pl.when(e == 0)
        def _():
            o_ref[0, tok, :] = part

        @pl.when(e > 0)
        def _():
            o_ref[0, tok, :] += part

    @pl.when(e == N_EXP - 1)
    def _():
        o_ref[0] = x_ref[0] + ga_ref[0] * o_ref[0]


def _combine(x, mod, rank, yh, yl, cap):
    b, lq, d = x.shape
    ne = rank.shape[1]
    dcol = d // 2
    xspec = pl.BlockSpec((1, lq, dcol), lambda i, j, e: (i, 0, j))
    yspec = pl.BlockSpec((1, 1, cap, dcol), lambda i, j, e: (i, e, 0, j))
    return pl.pallas_call(
        functools.partial(_combine_kernel, cap=cap),
        grid=(b, d // dcol, ne),
        in_specs=[xspec, pl.BlockSpec((1, 1, dcol), lambda i, j, e: (i, 0, j)),
                  pl.BlockSpec((1, 1, 1, lq), lambda i, j, e: (i, e, 0, 0)), yspec, yspec],
        out_specs=xspec,
        out_shape=jax.ShapeDtypeStruct((b, lq, d), F32),
        compiler_params=_cparams("parallel", "parallel", "arbitrary"),
        name="moe_combine",
    )(x, mod[:, 5:6, :], rank.reshape(b, ne, 1, lq), yh, yl)


def _split_w(w):
    hi = w.astype(BF16)
    return hi, (w - hi.astype(F32)).astype(BF16)


def _layout_w_in(w):
    d = w.shape[0]
    o = COLS_CONV
    q_nope = w[:, o:o + N_HEADS * MLA_NOPE]
    q_rope = w[:, o + N_HEADS * MLA_NOPE:o + N_HEADS * MLA_QK]
    ckv = w[:, o + N_HEADS * MLA_QK:o + N_HEADS * MLA_QK + KV_RANK]
    k_rope = w[:, o + N_HEADS * MLA_QK + KV_RANK:o + COLS_MLA_IN]
    pad = jnp.zeros((d, MLA_SLOT - MLA_QK), w.dtype)
    cols = [w[:, :o]]
    for h in range(N_HEADS):
        cols += [q_nope[:, h * MLA_NOPE:(h + 1) * MLA_NOPE], q_rope[:, h * MLA_ROPE:(h + 1) * MLA_ROPE], pad]
    cols += [jnp.zeros((d, MLA_NOPE), w.dtype), k_rope, pad, ckv, w[:, o + COLS_MLA_IN:]]
    return _split_w(jnp.concatenate(cols, axis=1))


def _layout_w_uk(w_uk):
    pad = jnp.zeros((w_uk.shape[0], MLA_SLOT - MLA_NOPE), w_uk.dtype)
    cols = []
    for h in range(N_HEADS):
        cols += [w_uk[:, h * MLA_NOPE:(h + 1) * MLA_NOPE], pad]
    return _split_w(jnp.concatenate(cols, axis=1))


def _slot_gain(g):
    return jnp.concatenate([g, jnp.zeros((MLA_SLOT - MLA_QK,), g.dtype)]).reshape(1, MLA_SLOT)


def _layout_lora(w2, a2):
    z = jnp.zeros_like(w2)
    return jnp.concatenate([jnp.concatenate([w2, z], axis=2), jnp.concatenate([z, a2], axis=2)], axis=1)


def _mla_tables(lq):
    rows = lq // GRID_W
    row = jnp.repeat(jnp.arange(rows, dtype=F32), GRID_W)
    col = jnp.tile(jnp.arange(GRID_W, dtype=F32), rows)
    axis_dim = MLA_ROPE // 2
    inv_axis = ROPE_BASE ** (-jnp.arange(0, axis_dim, 2, dtype=F32) / axis_dim)
    ar = row[:, None] * inv_axis[None, :]
    ac = col[:, None] * inv_axis[None, :]
    ones = jnp.ones((lq, MLA_NOPE), F32)
    tail = jnp.ones((lq, MLA_SLOT - MLA_QK), F32)
    cos = jnp.concatenate([ones, jnp.cos(ar), jnp.cos(ar), jnp.cos(ac), jnp.cos(ac), tail], axis=1)
    sin = jnp.concatenate([0 * ones, -jnp.sin(ar), jnp.sin(ar), -jnp.sin(ac), jnp.sin(ac), 0 * tail], axis=1)
    return cos, sin


def _ret_tables(lq):
    theta = 1.0 / (RET_THETA_BASE ** jnp.linspace(0.0, 1.0, HEAD_DIM // 2, dtype=F32))
    ang = jnp.arange(lq, dtype=F32)[:, None] * theta[None, :]
    cos = jnp.tile(jnp.cos(ang), (1, 2 * N_HEADS))
    sin = jnp.tile(jnp.concatenate([-jnp.sin(ang), jnp.sin(ang)], axis=1), (1, N_HEADS))
    return cos, sin


def _ret_decay_tables(c, reverse):
    log_gamma = jnp.log1p(-jnp.exp2(-5.0 - jnp.arange(N_HEADS, dtype=F32)))
    lanes = jnp.repeat(log_gamma, HEAD_DIM)[None, :]
    j = jnp.arange(c, dtype=F32)
    if reverse:
        xi = jnp.exp((c - j)[:, None] * lanes)
        zeta = jnp.exp(j[:, None] * lanes)
        diff = j[None, :] - j[:, None]
        mask = diff > 0
    else:
        xi = jnp.exp((j + 1.0)[:, None] * lanes)
        zeta = jnp.exp((c - 1.0 - j)[:, None] * lanes)
        diff = j[:, None] - j[None, :]
        mask = diff >= 0
    dmat = jnp.where(mask[None], jnp.exp(jnp.where(mask, diff, 0.0)[None] * log_gamma[:, None, None]), 0.0)
    dec = jnp.exp(c * lanes)
    return xi, zeta, dmat, dec


def _mixers_pre(x, mod, lw, tables):
    zc, zm, zr, zt = _inproj(x, mod, lw['g_norm1'], lw['w_in'])
    y_conv = _conv_mixer(zc, lw['conv_w'], lw['g_out_conv'])
    qh, ql, kh, kl, vh, vl = _mla_prep(zm, tables['mla_cos'], tables['mla_sin'], lw['g_kv_norm'], lw['w_uk'],
                                       lw['w_uv'], lw['g_q'], lw['g_k'])
    rw = _rwkv_proj(zr, lw['rwkv_mu'], lw['rwkv_k_k'], lw['rwkv_k_a'], lw['rwkv_w0'], lw['rwkv_a0'], lw['lora'])
    return dict(y_conv=y_conv, q=(qh, ql), k=(kh, kl), v=(vh, vl), rw=rw, zt=zt)


def _rwkv_stream(rw, s_fw, s_bw, lw, emit):
    r, v, kk, gl, ks, lw0, b0, kt0, lw1, b1, kt1 = rw
    yf, s_fw = _rwkv_scan(lw0, kk, b0, kt0, v, r, s_fw, False)
    yb, s_bw = _rwkv_scan(lw1, kk, b1, kt1, v, r, s_bw, True)
    y = _rwkv_out(yf, yb, r, v, ks, gl, lw['rwkv_g2'], lw['rwkv_r_k'], lw['g_ln_x']) if emit else None
    return y, s_fw, s_bw


def _ret_stream(zt, r_fw, r_bw, lw, tables, emit):
    lq = zt.shape[1]
    c = min(RET_CHUNK, lq)
    yf, r_fw = _ret_scan(zt, tables['ret_cos'], tables['ret_sin'], *_ret_decay_tables(c, False), r_fw, False)
    yb, r_bw = _ret_scan(zt, tables['ret_cos'], tables['ret_sin'], *_ret_decay_tables(c, True), r_bw, True)
    y = _ret_out(yf, yb, zt, lw['g_ret_norm']) if emit else None
    return y, r_fw, r_bw


def _channel_mix(ys, x, mod, lw):
    x, hh, hl, logits_t = _outproj(ys, x, mod, lw['w_out'], lw['g_norm2'], lw['w_router_t'], lw['b_router'])
    cap = CAP_FACTOR * x.shape[1] // N_EXP
    rank, gate = _route(logits_t, cap)
    xh, xl, gr = _gather(hh, hl, rank, gate, cap)
    yh, yl = _expert_ffn(xh, xl, gr, lw['w_gate'], lw['w_up'], lw['w_down'])
    return _combine(x, mod, rank, yh, yl, cap)


def kernel(x, c, ctx, c_ctx, w_ada, b_ada, g_norm1, g_norm2, w_in, w_out, conv_w, g_out_conv, g_kv_norm, w_uk, w_uv, g_q_norm, g_k_norm, g_out_mla, rwkv_mu, rwkv_w0, rwkv_w2, rwkv_a0, rwkv_a2, rwkv_g2, rwkv_k_k, rwkv_k_a, rwkv_r_k, g_ln_x, g_ret_norm, w_router, b_router, w_gate, w_up, w_down):
    depth = w_in.shape[0]
    bsz, lq, d = x.shape
    lc = ctx.shape[1]

    cc = jnp.zeros((16, d), F32).at[:bsz].set(c).at[bsz].set(c_ctx)
    mods = _mods(cc, w_ada, b_ada)

    cos_x, sin_x = _mla_tables(lq)
    rcos_x, rsin_x = _ret_tables(lq)
    tab_x = dict(mla_cos=cos_x, mla_sin=sin_x, ret_cos=rcos_x, ret_sin=rsin_x)
    tab_c = dict(mla_cos=jnp.ones((lc, MLA_SLOT), F32), mla_sin=jnp.zeros((lc, MLA_SLOT), F32),
                 ret_cos=jnp.ones((lc, GROUP_W), F32), ret_sin=jnp.zeros((lc, GROUP_W), F32))
    zero_state = jnp.zeros((bsz, GROUP_W, GROUP_W), F32)

    xc = ctx
    for l in range(depth):
        need_ctx = l < depth - 1
        lw = dict(
            g_norm1=g_norm1[l], g_norm2=g_norm2[l], w_in=_layout_w_in(w_in[l]), w_out=_split_w(w_out[l]),
            conv_w=conv_w[l], g_out_conv=g_out_conv[l], g_kv_norm=g_kv_norm[l],
            w_uk=_layout_w_uk(w_uk[l]), w_uv=_split_w(w_uv[l]),
            g_q=_slot_gain(g_q_norm[l]), g_k=_slot_gain(g_k_norm[l]), g_out_mla=g_out_mla[l],
            rwkv_mu=rwkv_mu[l], rwkv_w0=rwkv_w0[l], rwkv_a0=rwkv_a0[l],
            lora=_layout_lora(rwkv_w2[l], rwkv_a2[l]), rwkv_g2=_split_w(rwkv_g2[l]),
            rwkv_k_k=rwkv_k_k[l], rwkv_k_a=rwkv_k_a[l], rwkv_r_k=rwkv_r_k[l], g_ln_x=g_ln_x[l],
            g_ret_norm=g_ret_norm[l], w_router_t=w_router[l].T, b_router=b_router[l],
            w_gate=_split_w(w_gate[l]), w_up=_split_w(w_up[l]), w_down=_split_w(w_down[l]))
        mod_x = mods[l, :bsz].reshape(bsz, 6, d)
        mod_c = jnp.broadcast_to(mods[l, bsz].reshape(1, 6, d), (bsz, 6, d))

        pc = _mixers_pre(xc, mod_c, lw, tab_c)
        px = _mixers_pre(x, mod_x, lw, tab_x)

        cat = lambda a, b_: tuple(jnp.concatenate([u, w_], axis=1) for u, w_ in zip(a, b_))
        ym_x = _attention(px['q'], cat(pc['k'], px['k']), cat(pc['v'], px['v']), lw['g_out_mla'])
        yr_c, s_fw, s_bw = _rwkv_stream(pc['rw'], zero_state, zero_state, lw, need_ctx)
        yr_x, _, _ = _rwkv_stream(px['rw'], s_fw, s_bw, lw, True)
        yt_c, r_fw, r_bw = _ret_stream(pc['zt'], zero_state, zero_state, lw, tab_c, need_ctx)
        yt_x, _, _ = _ret_stream(px['zt'], r_fw, r_bw, lw, tab_x, True)

        x = _channel_mix((px['y_conv'], ym_x, yr_x, yt_x), x, mod_x, lw)
        if need_ctx:
            ym_c = _attention(pc['q'], pc['k'], pc['v'], lw['g_out_mla'])
            xc = _channel_mix((pc['y_conv'], ym_c, yr_c, yt_c), xc, mod_c, lw)
    return x
```

```python
import functools

import jax
import jax.numpy as jnp
from jax import lax
from jax.experimental import pallas as pl
from jax.experimental.pallas import tpu as pltpu

F32 = jnp.float32
BF16 = jnp.bfloat16
HI = lax.Precision.HIGHEST

D_MODEL = 1024
N_HEADS = 4
HEAD_DIM = 64
GROUP_W = N_HEADS * HEAD_DIM
NORM_EPS = 1e-6
MLA_NOPE = 64
MLA_ROPE = 32
MLA_QK = MLA_NOPE + MLA_ROPE
MLA_SLOT = 128
KV_RANK = 128
ROPE_BASE = 10000.0
DECAY_LORA = 64
ICLR_LORA = 64
GATE_LORA = 128
RWKV_GN_EPS = 64e-5
RWKV_CHUNK = 64
RWKV_INV_BASE = 16
RET_CHUNK = 256
RET_THETA_BASE = 10000.0
N_EXP = 16
CAP_FACTOR = 2
GRID_W = 64

COLS_CONV = 3 * GROUP_W
COLS_MLA_IN = N_HEADS * MLA_QK + KV_RANK + MLA_ROPE
COLS_MLA = N_HEADS * MLA_SLOT + MLA_SLOT + KV_RANK
COLS_RWKV = 3 * GROUP_W + DECAY_LORA + ICLR_LORA + GATE_LORA
COLS_RET = 4 * GROUP_W

TOKEN_TILE = 256
MOE_TOKEN_SLICE = 1024
VMEM_LIMIT = 56 * 1024 * 1024

NN = (((1,), (0,)), ((), ()))
NT = (((1,), (1,)), ((), ()))
TN = (((0,), (0,)), ((), ()))


def _cparams(*sem):
    return pltpu.CompilerParams(dimension_semantics=sem, vmem_limit_bytes=VMEM_LIMIT)


def _dg(a, b, dims=NN, precision=None):
    return lax.dot_general(a, b, dims, precision=precision, preferred_element_type=F32)


def _split(a):
    hi = a.astype(BF16)
    return hi, (a - hi.astype(F32)).astype(BF16)


def _mm3(a, b, dims=NN):
    ah, al = a if isinstance(a, tuple) else _split(a)
    bh, bl = b if isinstance(b, tuple) else _split(b)
    return _dg(ah, bh, dims) + _dg(ah, bl, dims) + _dg(al, bh, dims)


def _mm(a, b, dims=NN):
    return _dg(a.astype(BF16), b.astype(BF16), dims)


def _head_ones(n, width, scale=1.0):
    r = lax.broadcasted_iota(jnp.int32, (n, n), 0) // width
    c = lax.broadcasted_iota(jnp.int32, (n, n), 1) // width
    return jnp.where(r == c, scale, 0.0).astype(F32)


def _lane_head(shape, width):
    return lax.broadcasted_iota(jnp.int32, shape, len(shape) - 1) // width


def _sigmoid(x):
    return 1.0 / (1.0 + jnp.exp(-x))


def _mods_kernel(cc_ref, w_ref, b_ref, o_ref):
    cc = cc_ref[...]
    o_ref[0] = _dg(cc * _sigmoid(cc), w_ref[0], NN, HI) + b_ref[0]


def _mods(cc, w_ada, b_ada):
    depth, d, n = w_ada.shape
    tn = 1536
    return pl.pallas_call(
        _mods_kernel,
        grid=(depth, n // tn),
        in_specs=[pl.BlockSpec(cc.shape, lambda l, j: (0, 0)),
                  pl.BlockSpec((1, d, tn), lambda l, j: (l, 0, j)),
                  pl.BlockSpec((1, 1, tn), lambda l, j: (l, 0, j))],
        out_specs=pl.BlockSpec((1, cc.shape[0], tn), lambda l, j: (l, 0, j)),
        out_shape=jax.ShapeDtypeStruct((depth, cc.shape[0], n), F32),
        compiler_params=_cparams("parallel", "parallel"),
        name="adaln_mods",
    )(cc, w_ada, b_ada.reshape(depth, 1, n))


def _inproj_kernel(x_ref, mod_ref, g_ref, w_ref, zc_ref, zm_ref, zr_ref, zt_ref):
    x = x_ref[0]
    m = mod_ref[0]
    xn = x * lax.rsqrt(jnp.mean(x * x, axis=-1, keepdims=True) + NORM_EPS) * g_ref[...]
    h = (xn * (1.0 + m[1:2]) + m[0:1]).astype(BF16)
    o = 0
    for ref, n in ((zc_ref, COLS_CONV), (zm_ref, COLS_MLA), (zr_ref, COLS_RWKV), (zt_ref, COLS_RET)):
        ref[0] = _dg(h, w_ref[:, o:o + n])
        o += n


def _inproj(x, mod, g, w):
    b, lq, d = x.shape
    tm = min(TOKEN_TILE, lq)
    widths = (COLS_CONV, COLS_MLA, COLS_RWKV, COLS_RET)
    return pl.pallas_call(
        _inproj_kernel,
        grid=(b, lq // tm),
        in_specs=[pl.BlockSpec((1, tm, d), lambda i, j: (i, j, 0)),
                  pl.BlockSpec((1, 6, d), lambda i, j: (i, 0, 0)),
                  pl.BlockSpec((1, d), lambda i, j: (0, 0)),
                  pl.BlockSpec(w.shape, lambda i, j: (0, 0))],
        out_specs=[pl.BlockSpec((1, tm, n), lambda i, j: (i, j, 0)) for n in widths],
        out_shape=[jax.ShapeDtypeStruct((b, lq, n), F32) for n in widths],
        compiler_params=_cparams("parallel", "parallel"),
        name="inproj",
    )(x, mod, g.reshape(1, d), w)


def _halo_specs(tm, lq, width):
    r = tm // 8
    last = lq // 8 - 1
    return [pl.BlockSpec((1, tm, width), lambda i, j: (i, j, 0)),
            pl.BlockSpec((1, 8, width), lambda i, j: (i, jnp.maximum(j * r - 1, 0), 0)),
            pl.BlockSpec((1, 8, width), lambda i, j: (i, jnp.minimum((j + 1) * r, last), 0))]


def _shifted(t, prev_row, next_row):
    n = t.shape[0]
    rows = lax.broadcasted_iota(jnp.int32, t.shape, 0)
    t_prev = jnp.where(rows == 0, prev_row, pltpu.roll(t, 1, 0))
    t_next = jnp.where(rows == n - 1, next_row, pltpu.roll(t, n - 1, 0))
    return t_prev, t_next


def _conv_kernel(z_ref, zp_ref, zn_ref, cw_ref, g_ref, o_ref, *, nt):
    j = pl.program_id(1)
    w = GROUP_W
    z = z_ref[0]
    bgate = z[:, :w]
    u = z[:, w:2 * w] * z[:, 2 * w:]
    zp = zp_ref[0][7:8]
    zn = zn_ref[0][0:1]
    up = jnp.where(j > 0, zp[:, w:2 * w] * zp[:, 2 * w:], 0.0)
    un = jnp.where(j < nt - 1, zn[:, w:2 * w] * zn[:, 2 * w:], 0.0)
    u_prev, u_next = _shifted(u, up, un)
    cw = cw_ref[...]
    t = bgate * (cw[0:1] * u_prev + cw[1:2] * u + cw[2:3] * u_next)
    o_ref[0] = t * lax.rsqrt(jnp.mean(t * t, axis=-1, keepdims=True) + NORM_EPS) * g_ref[...]


def _conv_mixer(zc, conv_w, g_out):
    b, lq, width = zc.shape
    tm = min(TOKEN_TILE, lq)
    return pl.pallas_call(
        functools.partial(_conv_kernel, nt=lq // tm),
        grid=(b, lq // tm),
        in_specs=_halo_specs(tm, lq, width) + [
            pl.BlockSpec((3, GROUP_W), lambda i, j: (0, 0)),
            pl.BlockSpec((1, GROUP_W), lambda i, j: (0, 0))],
        out_specs=pl.BlockSpec((1, tm, GROUP_W), lambda i, j: (i, j, 0)),
        out_shape=jax.ShapeDtypeStruct((b, lq, GROUP_W), F32),
        compiler_params=_cparams("parallel", "parallel"),
        name="conv_mixer",
    )(zc, zc, zc, conv_w, g_out.reshape(1, GROUP_W))


def _tile4(t):
    return jnp.concatenate([t, t, t, t], axis=-1)


def _slot_norm_rope(t, g, cos, sin):
    parts = []
    for h in range(N_HEADS):
        s = t[:, h * MLA_SLOT:(h + 1) * MLA_SLOT]
        ms = jnp.sum(s * s, axis=-1, keepdims=True) * (1.0 / MLA_QK)
        parts.append(s * lax.rsqrt(ms + NORM_EPS) * g)
    tn = jnp.concatenate(parts, axis=-1)
    n = tn.shape[-1]
    lane = lax.broadcasted_iota(jnp.int32, tn.shape, 1)
    half = MLA_ROPE // 4
    partner = jnp.where(lane % (2 * half) < half, pltpu.roll(tn, n - half, 1), pltpu.roll(tn, half, 1))
    return tn * _tile4(cos) + partner * _tile4(sin)


def _mla_prep_kernel(z_ref, cos_ref, sin_ref, gkv_ref, wukh_ref, wukl_ref, wuvh_ref, wuvl_ref, gq_ref, gk_ref,
                     q_ref, k_ref, v_ref):
    z = z_ref[0]
    nq = N_HEADS * MLA_SLOT
    q_in = z[:, :nq]
    k_rope = z[:, nq:nq + MLA_SLOT]
    ckv = z[:, nq + MLA_SLOT:]
    ckv = _split(ckv * lax.rsqrt(jnp.mean(ckv * ckv, axis=-1, keepdims=True) + NORM_EPS) * gkv_ref[...])
    k_in = _mm3(ckv, (wukh_ref[...], wukl_ref[...])) + _tile4(k_rope)
    cos = cos_ref[...]
    sin = sin_ref[...]
    q_ref[0] = (_slot_norm_rope(q_in, gq_ref[...], cos, sin) * (MLA_QK ** -0.5)).astype(BF16)
    k_ref[0] = _slot_norm_rope(k_in, gk_ref[...], cos, sin).astype(BF16)
    v_ref[0] = _mm3(ckv, (wuvh_ref[...], wuvl_ref[...])).astype(BF16)


def _mla_prep(zm, cos, sin, g_kv, wuk, wuv, g_q, g_k):
    b, lq, width = zm.shape
    tm = min(TOKEN_TILE, lq)
    nq = N_HEADS * MLA_SLOT
    const = lambda shape: pl.BlockSpec(shape, lambda i, j: (0, 0))
    tok = lambda i, j: (i, j, 0)
    return pl.pallas_call(
        _mla_prep_kernel,
        grid=(b, lq // tm),
        in_specs=[pl.BlockSpec((1, tm, width), tok),
                  pl.BlockSpec((tm, MLA_SLOT), lambda i, j: (j, 0)),
                  pl.BlockSpec((tm, MLA_SLOT), lambda i, j: (j, 0)),
                  const((1, KV_RANK)), const(wuk[0].shape), const(wuk[0].shape),
                  const(wuv[0].shape), const(wuv[0].shape),
                  const((1, MLA_SLOT)), const((1, MLA_SLOT))],
        out_specs=[pl.BlockSpec((1, tm, nq), tok), pl.BlockSpec((1, tm, nq), tok), pl.BlockSpec((1, tm, GROUP_W), tok)],
        out_shape=[jax.ShapeDtypeStruct((b, lq, nq), BF16)] * 2 + [jax.ShapeDtypeStruct((b, lq, GROUP_W), BF16)],
        compiler_params=_cparams("parallel", "parallel"),
        name="mla_prep",
    )(zm, cos, sin, g_kv.reshape(1, KV_RANK), *wuk, *wuv, g_q, g_k)


def _attn_kernel(q_ref, k_ref, v_ref, g_ref, o_ref):
    v = v_ref[0]
    tq = q_ref.shape[1]
    head = _lane_head((tq, GROUP_W), HEAD_DIM)
    out = jnp.zeros((tq, GROUP_W), F32)
    for h in range(N_HEADS):
        sl = slice(h * MLA_SLOT, (h + 1) * MLA_SLOT)
        s = _dg(q_ref[0, :, sl], k_ref[0, :, sl], NT)
        p = jnp.exp(s - jnp.max(s, axis=-1, keepdims=True))
        inv = 1.0 / jnp.sum(p, axis=-1, keepdims=True)
        out = jnp.where(head == h, _mm(p, v) * inv, out)
    o_ref[0] = out * lax.rsqrt(jnp.mean(out * out, axis=-1, keepdims=True) + NORM_EPS) * g_ref[...]


def _attention(q, k, v, g_out):
    b, lq, nq = q.shape
    lk = k.shape[1]
    tq = min(TOKEN_TILE, lq)
    return pl.pallas_call(
        _attn_kernel,
        grid=(b, lq // tq),
        in_specs=[pl.BlockSpec((1, tq, nq), lambda i, j: (i, j, 0)),
                  pl.BlockSpec((1, lk, nq), lambda i, j: (i, 0, 0)),
                  pl.BlockSpec((1, lk, GROUP_W), lambda i, j: (i, 0, 0)),
                  pl.BlockSpec((1, GROUP_W), lambda i, j: (0, 0))],
        out_specs=pl.BlockSpec((1, tq, GROUP_W), lambda i, j: (i, j, 0)),
        out_shape=jax.ShapeDtypeStruct((b, lq, GROUP_W), F32),
        compiler_params=_cparams("parallel", "arbitrary"),
        name="mla_attention",
    )(q, k, v, g_out.reshape(1, GROUP_W))


def _rwkv_proj_kernel(z_ref, zp_ref, zn_ref, mu_ref, kk_ref, ka_ref, w0_ref, a0_ref, lora_ref,
                      r_ref, v_ref, kkn_ref, gl_ref, ks_ref,
                      lw0_ref, b0_ref, kt0_ref, lw1_ref, b1_ref, kt1_ref, *, nt):
    j = pl.program_id(1)
    w = GROUP_W
    z = z_ref[0]
    zp = jnp.where(j > 0, zp_ref[0][7:8], 0.0)
    zn = jnp.where(j < nt - 1, zn_ref[0][0:1], 0.0)
    z_prev, z_next = _shifted(z, zp, zn)
    z = z + mu_ref[...] * (0.5 * (z_prev + z_next) - z)
    r = z[:, :w]
    k = z[:, w:2 * w]
    v = z[:, 2 * w:3 * w]
    lora_in = z[:, 3 * w:3 * w + DECAY_LORA + ICLR_LORA]
    gl = z[:, 3 * w + DECAY_LORA + ICLR_LORA:]
    kk = k * kk_ref[...]
    kk = kk * lax.rsqrt(_dg(kk * kk, _head_ones(w, HEAD_DIM), NN, HI) + 1e-12)
    lane = lax.broadcasted_iota(jnp.int32, lora_in.shape, 1)
    lora_in = jnp.where(lane < DECAY_LORA, jnp.tanh(lora_in), lora_in)
    r_ref[0] = r
    v_ref[0] = v
    kkn_ref[0] = kk
    gl_ref[0] = gl
    ksum = jnp.zeros_like(k)
    for d, (lw_ref, b_ref, kt_ref) in enumerate(((lw0_ref, b0_ref, kt0_ref), (lw1_ref, b1_ref, kt1_ref))):
        lo = _dg(lora_in, lora_ref[d], NN, HI)
        t = -(w0_ref[d:d + 1] + lo[:, :w])
        softplus = jnp.maximum(t, 0.0) + jnp.log1p(jnp.exp(-jnp.abs(t)))
        lw_ref[0] = -jnp.exp(-softplus - 0.5)
        a = _sigmoid(a0_ref[d:d + 1] + lo[:, w:])
        kt = k * (1.0 + (a - 1.0) * ka_ref[...])
        b_ref[0] = kk * a
        kt_ref[0] = kt
        ksum = ksum + kt
    ks_ref[0] = ksum


def _rwkv_proj(zr, mu, k_k, k_a, w0, a0, lora):
    b, lq, width = zr.shape
    tm = min(TOKEN_TILE, lq)
    w = GROUP_W
    const2 = lambda shape: pl.BlockSpec(shape, lambda i, j: (0,) * len(shape))
    out_w = (w, w, w, GATE_LORA, w, w, w, w, w, w, w)
    return pl.pallas_call(
        functools.partial(_rwkv_proj_kernel, nt=lq // tm),
        grid=(b, lq // tm),
        in_specs=_halo_specs(tm, lq, width) + [
            const2((1, width)), const2((1, w)), const2((1, w)), const2((2, w)), const2((2, w)),
            const2(lora.shape)],
        out_specs=[pl.BlockSpec((1, tm, n), lambda i, j: (i, j, 0)) for n in out_w],
        out_shape=[jax.ShapeDtypeStruct((b, lq, n), F32) for n in out_w],
        compiler_params=_cparams("parallel", "parallel"),
        name="rwkv_proj",
    )(zr, zr, zr, mu.reshape(1, width), k_k.reshape(1, w), k_a.reshape(1, w), w0, a0, lora)


def _stack_heads(t):
    head = _lane_head(t.shape, HEAD_DIM)
    return jnp.concatenate([jnp.where(head == h, t, 0.0) for h in range(N_HEADS)], axis=0)


def _unit_lower_inverse(n_mat, p_i, q_i, size):
    base = RWKV_INV_BASE
    same = lambda w: (p_i // w) == (q_i // w)
    m = jnp.where(same(base), -n_mat, 0.0)
    inv = jnp.where(p_i == q_i, 1.0, 0.0) + m
    pw = m
    span = 1
    while 2 * span < base:
        pw = _mm(pw, pw)
        inv = inv + _mm(inv, pw)
        span *= 2
    w = base
    while w < size:
        off = jnp.where(same(2 * w) & jnp.logical_not(same(w)), n_mat, 0.0)
        invb = inv.astype(BF16)
        inv = inv - _mm(_mm(invb, off), invb)
        w *= 2
    return inv


def _rwkv_scan_kernel(lw_ref, kk_ref, b_ref, kt_ref, v_ref, r_ref, s0_ref, y_ref, sout_ref, s_ref,
                      *, reverse, nc):
    i = pl.program_id(1)

    @pl.when(i == 0)
    def _():
        s_ref[...] = s0_ref[0]

    c = RWKV_CHUNK
    n = N_HEADS * c
    lw = lw_ref[0]
    t_i = lax.broadcasted_iota(jnp.int32, (c, c), 0)
    s_i = lax.broadcasted_iota(jnp.int32, (c, c), 1)
    seen = (s_i >= t_i) if reverse else (s_i <= t_i)
    g = _dg(jnp.where(seen, 1.0, 0.0).astype(F32), lw, NN, HI)
    g_tot = g[0:1] if reverse else g[c - 1:c]
    e_neg = jnp.exp(-g)
    e_tot = jnp.exp(g_tot)
    kkd = _stack_heads(kk_ref[0] * jnp.exp(g - lw)).astype(BF16)
    rd = _stack_heads(r_ref[0] * jnp.exp(g)).astype(BF16)
    bd = _stack_heads(b_ref[0] * e_neg)
    ktd = _stack_heads(kt_ref[0] * e_neg)
    vs = _stack_heads(v_ref[0]).astype(BF16)
    bds = bd.astype(BF16)
    ktds = ktd.astype(BF16)

    p_i = lax.broadcasted_iota(jnp.int32, (n, n), 0)
    q_i = lax.broadcasted_iota(jnp.int32, (n, n), 1)
    strict = (q_i > p_i) if reverse else (q_i < p_i)
    incl = (q_i >= p_i) if reverse else (q_i <= p_i)
    n_ab = jnp.where(strict, _mm(kkd, bds, NT), 0.0)
    n_ak = jnp.where(strict, _mm(kkd, ktds, NT), 0.0)
    a_rb = jnp.where(incl, _mm(rd, bds, NT), 0.0)
    a_rk = jnp.where(incl, _mm(rd, ktds, NT), 0.0)
    t_inv = _unit_lower_inverse(n_ab, p_i, q_i, c)

    s = s_ref[...]
    ss = s.astype(BF16)
    u = (-_mm(t_inv, _mm(kkd, ss, NT) + _mm(n_ak, vs))).astype(BF16)
    y = _mm(rd, ss, NT) + _mm(a_rb, u) + _mm(a_rk, vs)
    y_ref[0] = y[0:c] + y[c:2 * c] + y[2 * c:3 * c] + y[3 * c:]
    s_new = s * e_tot + _mm(u, bd * e_tot, TN) + _mm(vs, ktd * e_tot, TN)
    s_ref[...] = s_new

    @pl.when(i == nc - 1)
    def _():
        sout_ref[0] = s_new


def _rwkv_scan(lw, kk, bb, kt, v, r, s0, reverse):
    b, lq, w = lw.shape
    c = RWKV_CHUNK
    nc = lq // c
    tok = (lambda i, j: (i, nc - 1 - j, 0)) if reverse else (lambda i, j: (i, j, 0))
    tspec = pl.BlockSpec((1, c, w), tok)
    sspec = pl.BlockSpec((1, w, w), lambda i, j: (i, 0, 0))
    return pl.pallas_call(
        functools.partial(_rwkv_scan_kernel, reverse=reverse, nc=nc),
        grid=(b, nc),
        in_specs=[tspec] * 6 + [sspec],
        out_specs=[tspec, sspec],
        out_shape=[jax.ShapeDtypeStruct((b, lq, w), F32), jax.ShapeDtypeStruct((b, w, w), F32)],
        scratch_shapes=[pltpu.VMEM((w, w), F32)],
        compiler_params=_cparams("parallel", "arbitrary"),
        name="rwkv_scan_bw" if reverse else "rwkv_scan_fw",
    )(lw, kk, bb, kt, v, r, s0)


def _rwkv_out_kernel(yf_ref, yb_ref, r_ref, v_ref, ks_ref, gl_ref, g2h_ref, g2l_ref, rk_ref, gln_ref, o_ref):
    y = yf_ref[0] + yb_ref[0]
    avg = _head_ones(GROUP_W, HEAD_DIM, 1.0 / HEAD_DIM)
    dlt = y - _dg(y, avg, NN, HI)
    var = _dg(dlt * dlt, avg, NN, HI)
    yn = dlt * lax.rsqrt(var + RWKV_GN_EPS) * gln_ref[...]
    bonus = _dg(r_ref[0] * ks_ref[0] * rk_ref[...], _head_ones(GROUP_W, HEAD_DIM), NN, HI)
    yn = yn + bonus * v_ref[0]
    o_ref[0] = yn * _mm3(_sigmoid(gl_ref[0]), (g2h_ref[...], g2l_ref[...]))


def _rwkv_out(yf, yb, r, v, ks, gl, g2, r_k, g_ln):
    b, lq, w = yf.shape
    tm = min(TOKEN_TILE, lq)
    tspec = pl.BlockSpec((1, tm, w), lambda i, j: (i, j, 0))
    const = lambda shape: pl.BlockSpec(shape, lambda i, j: (0, 0))
    return pl.pallas_call(
        _rwkv_out_kernel,
        grid=(b, lq // tm),
        in_specs=[tspec] * 5 + [pl.BlockSpec((1, tm, GATE_LORA), lambda i, j: (i, j, 0)),
                                const(g2[0].shape), const(g2[0].shape), const((1, w)), const((1, w))],
        out_specs=tspec,
        out_shape=jax.ShapeDtypeStruct((b, lq, w), F32),
        compiler_params=_cparams("parallel", "parallel"),
        name="rwkv_out",
    )(yf, yb, r, v, ks, gl, *g2, r_k.reshape(1, w), g_ln.reshape(1, w))


def _half_rotate(t, cos, sin):
    n = t.shape[-1]
    lane = lax.broadcasted_iota(jnp.int32, t.shape, 1)
    half = HEAD_DIM // 2
    partner = jnp.where(lane % HEAD_DIM < half, pltpu.roll(t, n - half, 1), pltpu.roll(t, half, 1))
    return t * cos + partner * sin


def _ret_scan_kernel(z_ref, cos_ref, sin_ref, xi_ref, zeta_ref, dmat_ref, dec_ref, r0_ref,
                     y_ref, rout_ref, st_ref, *, nc):
    i = pl.program_id(1)

    @pl.when(i == 0)
    def _():
        st_ref[...] = r0_ref[0]

    w = GROUP_W
    z = z_ref[0]
    cos = cos_ref[...]
    sin = sin_ref[...]
    q = _half_rotate(z[:, :w] * (HEAD_DIM ** -0.5), cos, sin)
    k = _half_rotate(z[:, w:2 * w], cos, sin)
    vs = z[:, 2 * w:3 * w].astype(BF16)
    ks = k.astype(BF16)
    head = _lane_head(q.shape, HEAD_DIM)
    st = st_ref[...]
    y = _mm(q * xi_ref[...], st)
    for h in range(N_HEADS):
        s = _mm(jnp.where(head == h, q, 0.0), ks, NT) * dmat_ref[h]
        y = y + jnp.where(head == h, _mm(s, vs), 0.0)
    y_ref[0] = y
    st_new = st * dec_ref[...] + _mm(k * zeta_ref[...], vs, TN) * _head_ones(w, HEAD_DIM)
    st_ref[...] = st_new

    @pl.when(i == nc - 1)
    def _():
        rout_ref[0] = st_new


def _ret_scan(zt, cos, sin, xi, zeta, dmat, dec, r0, reverse):
    b, lq, width = zt.shape
    c = min(RET_CHUNK, lq)
    nc = lq // c
    w = GROUP_W
    tok3 = (lambda i, j: (i, nc - 1 - j, 0)) if reverse else (lambda i, j: (i, j, 0))
    tok2 = (lambda i, j: (nc - 1 - j, 0)) if reverse else (lambda i, j: (j, 0))
    const = lambda shape: pl.BlockSpec(shape, lambda i, j: (0,) * len(shape))
    sspec = pl.BlockSpec((1, w, w), lambda i, j: (i, 0, 0))
    return pl.pallas_call(
        functools.partial(_ret_scan_kernel, nc=nc),
        grid=(b, nc),
        in_specs=[pl.BlockSpec((1, c, width), tok3),
                  pl.BlockSpec((c, w), tok2), pl.BlockSpec((c, w), tok2),
                  const((c, w)), const((c, w)), const((N_HEADS, c, c)), const((1, w)), sspec],
        out_specs=[pl.BlockSpec((1, c, w), tok3), sspec],
        out_shape=[jax.ShapeDtypeStruct((b, lq, w), F32), jax.ShapeDtypeStruct((b, w, w), F32)],
        scratch_shapes=[pltpu.VMEM((w, w), F32)],
        compiler_params=_cparams("parallel", "arbitrary"),
        name="ret_scan_bw" if reverse else "ret_scan_fw",
    )(zt, cos, sin, xi, zeta, dmat, dec, r0)


def _ret_out_kernel(yf_ref, yb_ref, z_ref, g_ref, o_ref):
    y = yf_ref[0] + yb_ref[0]
    ms = _dg(y * y, _head_ones(GROUP_W, HEAD_DIM, 1.0 / HEAD_DIM), NN, HI)
    gate = z_ref[0]
    o_ref[0] = gate * _sigmoid(gate) * (y * lax.rsqrt(ms + NORM_EPS) * g_ref[...])


def _ret_out(yf, yb, zt, g_norm):
    b, lq, w = yf.shape
    tm = min(TOKEN_TILE, lq)
    tspec = pl.BlockSpec((1, tm, w), lambda i, j: (i, j, 0))
    return pl.pallas_call(
        _ret_out_kernel,
        grid=(b, lq // tm),
        in_specs=[tspec, tspec, pl.BlockSpec((1, tm, w), lambda i, j: (i, j, 3)),
                  pl.BlockSpec((1, w), lambda i, j: (0, 0))],
        out_specs=tspec,
        out_shape=jax.ShapeDtypeStruct((b, lq, w), F32),
        compiler_params=_cparams("parallel", "parallel"),
        name="ret_out",
    )(yf, yb, zt, g_norm.reshape(1, w))


def _outproj_kernel(yc_ref, ym_ref, yr_ref, yt_ref, x_ref, mod_ref, w_ref, g_ref, wr_ref, br_ref,
                    xo_ref, h_ref, lg_ref):
    w = GROUP_W
    acc = None
    for n, ref in enumerate((yc_ref, ym_ref, yr_ref, yt_ref)):
        rows = slice(n * w, (n + 1) * w)
        part = _mm(ref[0], w_ref[rows, :])
        acc = part if acc is None else acc + part
    m = mod_ref[0]
    x = x_ref[0] + m[2:3] * acc
    xo_ref[0] = x
    xn = x * lax.rsqrt(jnp.mean(x * x, axis=-1, keepdims=True) + NORM_EPS) * g_ref[...]
    h = xn * (1.0 + m[4:5]) + m[3:4]
    h_ref[0] = h.astype(BF16)
    lg_ref[0] = _dg(wr_ref[...], h, NT, HI) + br_ref[...]


def _outproj(ys, x, mod, w_out, g2, wr_t, b_r):
    b, lq, d = x.shape
    tm = min(TOKEN_TILE, lq)
    yspec = pl.BlockSpec((1, tm, GROUP_W), lambda i, j: (i, j, 0))
    xspec = pl.BlockSpec((1, tm, d), lambda i, j: (i, j, 0))
    const = lambda shape: pl.BlockSpec(shape, lambda i, j: (0, 0))
    return pl.pallas_call(
        _outproj_kernel,
        grid=(b, lq // tm),
        in_specs=[yspec] * 4 + [xspec, pl.BlockSpec((1, 6, d), lambda i, j: (i, 0, 0)),
                                const(w_out.shape), const((1, d)),
                                const(wr_t.shape), const((N_EXP, 1))],
        out_specs=[xspec, xspec, pl.BlockSpec((1, N_EXP, tm), lambda i, j: (i, 0, j))],
        out_shape=[jax.ShapeDtypeStruct((b, lq, d), F32), jax.ShapeDtypeStruct((b, lq, d), BF16),
                   jax.ShapeDtypeStruct((b, N_EXP, lq), F32)],
        compiler_params=_cparams("parallel", "parallel"),
        name="outproj",
    )(*ys, x, mod, w_out, g2.reshape(1, d), wr_t, b_r.reshape(N_EXP, 1))


def _lane_cumsum(m):
    n = m.shape[-1]
    blk = 128
    r_i = lax.broadcasted_iota(jnp.int32, (blk, blk), 0)
    c_i = lax.broadcasted_iota(jnp.int32, (blk, blk), 1)
    tri = jnp.where(r_i <= c_i, 1.0, 0.0).astype(BF16)
    run = jnp.zeros((m.shape[0], 1), F32)
    parts = []
    for j in range(n // blk):
        cs = _dg(m[:, j * blk:(j + 1) * blk].astype(BF16), tri) + run
        parts.append(cs)
        run = cs[:, blk - 1:blk]
    return jnp.concatenate(parts, axis=-1)


def _route_kernel(lg_ref, rank_ref, gate_ref, *, cap):
    lg = lg_ref[0]
    e = jnp.exp(lg - jnp.max(lg, axis=0, keepdims=True))
    aff = e / jnp.sum(e, axis=0, keepdims=True)
    bits = pltpu.bitcast(aff, jnp.int32)
    thr = jnp.zeros((aff.shape[0], 1), jnp.int32)
    for bit in range(30, -1, -1):
        cand = thr | (1 << bit)
        cnt = jnp.sum(jnp.where(bits >= cand, 1.0, 0.0), axis=-1, keepdims=True)
        thr = jnp.where(cnt >= cap, cand, thr)
    above = bits > thr
    tied = bits == thr
    n_above = jnp.sum(jnp.where(above, 1.0, 0.0), axis=-1, keepdims=True)
    tied_f = jnp.where(tied, 1.0, 0.0)
    tied_rank = _lane_cumsum(tied_f) - tied_f
    sel = above | (tied & (tied_rank < cap - n_above))
    sel_f = jnp.where(sel, 1.0, 0.0)
    rank = _lane_cumsum(sel_f) - sel_f
    rank_ref[0] = jnp.where(sel, rank, -1.0).astype(jnp.int32)
    gate_ref[0] = jnp.where(sel, aff, 0.0)


def _route(logits_t, cap):
    b, ne, lq = logits_t.shape
    spec = pl.BlockSpec((1, ne, lq), lambda i: (i, 0, 0))
    return pl.pallas_call(
        functools.partial(_route_kernel, cap=cap),
        grid=(b,),
        in_specs=[spec],
        out_specs=[spec, spec],
        out_shape=[jax.ShapeDtypeStruct((b, ne, lq), jnp.int32), jax.ShapeDtypeStruct((b, ne, lq), F32)],
        compiler_params=_cparams("parallel"),
        name="moe_route",
    )(logits_t)


def _onehot_rows(rank_row, cap):
    r_i = lax.broadcasted_iota(jnp.int32, (cap, rank_row.shape[-1]), 0)
    return r_i == rank_row


def _gather_kernel(h_ref, rank_ref, gate_ref, xe_ref, gr_ref, *, cap):
    lq = h_ref.shape[1]
    ts = min(MOE_TOKEN_SLICE, lq)
    xe = gr = None
    for t in range(lq // ts):
        tok = slice(t * ts, (t + 1) * ts)
        sel = _onehot_rows(rank_ref[0, 0, :, tok], cap)
        px = _dg(jnp.where(sel, 1.0, 0.0).astype(BF16), h_ref[0, tok, :])
        pg = jnp.sum(jnp.where(sel, gate_ref[0, 0, :, tok], 0.0), axis=-1, keepdims=True)
        xe, gr = (px, pg) if xe is None else (xe + px, gr + pg)
    xe_ref[0, 0] = xe.astype(BF16)
    gr_ref[0, 0] = gr


def _gather(h, rank, gate, cap):
    b, lq, d = h.shape
    ne = rank.shape[1]
    rspec = pl.BlockSpec((1, 1, 1, lq), lambda i, e: (i, e, 0, 0))
    xspec = pl.BlockSpec((1, 1, cap, d), lambda i, e: (i, e, 0, 0))
    return pl.pallas_call(
        functools.partial(_gather_kernel, cap=cap),
        grid=(b, ne),
        in_specs=[pl.BlockSpec((1, lq, d), lambda i, e: (i, 0, 0)), rspec, rspec],
        out_specs=[xspec, pl.BlockSpec((1, 1, cap, 1), lambda i, e: (i, e, 0, 0))],
        out_shape=[jax.ShapeDtypeStruct((b, ne, cap, d), BF16), jax.ShapeDtypeStruct((b, ne, cap, 1), F32)],
        compiler_params=_cparams("parallel", "arbitrary"),
        name="moe_gather",
    )(h, rank.reshape(b, ne, 1, lq), gate.reshape(b, ne, 1, lq))


def _ffn_kernel(xe_ref, gr_ref, wg_ref, wu_ref, wd_ref, ye_ref, wgb_ref, wub_ref, wdb_ref):
    @pl.when(pl.program_id(1) == 0)
    def _():
        wgb_ref[...] = wg_ref[0].astype(BF16)
        wub_ref[...] = wu_ref[0].astype(BF16)
        wdb_ref[...] = wd_ref[0].astype(BF16)

    xe = xe_ref[0, 0]
    a = _dg(xe, wgb_ref[...])
    hid = a * _sigmoid(a) * _dg(xe, wub_ref[...])
    ye_ref[0, 0] = (_mm(hid, wdb_ref[...]) * gr_ref[0, 0]).astype(BF16)


def _expert_ffn(xe, gr, w_gate, w_up, w_down):
    b, ne, cap, d = xe.shape
    ff = w_gate.shape[2]
    xspec = pl.BlockSpec((1, 1, cap, d), lambda e, i: (i, e, 0, 0))
    wspec = lambda shape: pl.BlockSpec((1,) + shape, lambda e, i: (e, 0, 0))
    return pl.pallas_call(
        _ffn_kernel,
        grid=(ne, b),
        in_specs=[xspec, pl.BlockSpec((1, 1, cap, 1), lambda e, i: (i, e, 0, 0)),
                  wspec((d, ff)), wspec((d, ff)), wspec((ff, d))],
        out_specs=xspec,
        out_shape=jax.ShapeDtypeStruct((b, ne, cap, d), BF16),
        scratch_shapes=[pltpu.VMEM((d, ff), BF16), pltpu.VMEM((d, ff), BF16), pltpu.VMEM((ff, d), BF16)],
        compiler_params=_cparams("parallel", "arbitrary"),
        name="moe_ffn",
    )(xe, gr, w_gate, w_up, w_down)


def _combine_kernel(x_ref, ga_ref, rank_ref, ye_ref, o_ref, *, cap):
    e = pl.program_id(2)
    lq = x_ref.shape[1]
    ts = min(MOE_TOKEN_SLICE, lq)
    for t in range(lq // ts):
        tok = slice(t * ts, (t + 1) * ts)
        p = jnp.where(_onehot_rows(rank_ref[0, 0, :, tok], cap), 1.0, 0.0).astype(BF16)
        part = _dg(p, ye_ref[0, 0], TN)

        @pl.when(e == 0)
        def _():
            o_ref[0, tok, :] = part

        @pl.when(e > 0)
        def _():
            o_ref[0, tok, :] += part

    @pl.when(e == N_EXP - 1)
    def _():
        o_ref[0] = x_ref[0] + ga_ref[0] * o_ref[0]


def _combine(x, mod, rank, ye, cap):
    b, lq, d = x.shape
    ne = rank.shape[1]
    dcol = d // 2
    xspec = pl.BlockSpec((1, lq, dcol), lambda i, j, e: (i, 0, j))
    yspec = pl.BlockSpec((1, 1, cap, dcol), lambda i, j, e: (i, e, 0, j))
    return pl.pallas_call(
        functools.partial(_combine_kernel, cap=cap),
        grid=(b, d // dcol, ne),
        in_specs=[xspec, pl.BlockSpec((1, 1, dcol), lambda i, j, e: (i, 0, j)),
                  pl.BlockSpec((1, 1, 1, lq), lambda i, j, e: (i, e, 0, 0)), yspec],
        out_specs=xspec,
        out_shape=jax.ShapeDtypeStruct((b, lq, d), F32),
        compiler_params=_cparams("parallel", "parallel", "arbitrary"),
        name="moe_combine",
    )(x, mod[:, 5:6, :], rank.reshape(b, ne, 1, lq), ye)


def _split_w(w):
    hi = w.astype(BF16)
    return hi, (w - hi.astype(F32)).astype(BF16)


def _layout_w_in(w):
    d = w.shape[0]
    o = COLS_CONV
    q_nope = w[:, o:o + N_HEADS * MLA_NOPE]
    q_rope = w[:, o + N_HEADS * MLA_NOPE:o + N_HEADS * MLA_QK]
    ckv = w[:, o + N_HEADS * MLA_QK:o + N_HEADS * MLA_QK + KV_RANK]
    k_rope = w[:, o + N_HEADS * MLA_QK + KV_RANK:o + COLS_MLA_IN]
    pad = jnp.zeros((d, MLA_SLOT - MLA_QK), w.dtype)
    cols = [w[:, :o]]
    for h in range(N_HEADS):
        cols += [q_nope[:, h * MLA_NOPE:(h + 1) * MLA_NOPE], q_rope[:, h * MLA_ROPE:(h + 1) * MLA_ROPE], pad]
    cols += [jnp.zeros((d, MLA_NOPE), w.dtype), k_rope, pad, ckv, w[:, o + COLS_MLA_IN:]]
    return jnp.concatenate(cols, axis=1).astype(BF16)


def _layout_w_uk(w_uk):
    pad = jnp.zeros((w_uk.shape[0], MLA_SLOT - MLA_NOPE), w_uk.dtype)
    cols = []
    for h in range(N_HEADS):
        cols += [w_uk[:, h * MLA_NOPE:(h + 1) * MLA_NOPE], pad]
    return _split_w(jnp.concatenate(cols, axis=1))


def _slot_gain(g):
    return jnp.concatenate([g, jnp.zeros((MLA_SLOT - MLA_QK,), g.dtype)]).reshape(1, MLA_SLOT)


def _layout_lora(w2, a2):
    z = jnp.zeros_like(w2)
    return jnp.concatenate([jnp.concatenate([w2, z], axis=2), jnp.concatenate([z, a2], axis=2)], axis=1)


def _mla_tables(lq):
    rows = lq // GRID_W
    row = jnp.repeat(jnp.arange(rows, dtype=F32), GRID_W)
    col = jnp.tile(jnp.arange(GRID_W, dtype=F32), rows)
    axis_dim = MLA_ROPE // 2
    inv_axis = ROPE_BASE ** (-jnp.arange(0, axis_dim, 2, dtype=F32) / axis_dim)
    ar = row[:, None] * inv_axis[None, :]
    ac = col[:, None] * inv_axis[None, :]
    ones = jnp.ones((lq, MLA_NOPE), F32)
    tail = jnp.ones((lq, MLA_SLOT - MLA_QK), F32)
    cos = jnp.concatenate([ones, jnp.cos(ar), jnp.cos(ar), jnp.cos(ac), jnp.cos(ac), tail], axis=1)
    sin = jnp.concatenate([0 * ones, -jnp.sin(ar), jnp.sin(ar), -jnp.sin(ac), jnp.sin(ac), 0 * tail], axis=1)
    return cos, sin


def _ret_tables(lq):
    theta = 1.0 / (RET_THETA_BASE ** jnp.linspace(0.0, 1.0, HEAD_DIM // 2, dtype=F32))
    ang = jnp.arange(lq, dtype=F32)[:, None] * theta[None, :]
    cos = jnp.tile(jnp.cos(ang), (1, 2 * N_HEADS))
    sin = jnp.tile(jnp.concatenate([-jnp.sin(ang), jnp.sin(ang)], axis=1), (1, N_HEADS))
    return cos, sin


def _ret_decay_tables(c, reverse):
    log_gamma = jnp.log1p(-jnp.exp2(-5.0 - jnp.arange(N_HEADS, dtype=F32)))
    lanes = jnp.repeat(log_gamma, HEAD_DIM)[None, :]
    j = jnp.arange(c, dtype=F32)
    if reverse:
        xi = jnp.exp((c - j)[:, None] * lanes)
        zeta = jnp.exp(j[:, None] * lanes)
        diff = j[None, :] - j[:, None]
        mask = diff > 0
    else:
        xi = jnp.exp((j + 1.0)[:, None] * lanes)
        zeta = jnp.exp((c - 1.0 - j)[:, None] * lanes)
        diff = j[:, None] - j[None, :]
        mask = diff >= 0
    dmat = jnp.where(mask[None], jnp.exp(jnp.where(mask, diff, 0.0)[None] * log_gamma[:, None, None]), 0.0)
    dec = jnp.exp(c * lanes)
    return xi, zeta, dmat, dec


def _mixers_pre(x, mod, lw, tables):
    zc, zm, zr, zt = _inproj(x, mod, lw['g_norm1'], lw['w_in'])
    y_conv = _conv_mixer(zc, lw['conv_w'], lw['g_out_conv'])
    q, k, v = _mla_prep(zm, tables['mla_cos'], tables['mla_sin'], lw['g_kv_norm'], lw['w_uk'],
                        lw['w_uv'], lw['g_q'], lw['g_k'])
    rw = _rwkv_proj(zr, lw['rwkv_mu'], lw['rwkv_k_k'], lw['rwkv_k_a'], lw['rwkv_w0'], lw['rwkv_a0'], lw['lora'])
    return dict(y_conv=y_conv, q=q, k=k, v=v, rw=rw, zt=zt)


def _rwkv_stream(rw, s_fw, s_bw, lw, emit):
    r, v, kk, gl, ks, lw0, b0, kt0, lw1, b1, kt1 = rw
    yf, s_fw = _rwkv_scan(lw0, kk, b0, kt0, v, r, s_fw, False)
    yb, s_bw = _rwkv_scan(lw1, kk, b1, kt1, v, r, s_bw, True)
    y = _rwkv_out(yf, yb, r, v, ks, gl, lw['rwkv_g2'], lw['rwkv_r_k'], lw['g_ln_x']) if emit else None
    return y, s_fw, s_bw


def _ret_stream(zt, r_fw, r_bw, lw, tables, emit):
    lq = zt.shape[1]
    c = min(RET_CHUNK, lq)
    yf, r_fw = _ret_scan(zt, tables['ret_cos'], tables['ret_sin'], *_ret_decay_tables(c, False), r_fw, False)
    yb, r_bw = _ret_scan(zt, tables['ret_cos'], tables['ret_sin'], *_ret_decay_tables(c, True), r_bw, True)
    y = _ret_out(yf, yb, zt, lw['g_ret_norm']) if emit else None
    return y, r_fw, r_bw


def _channel_mix(ys, x, mod, lw):
    x, h, logits_t = _outproj(ys, x, mod, lw['w_out'], lw['g_norm2'], lw['w_router_t'], lw['b_router'])
    cap = CAP_FACTOR * x.shape[1] // N_EXP
    rank, gate = _route(logits_t, cap)
    xe, gr = _gather(h, rank, gate, cap)
    ye = _expert_ffn(xe, gr, lw['w_gate'], lw['w_up'], lw['w_down'])
    return _combine(x, mod, rank, ye, cap)


def kernel(x, c, ctx, c_ctx, w_ada, b_ada, g_norm1, g_norm2, w_in, w_out, conv_w, g_out_conv, g_kv_norm, w_uk, w_uv, g_q_norm, g_k_norm, g_out_mla, rwkv_mu, rwkv_w0, rwkv_w2, rwkv_a0, rwkv_a2, rwkv_g2, rwkv_k_k, rwkv_k_a, rwkv_r_k, g_ln_x, g_ret_norm, w_router, b_router, w_gate, w_up, w_down):
    depth = w_in.shape[0]
    bsz, lq, d = x.shape
    lc = ctx.shape[1]

    cc = jnp.zeros((16, d), F32).at[:bsz].set(c).at[bsz].set(c_ctx)
    mods = _mods(cc, w_ada, b_ada)

    cos_x, sin_x = _mla_tables(lq)
    rcos_x, rsin_x = _ret_tables(lq)
    tab_x = dict(mla_cos=cos_x, mla_sin=sin_x, ret_cos=rcos_x, ret_sin=rsin_x)
    tab_c = dict(mla_cos=jnp.ones((lc, MLA_SLOT), F32), mla_sin=jnp.zeros((lc, MLA_SLOT), F32),
                 ret_cos=jnp.ones((lc, GROUP_W), F32), ret_sin=jnp.zeros((lc, GROUP_W), F32))
    zero_state = jnp.zeros((bsz, GROUP_W, GROUP_W), F32)

    xc = ctx
    for l in range(depth):
        need_ctx = l < depth - 1
        lw = dict(
            g_norm1=g_norm1[l], g_norm2=g_norm2[l], w_in=_layout_w_in(w_in[l]), w_out=w_out[l].astype(BF16),
            conv_w=conv_w[l], g_out_conv=g_out_conv[l], g_kv_norm=g_kv_norm[l],
            w_uk=_layout_w_uk(w_uk[l]), w_uv=_split_w(w_uv[l]),
            g_q=_slot_gain(g_q_norm[l]), g_k=_slot_gain(g_k_norm[l]), g_out_mla=g_out_mla[l],
            rwkv_mu=rwkv_mu[l], rwkv_w0=rwkv_w0[l], rwkv_a0=rwkv_a0[l],
            lora=_layout_lora(rwkv_w2[l], rwkv_a2[l]), rwkv_g2=_split_w(rwkv_g2[l]),
            rwkv_k_k=rwkv_k_k[l], rwkv_k_a=rwkv_k_a[l], rwkv_r_k=rwkv_r_k[l], g_ln_x=g_ln_x[l],
            g_ret_norm=g_ret_norm[l], w_router_t=w_router[l].T, b_router=b_router[l],
            w_gate=w_gate[l], w_up=w_up[l], w_down=w_down[l])
        mod_x = mods[l, :bsz].reshape(bsz, 6, d)
        mod_c = jnp.broadcast_to(mods[l, bsz].reshape(1, 6, d), (bsz, 6, d))

        pc = _mixers_pre(xc, mod_c, lw, tab_c)
        px = _mixers_pre(x, mod_x, lw, tab_x)

        k_all = jnp.concatenate([pc['k'], px['k']], axis=1)
        v_all = jnp.concatenate([pc['v'], px['v']], axis=1)
        ym_x = _attention(px['q'], k_all, v_all, lw['g_out_mla'])
        yr_c, s_fw, s_bw = _rwkv_stream(pc['rw'], zero_state, zero_state, lw, need_ctx)
        yr_x, _, _ = _rwkv_stream(px['rw'], s_fw, s_bw, lw, True)
        yt_c, r_fw, r_bw = _ret_stream(pc['zt'], zero_state, zero_state, lw, tab_c, need_ctx)
        yt_x, _, _ = _ret_stream(px['zt'], r_fw, r_bw, lw, tab_x, True)

        x = _channel_mix((px['y_conv'], ym_x, yr_x, yt_x), x, mod_x, lw)
        if need_ctx:
            ym_c = _attention(pc['q'], pc['k'], pc['v'], lw['g_out_mla'])
            xc = _channel_mix((pc['y_conv'], ym_c, yr_c, yt_c), xc, mod_c, lw)
    return x
```

```python
import functools

import jax
import jax.numpy as jnp
from jax import lax
from jax.experimental import pallas as pl
from jax.experimental.pallas import tpu as pltpu

F32 = jnp.float32
BF16 = jnp.bfloat16
HI = lax.Precision.HIGHEST

D_MODEL = 1024
N_HEADS = 4
HEAD_DIM = 64
GROUP_W = N_HEADS * HEAD_DIM
NORM_EPS = 1e-6
MLA_NOPE = 64
MLA_ROPE = 32
MLA_QK = MLA_NOPE + MLA_ROPE
MLA_SLOT = 128
KV_RANK = 128
ROPE_BASE = 10000.0
DECAY_LORA = 64
ICLR_LORA = 64
GATE_LORA = 128
RWKV_GN_EPS = 64e-5
RWKV_CHUNK = 64
RWKV_INV_BASE = 16
RWKV_BATCH = 2
RET_CHUNK = 256
RET_THETA_BASE = 10000.0
N_EXP = 16
CAP_FACTOR = 2
GRID_W = 64

COLS_CONV = 3 * GROUP_W
COLS_MLA_IN = N_HEADS * MLA_QK + KV_RANK + MLA_ROPE
COLS_MLA = N_HEADS * MLA_SLOT + MLA_SLOT + KV_RANK
COLS_RWKV = 3 * GROUP_W + DECAY_LORA + ICLR_LORA + GATE_LORA
COLS_RET = 4 * GROUP_W

TOKEN_TILE = 256
MOE_TOKEN_SLICE = 1024
COMBINE_EXPERTS = 4
COMBINE_TOKEN_SLICE = 512
VMEM_LIMIT = 56 * 1024 * 1024

NN = (((1,), (0,)), ((), ()))
NT = (((1,), (1,)), ((), ()))
TN = (((0,), (0,)), ((), ()))


def _cparams(*sem):
    return pltpu.CompilerParams(dimension_semantics=sem, vmem_limit_bytes=VMEM_LIMIT)


def _dg(a, b, dims=NN, precision=None):
    return lax.dot_general(a, b, dims, precision=precision, preferred_element_type=F32)


def _split(a):
    hi = a.astype(BF16)
    return hi, (a - hi.astype(F32)).astype(BF16)


def _mm3(a, b, dims=NN):
    ah, al = a if isinstance(a, tuple) else _split(a)
    bh, bl = b if isinstance(b, tuple) else _split(b)
    return _dg(ah, bh, dims) + _dg(ah, bl, dims) + _dg(al, bh, dims)


def _mm(a, b, dims=NN):
    return _dg(a.astype(BF16), b.astype(BF16), dims)


def _head_ones(n, width, scale=1.0):
    r = lax.broadcasted_iota(jnp.int32, (n, n), 0) // width
    c = lax.broadcasted_iota(jnp.int32, (n, n), 1) // width
    return jnp.where(r == c, scale, 0.0).astype(F32)


def _lane_head(shape, width):
    return lax.broadcasted_iota(jnp.int32, shape, len(shape) - 1) // width


def _sigmoid(x):
    return 1.0 / (1.0 + jnp.exp(-x))


def _interleave(*chains):
    results = [None] * len(chains)
    live = list(range(len(chains)))
    while live:
        for n in list(live):
            try:
                next(chains[n])
            except StopIteration as done:
                results[n] = done.value
                live.remove(n)
    return results


def _mods_kernel(cc_ref, w_ref, b_ref, o_ref):
    cc = cc_ref[...]
    o_ref[0] = _dg(cc * _sigmoid(cc), w_ref[0], NN, HI) + b_ref[0]


def _mods(cc, w_ada, b_ada):
    depth, d, n = w_ada.shape
    tn = 1536
    return pl.pallas_call(
        _mods_kernel,
        grid=(depth, n // tn),
        in_specs=[pl.BlockSpec(cc.shape, lambda l, j: (0, 0)),
                  pl.BlockSpec((1, d, tn), lambda l, j: (l, 0, j)),
                  pl.BlockSpec((1, 1, tn), lambda l, j: (l, 0, j))],
        out_specs=pl.BlockSpec((1, cc.shape[0], tn), lambda l, j: (l, 0, j)),
        out_shape=jax.ShapeDtypeStruct((depth, cc.shape[0], n), F32),
        compiler_params=_cparams("parallel", "parallel"),
        name="adaln_mods",
    )(cc, w_ada, b_ada.reshape(depth, 1, n))


def _inproj_kernel(x_ref, mod_ref, g_ref, w_ref, zc_ref, zm_ref, zr_ref, zt_ref):
    x = x_ref[0]
    m = mod_ref[0]
    xn = x * lax.rsqrt(jnp.mean(x * x, axis=-1, keepdims=True) + NORM_EPS) * g_ref[...]
    h = (xn * (1.0 + m[1:2]) + m[0:1]).astype(BF16)
    o = 0
    for ref, n in ((zc_ref, COLS_CONV), (zm_ref, COLS_MLA), (zr_ref, COLS_RWKV), (zt_ref, COLS_RET)):
        ref[0] = _dg(h, w_ref[:, o:o + n])
        o += n


def _inproj(x, mod, g, w):
    b, lq, d = x.shape
    tm = min(TOKEN_TILE, lq)
    widths = (COLS_CONV, COLS_MLA, COLS_RWKV, COLS_RET)
    return pl.pallas_call(
        _inproj_kernel,
        grid=(b, lq // tm),
        in_specs=[pl.BlockSpec((1, tm, d), lambda i, j: (i, j, 0)),
                  pl.BlockSpec((1, 6, d), lambda i, j: (i, 0, 0)),
                  pl.BlockSpec((1, d), lambda i, j: (0, 0)),
                  pl.BlockSpec(w.shape, lambda i, j: (0, 0))],
        out_specs=[pl.BlockSpec((1, tm, n), lambda i, j: (i, j, 0)) for n in widths],
        out_shape=[jax.ShapeDtypeStruct((b, lq, n), F32) for n in widths],
        compiler_params=_cparams("parallel", "parallel"),
        name="inproj",
    )(x, mod, g.reshape(1, d), w)


def _halo_specs(tm, lq, width):
    r = tm // 8
    last = lq // 8 - 1
    return [pl.BlockSpec((1, tm, width), lambda i, j: (i, j, 0)),
            pl.BlockSpec((1, 8, width), lambda i, j: (i, jnp.maximum(j * r - 1, 0), 0)),
            pl.BlockSpec((1, 8, width), lambda i, j: (i, jnp.minimum((j + 1) * r, last), 0))]


def _shifted(t, prev_row, next_row):
    n = t.shape[0]
    rows = lax.broadcasted_iota(jnp.int32, t.shape, 0)
    t_prev = jnp.where(rows == 0, prev_row, pltpu.roll(t, 1, 0))
    t_next = jnp.where(rows == n - 1, next_row, pltpu.roll(t, n - 1, 0))
    return t_prev, t_next


def _conv_kernel(z_ref, zp_ref, zn_ref, cw_ref, g_ref, o_ref, *, nt):
    j = pl.program_id(1)
    w = GROUP_W
    z = z_ref[0]
    bgate = z[:, :w]
    u = z[:, w:2 * w] * z[:, 2 * w:]
    zp = zp_ref[0][7:8]
    zn = zn_ref[0][0:1]
    up = jnp.where(j > 0, zp[:, w:2 * w] * zp[:, 2 * w:], 0.0)
    un = jnp.where(j < nt - 1, zn[:, w:2 * w] * zn[:, 2 * w:], 0.0)
    u_prev, u_next = _shifted(u, up, un)
    cw = cw_ref[...]
    t = bgate * (cw[0:1] * u_prev + cw[1:2] * u + cw[2:3] * u_next)
    o_ref[0] = t * lax.rsqrt(jnp.mean(t * t, axis=-1, keepdims=True) + NORM_EPS) * g_ref[...]


def _conv_mixer(zc, conv_w, g_out):
    b, lq, width = zc.shape
    tm = min(TOKEN_TILE, lq)
    return pl.pallas_call(
        functools.partial(_conv_kernel, nt=lq // tm),
        grid=(b, lq // tm),
        in_specs=_halo_specs(tm, lq, width) + [
            pl.BlockSpec((3, GROUP_W), lambda i, j: (0, 0)),
            pl.BlockSpec((1, GROUP_W), lambda i, j: (0, 0))],
        out_specs=pl.BlockSpec((1, tm, GROUP_W), lambda i, j: (i, j, 0)),
        out_shape=jax.ShapeDtypeStruct((b, lq, GROUP_W), F32),
        compiler_params=_cparams("parallel", "parallel"),
        name="conv_mixer",
    )(zc, zc, zc, conv_w, g_out.reshape(1, GROUP_W))


def _tile4(t):
    return jnp.concatenate([t, t, t, t], axis=-1)


def _slot_norm_rope(t, g, cos, sin):
    parts = []
    for h in range(N_HEADS):
        s = t[:, h * MLA_SLOT:(h + 1) * MLA_SLOT]
        ms = jnp.sum(s * s, axis=-1, keepdims=True) * (1.0 / MLA_QK)
        parts.append(s * lax.rsqrt(ms + NORM_EPS) * g)
    tn = jnp.concatenate(parts, axis=-1)
    n = tn.shape[-1]
    lane = lax.broadcasted_iota(jnp.int32, tn.shape, 1)
    half = MLA_ROPE // 4
    partner = jnp.where(lane % (2 * half) < half, pltpu.roll(tn, n - half, 1), pltpu.roll(tn, half, 1))
    return tn * _tile4(cos) + partner * _tile4(sin)


def _mla_prep_kernel(z_ref, cos_ref, sin_ref, gkv_ref, wukh_ref, wukl_ref, wuvh_ref, wuvl_ref, gq_ref, gk_ref,
                     q_ref, k_ref, v_ref):
    z = z_ref[0]
    nq = N_HEADS * MLA_SLOT
    q_in = z[:, :nq]
    k_rope = z[:, nq:nq + MLA_SLOT]
    ckv = z[:, nq + MLA_SLOT:]
    ckv = _split(ckv * lax.rsqrt(jnp.mean(ckv * ckv, axis=-1, keepdims=True) + NORM_EPS) * gkv_ref[...])
    k_in = _mm3(ckv, (wukh_ref[...], wukl_ref[...])) + _tile4(k_rope)
    cos = cos_ref[...]
    sin = sin_ref[...]
    q_ref[0] = (_slot_norm_rope(q_in, gq_ref[...], cos, sin) * (MLA_QK ** -0.5)).astype(BF16)
    k_ref[0] = _slot_norm_rope(k_in, gk_ref[...], cos, sin).astype(BF16)
    v_ref[0] = _mm3(ckv, (wuvh_ref[...], wuvl_ref[...])).astype(BF16)


def _mla_prep(zm, cos, sin, g_kv, wuk, wuv, g_q, g_k):
    b, lq, width = zm.shape
    tm = min(TOKEN_TILE, lq)
    nq = N_HEADS * MLA_SLOT
    const = lambda shape: pl.BlockSpec(shape, lambda i, j: (0, 0))
    tok = lambda i, j: (i, j, 0)
    return pl.pallas_call(
        _mla_prep_kernel,
        grid=(b, lq // tm),
        in_specs=[pl.BlockSpec((1, tm, width), tok),
                  pl.BlockSpec((tm, MLA_SLOT), lambda i, j: (j, 0)),
                  pl.BlockSpec((tm, MLA_SLOT), lambda i, j: (j, 0)),
                  const((1, KV_RANK)), const(wuk[0].shape), const(wuk[0].shape),
                  const(wuv[0].shape), const(wuv[0].shape),
                  const((1, MLA_SLOT)), const((1, MLA_SLOT))],
        out_specs=[pl.BlockSpec((1, tm, nq), tok), pl.BlockSpec((1, tm, nq), tok), pl.BlockSpec((1, tm, GROUP_W), tok)],
        out_shape=[jax.ShapeDtypeStruct((b, lq, nq), BF16)] * 2 + [jax.ShapeDtypeStruct((b, lq, GROUP_W), BF16)],
        compiler_params=_cparams("parallel", "parallel"),
        name="mla_prep",
    )(zm, cos, sin, g_kv.reshape(1, KV_RANK), *wuk, *wuv, g_q, g_k)


def _attn_head(q, k, v):
    s = _dg(q, k, NT)
    yield
    p = jnp.exp(s - jnp.max(s, axis=-1, keepdims=True))
    inv = 1.0 / jnp.sum(p, axis=-1, keepdims=True)
    return _mm(p, v) * inv


def _attn_kernel(q_ref, k_ref, v_ref, g_ref, o_ref):
    v = v_ref[0]
    tq = q_ref.shape[1]
    head = _lane_head((tq, GROUP_W), HEAD_DIM)
    slots = [slice(h * MLA_SLOT, (h + 1) * MLA_SLOT) for h in range(N_HEADS)]
    outs = _interleave(*[_attn_head(q_ref[0, :, sl], k_ref[0, :, sl], v) for sl in slots])
    out = outs[0]
    for h in range(1, N_HEADS):
        out = jnp.where(head == h, outs[h], out)
    o_ref[0] = out * lax.rsqrt(jnp.mean(out * out, axis=-1, keepdims=True) + NORM_EPS) * g_ref[...]


def _attention(q, k, v, g_out):
    b, lq, nq = q.shape
    lk = k.shape[1]
    tq = min(TOKEN_TILE, lq)
    return pl.pallas_call(
        _attn_kernel,
        grid=(b, lq // tq),
        in_specs=[pl.BlockSpec((1, tq, nq), lambda i, j: (i, j, 0)),
                  pl.BlockSpec((1, lk, nq), lambda i, j: (i, 0, 0)),
                  pl.BlockSpec((1, lk, GROUP_W), lambda i, j: (i, 0, 0)),
                  pl.BlockSpec((1, GROUP_W), lambda i, j: (0, 0))],
        out_specs=pl.BlockSpec((1, tq, GROUP_W), lambda i, j: (i, j, 0)),
        out_shape=jax.ShapeDtypeStruct((b, lq, GROUP_W), F32),
        compiler_params=_cparams("parallel", "arbitrary"),
        name="mla_attention",
    )(q, k, v, g_out.reshape(1, GROUP_W))


def _rwkv_proj_kernel(z_ref, zp_ref, zn_ref, mu_ref, kk_ref, ka_ref, w0_ref, a0_ref, lora_ref,
                      r_ref, v_ref, kkn_ref, gl_ref, ks_ref,
                      lw0_ref, b0_ref, kt0_ref, lw1_ref, b1_ref, kt1_ref, *, nt):
    j = pl.program_id(1)
    w = GROUP_W
    z = z_ref[0]
    zp = jnp.where(j > 0, zp_ref[0][7:8], 0.0)
    zn = jnp.where(j < nt - 1, zn_ref[0][0:1], 0.0)
    z_prev, z_next = _shifted(z, zp, zn)
    z = z + mu_ref[...] * (0.5 * (z_prev + z_next) - z)
    r = z[:, :w]
    k = z[:, w:2 * w]
    v = z[:, 2 * w:3 * w]
    lora_in = z[:, 3 * w:3 * w + DECAY_LORA + ICLR_LORA]
    gl = z[:, 3 * w + DECAY_LORA + ICLR_LORA:]
    kk = k * kk_ref[...]
    kk = kk * lax.rsqrt(_dg(kk * kk, _head_ones(w, HEAD_DIM), NN, HI) + 1e-12)
    lane = lax.broadcasted_iota(jnp.int32, lora_in.shape, 1)
    lora_in = jnp.where(lane < DECAY_LORA, jnp.tanh(lora_in), lora_in)
    r_ref[0] = r
    v_ref[0] = v
    kkn_ref[0] = kk
    gl_ref[0] = gl
    ksum = jnp.zeros_like(k)
    for d, (lw_ref, b_ref, kt_ref) in enumerate(((lw0_ref, b0_ref, kt0_ref), (lw1_ref, b1_ref, kt1_ref))):
        lo = _dg(lora_in, lora_ref[d], NN, HI)
        t = -(w0_ref[d:d + 1] + lo[:, :w])
        softplus = jnp.maximum(t, 0.0) + jnp.log1p(jnp.exp(-jnp.abs(t)))
        lw_ref[0] = -jnp.exp(-softplus - 0.5)
        a = _sigmoid(a0_ref[d:d + 1] + lo[:, w:])
        kt = k * (1.0 + (a - 1.0) * ka_ref[...])
        b_ref[0] = kk * a
        kt_ref[0] = kt
        ksum = ksum + kt
    ks_ref[0] = ksum


def _rwkv_proj(zr, mu, k_k, k_a, w0, a0, lora):
    b, lq, width = zr.shape
    tm = min(TOKEN_TILE, lq)
    w = GROUP_W
    const2 = lambda shape: pl.BlockSpec(shape, lambda i, j: (0,) * len(shape))
    out_w = (w, w, w, GATE_LORA, w, w, w, w, w, w, w)
    return pl.pallas_call(
        functools.partial(_rwkv_proj_kernel, nt=lq // tm),
        grid=(b, lq // tm),
        in_specs=_halo_specs(tm, lq, width) + [
            const2((1, width)), const2((1, w)), const2((1, w)), const2((2, w)), const2((2, w)),
            const2(lora.shape)],
        out_specs=[pl.BlockSpec((1, tm, n), lambda i, j: (i, j, 0)) for n in out_w],
        out_shape=[jax.ShapeDtypeStruct((b, lq, n), F32) for n in out_w],
        compiler_params=_cparams("parallel", "parallel"),
        name="rwkv_proj",
    )(zr, zr, zr, mu.reshape(1, width), k_k.reshape(1, w), k_a.reshape(1, w), w0, a0, lora)


def _stack_heads(t):
    head = _lane_head(t.shape, HEAD_DIM)
    return jnp.concatenate([jnp.where(head == h, t, 0.0) for h in range(N_HEADS)], axis=0)


def _unit_lower_inverse(n_mat, p_i, q_i, size):
    base = RWKV_INV_BASE
    same = lambda w: (p_i // w) == (q_i // w)
    m = jnp.where(same(base), -n_mat, 0.0)
    inv = jnp.where(p_i == q_i, 1.0, 0.0) + m
    pw = m
    span = 1
    while 2 * span < base:
        pw = _mm(pw, pw)
        yield
        inv = inv + _mm(inv, pw)
        yield
        span *= 2
    w = base
    while w < size:
        off = jnp.where(same(2 * w) & jnp.logical_not(same(w)), n_mat, 0.0)
        invb = inv.astype(BF16)
        left = _mm(invb, off)
        yield
        inv = inv - _mm(left, invb)
        yield
        w *= 2
    return inv


def _rwkv_chunk(lw, kk, b, kt, v, r, s, reverse):
    c = RWKV_CHUNK
    n = N_HEADS * c
    t_i = lax.broadcasted_iota(jnp.int32, (c, c), 0)
    s_i = lax.broadcasted_iota(jnp.int32, (c, c), 1)
    seen = (s_i >= t_i) if reverse else (s_i <= t_i)
    g = _dg(jnp.where(seen, 1.0, 0.0).astype(F32), lw, NN, HI)
    g_tot = g[0:1] if reverse else g[c - 1:c]
    e_neg = jnp.exp(-g)
    e_tot = jnp.exp(g_tot)
    bd = _stack_heads(b * e_neg)
    ktd = _stack_heads(kt * e_neg)
    vs = _stack_heads(v).astype(BF16)
    lhs = jnp.concatenate([_stack_heads(kk * jnp.exp(g - lw)), _stack_heads(r * jnp.exp(g))], axis=0).astype(BF16)
    gram = _mm(lhs, jnp.concatenate([bd, ktd], axis=0), NT)
    from_state = _mm(lhs, s, NT)
    yield

    p_i = lax.broadcasted_iota(jnp.int32, (n, n), 0)
    q_i = lax.broadcasted_iota(jnp.int32, (n, n), 1)
    strict = (q_i > p_i) if reverse else (q_i < p_i)
    incl = (q_i >= p_i) if reverse else (q_i <= p_i)
    n_ab = jnp.where(strict, gram[:n, :n], 0.0)
    n_ak = jnp.where(strict, gram[:n, n:], 0.0)
    a_rb = jnp.where(incl, gram[n:, :n], 0.0)
    a_rk = jnp.where(incl, gram[n:, n:], 0.0)
    from_v = _mm(jnp.concatenate([n_ak, a_rk], axis=0), vs)
    t_inv = yield from _unit_lower_inverse(n_ab, p_i, q_i, c)
    u = (-_mm(t_inv, from_state[:n] + from_v[:n])).astype(BF16)
    yield
    y = from_state[n:] + _mm(a_rb, u) + from_v[n:]
    y = y[0:c] + y[c:2 * c] + y[2 * c:3 * c] + y[3 * c:]
    s_new = s * e_tot + _mm(jnp.concatenate([u, vs], axis=0),
                            jnp.concatenate([bd * e_tot, ktd * e_tot], axis=0), TN)
    return y, s_new


def _rwkv_scan_kernel(lwf_ref, kkf_ref, bf_ref, ktf_ref, vf_ref, rf_ref,
                      lwb_ref, kkb_ref, bb_ref, ktb_ref, vb_ref, rb_ref, sf0_ref, sb0_ref,
                      yf_ref, yb_ref, sfo_ref, sbo_ref, sf_ref, sb_ref, *, nc):
    i = pl.program_id(1)

    @pl.when(i == 0)
    def _():
        sf_ref[...] = sf0_ref[...]
        sb_ref[...] = sb0_ref[...]

    chains = []
    for n in range(RWKV_BATCH):
        chains.append(_rwkv_chunk(lwf_ref[n], kkf_ref[n], bf_ref[n], ktf_ref[n], vf_ref[n], rf_ref[n], sf_ref[n], False))
        chains.append(_rwkv_chunk(lwb_ref[n], kkb_ref[n], bb_ref[n], ktb_ref[n], vb_ref[n], rb_ref[n], sb_ref[n], True))
    results = _interleave(*chains)
    for n in range(RWKV_BATCH):
        (yf, sf), (yb, sb) = results[2 * n], results[2 * n + 1]
        yf_ref[n] = yf
        yb_ref[n] = yb
        sf_ref[n] = sf
        sb_ref[n] = sb

    @pl.when(i == nc - 1)
    def _():
        sfo_ref[...] = sf_ref[...]
        sbo_ref[...] = sb_ref[...]


def _rwkv_scan(fw, bw, kk, v, r, s_fw, s_bw):
    b, lq, w = kk.shape
    c = RWKV_CHUNK
    nc = lq // c
    nb = RWKV_BATCH
    fspec = pl.BlockSpec((nb, c, w), lambda i, j: (i, j, 0))
    bspec = pl.BlockSpec((nb, c, w), lambda i, j: (i, nc - 1 - j, 0))
    sspec = pl.BlockSpec((nb, w, w), lambda i, j: (i, 0, 0))
    yshape = jax.ShapeDtypeStruct((b, lq, w), F32)
    sshape = jax.ShapeDtypeStruct((b, w, w), F32)
    return pl.pallas_call(
        functools.partial(_rwkv_scan_kernel, nc=nc),
        grid=(b // nb, nc),
        in_specs=[fspec] * 6 + [bspec] * 6 + [sspec, sspec],
        out_specs=[fspec, bspec, sspec, sspec],
        out_shape=[yshape, yshape, sshape, sshape],
        scratch_shapes=[pltpu.VMEM((nb, w, w), F32), pltpu.VMEM((nb, w, w), F32)],
        compiler_params=_cparams("parallel", "arbitrary"),
        name="rwkv_scan",
    )(fw[0], kk, fw[1], fw[2], v, r, bw[0], kk, bw[1], bw[2], v, r, s_fw, s_bw)


def _rwkv_out_kernel(yf_ref, yb_ref, r_ref, v_ref, ks_ref, gl_ref, g2h_ref, g2l_ref, rk_ref, gln_ref, o_ref):
    y = yf_ref[0] + yb_ref[0]
    avg = _head_ones(GROUP_W, HEAD_DIM, 1.0 / HEAD_DIM)
    dlt = y - _dg(y, avg, NN, HI)
    var = _dg(dlt * dlt, avg, NN, HI)
    yn = dlt * lax.rsqrt(var + RWKV_GN_EPS) * gln_ref[...]
    bonus = _dg(r_ref[0] * ks_ref[0] * rk_ref[...], _head_ones(GROUP_W, HEAD_DIM), NN, HI)
    yn = yn + bonus * v_ref[0]
    o_ref[0] = yn * _mm3(_sigmoid(gl_ref[0]), (g2h_ref[...], g2l_ref[...]))


def _rwkv_out(yf, yb, r, v, ks, gl, g2, r_k, g_ln):
    b, lq, w = yf.shape
    tm = min(TOKEN_TILE, lq)
    tspec = pl.BlockSpec((1, tm, w), lambda i, j: (i, j, 0))
    const = lambda shape: pl.BlockSpec(shape, lambda i, j: (0, 0))
    return pl.pallas_call(
        _rwkv_out_kernel,
        grid=(b, lq // tm),
        in_specs=[tspec] * 5 + [pl.BlockSpec((1, tm, GATE_LORA), lambda i, j: (i, j, 0)),
                                const(g2[0].shape), const(g2[0].shape), const((1, w)), const((1, w))],
        out_specs=tspec,
        out_shape=jax.ShapeDtypeStruct((b, lq, w), F32),
        compiler_params=_cparams("parallel", "parallel"),
        name="rwkv_out",
    )(yf, yb, r, v, ks, gl, *g2, r_k.reshape(1, w), g_ln.reshape(1, w))


def _half_rotate(t, cos, sin):
    n = t.shape[-1]
    lane = lax.broadcasted_iota(jnp.int32, t.shape, 1)
    half = HEAD_DIM // 2
    partner = jnp.where(lane % HEAD_DIM < half, pltpu.roll(t, n - half, 1), pltpu.roll(t, half, 1))
    return t * cos + partner * sin


def _ret_chunk(z, cos, sin, xi, zeta, dmat_ref, dec, st):
    w = GROUP_W
    q = _half_rotate(z[:, :w] * (HEAD_DIM ** -0.5), cos, sin)
    k = _half_rotate(z[:, w:2 * w], cos, sin)
    vs = z[:, 2 * w:3 * w].astype(BF16)
    ks = k.astype(BF16)
    head = _lane_head(q.shape, HEAD_DIM)
    y = _mm(q * xi, st)
    upd = _mm(k * zeta, vs, TN)
    scores = [_mm(jnp.where(head == h, q, 0.0), ks, NT) * dmat_ref[h] for h in range(N_HEADS)]
    yield
    for h in range(N_HEADS):
        y = y + jnp.where(head == h, _mm(scores[h], vs), 0.0)
    return y, st * dec + upd * _head_ones(w, HEAD_DIM)


def _ret_scan_kernel(zf_ref, cosf_ref, sinf_ref, zb_ref, cosb_ref, sinb_ref,
                     xif_ref, zetaf_ref, dmatf_ref, xib_ref, zetab_ref, dmatb_ref, dec_ref, rf0_ref, rb0_ref,
                     yf_ref, yb_ref, rfo_ref, rbo_ref, stf_ref, stb_ref, *, nc):
    i = pl.program_id(1)

    @pl.when(i == 0)
    def _():
        stf_ref[...] = rf0_ref[0]
        stb_ref[...] = rb0_ref[0]

    dec = dec_ref[...]
    (yf, stf), (yb, stb) = _interleave(
        _ret_chunk(zf_ref[0], cosf_ref[...], sinf_ref[...], xif_ref[...], zetaf_ref[...], dmatf_ref, dec, stf_ref[...]),
        _ret_chunk(zb_ref[0], cosb_ref[...], sinb_ref[...], xib_ref[...], zetab_ref[...], dmatb_ref, dec, stb_ref[...]))
    yf_ref[0] = yf
    yb_ref[0] = yb
    stf_ref[...] = stf
    stb_ref[...] = stb

    @pl.when(i == nc - 1)
    def _():
        rfo_ref[0] = stf
        rbo_ref[0] = stb


def _ret_scan(zt, cos, sin, r_fw, r_bw):
    b, lq, width = zt.shape
    c = min(RET_CHUNK, lq)
    nc = lq // c
    w = GROUP_W
    xi_f, zeta_f, dmat_f, dec = _ret_decay_tables(c, False)
    xi_b, zeta_b, dmat_b, _ = _ret_decay_tables(c, True)
    fw3 = lambda i, j: (i, j, 0)
    bw3 = lambda i, j: (i, nc - 1 - j, 0)
    fw2 = lambda i, j: (j, 0)
    bw2 = lambda i, j: (nc - 1 - j, 0)
    const = lambda shape: pl.BlockSpec(shape, lambda i, j: (0,) * len(shape))
    sspec = pl.BlockSpec((1, w, w), lambda i, j: (i, 0, 0))
    tables = [const((c, w)), const((c, w)), const((N_HEADS, c, c))]
    yshape = jax.ShapeDtypeStruct((b, lq, w), F32)
    sshape = jax.ShapeDtypeStruct((b, w, w), F32)
    return pl.pallas_call(
        functools.partial(_ret_scan_kernel, nc=nc),
        grid=(b, nc),
        in_specs=[pl.BlockSpec((1, c, width), fw3), pl.BlockSpec((c, w), fw2), pl.BlockSpec((c, w), fw2),
                  pl.BlockSpec((1, c, width), bw3), pl.BlockSpec((c, w), bw2), pl.BlockSpec((c, w), bw2)]
        + tables + tables + [const((1, w)), sspec, sspec],
        out_specs=[pl.BlockSpec((1, c, w), fw3), pl.BlockSpec((1, c, w), bw3), sspec, sspec],
        out_shape=[yshape, yshape, sshape, sshape],
        scratch_shapes=[pltpu.VMEM((w, w), F32), pltpu.VMEM((w, w), F32)],
        compiler_params=_cparams("parallel", "arbitrary"),
        name="ret_scan",
    )(zt, cos, sin, zt, cos, sin, xi_f, zeta_f, dmat_f, xi_b, zeta_b, dmat_b, dec, r_fw, r_bw)


def _ret_out_kernel(yf_ref, yb_ref, z_ref, g_ref, o_ref):
    y = yf_ref[0] + yb_ref[0]
    ms = _dg(y * y, _head_ones(GROUP_W, HEAD_DIM, 1.0 / HEAD_DIM), NN, HI)
    gate = z_ref[0]
    o_ref[0] = gate * _sigmoid(gate) * (y * lax.rsqrt(ms + NORM_EPS) * g_ref[...])


def _ret_out(yf, yb, zt, g_norm):
    b, lq, w = yf.shape
    tm = min(TOKEN_TILE, lq)
    tspec = pl.BlockSpec((1, tm, w), lambda i, j: (i, j, 0))
    return pl.pallas_call(
        _ret_out_kernel,
        grid=(b, lq // tm),
        in_specs=[tspec, tspec, pl.BlockSpec((1, tm, w), lambda i, j: (i, j, 3)),
                  pl.BlockSpec((1, w), lambda i, j: (0, 0))],
        out_specs=tspec,
        out_shape=jax.ShapeDtypeStruct((b, lq, w), F32),
        compiler_params=_cparams("parallel", "parallel"),
        name="ret_out",
    )(yf, yb, zt, g_norm.reshape(1, w))


def _outproj_kernel(yc_ref, ym_ref, yr_ref, yt_ref, x_ref, mod_ref, w_ref, g_ref, wr_ref, br_ref,
                    xo_ref, h_ref, lg_ref):
    w = GROUP_W
    acc = None
    for n, ref in enumerate((yc_ref, ym_ref, yr_ref, yt_ref)):
        rows = slice(n * w, (n + 1) * w)
        part = _mm(ref[0], w_ref[rows, :])
        acc = part if acc is None else acc + part
    m = mod_ref[0]
    x = x_ref[0] + m[2:3] * acc
    xo_ref[0] = x
    xn = x * lax.rsqrt(jnp.mean(x * x, axis=-1, keepdims=True) + NORM_EPS) * g_ref[...]
    h = xn * (1.0 + m[4:5]) + m[3:4]
    h_ref[0] = h.astype(BF16)
    lg_ref[0] = _dg(h, wr_ref[...], NN, HI) + br_ref[...]


def _outproj(ys, x, mod, w_out, g2, w_r, b_r):
    b, lq, d = x.shape
    tm = min(TOKEN_TILE, lq)
    yspec = pl.BlockSpec((1, tm, GROUP_W), lambda i, j: (i, j, 0))
    xspec = pl.BlockSpec((1, tm, d), lambda i, j: (i, j, 0))
    const = lambda shape: pl.BlockSpec(shape, lambda i, j: (0, 0))
    return pl.pallas_call(
        _outproj_kernel,
        grid=(b, lq // tm),
        in_specs=[yspec] * 4 + [xspec, pl.BlockSpec((1, 6, d), lambda i, j: (i, 0, 0)),
                                const(w_out.shape), const((1, d)),
                                const(w_r.shape), const((1, N_EXP))],
        out_specs=[xspec, xspec, pl.BlockSpec((1, tm, N_EXP), lambda i, j: (i, j, 0))],
        out_shape=[jax.ShapeDtypeStruct((b, lq, d), F32), jax.ShapeDtypeStruct((b, lq, d), BF16),
                   jax.ShapeDtypeStruct((b, lq, N_EXP), F32)],
        compiler_params=_cparams("parallel", "parallel"),
        name="outproj",
    )(*ys, x, mod, w_out, g2.reshape(1, d), w_r, b_r.reshape(1, N_EXP))


def _lane_cumsum(m):
    n = m.shape[-1]
    blk = 128
    r_i = lax.broadcasted_iota(jnp.int32, (blk, blk), 0)
    c_i = lax.broadcasted_iota(jnp.int32, (blk, blk), 1)
    tri = jnp.where(r_i <= c_i, 1.0, 0.0).astype(BF16)
    run = jnp.zeros((m.shape[0], 1), F32)
    parts = []
    for j in range(n // blk):
        cs = _dg(m[:, j * blk:(j + 1) * blk].astype(BF16), tri) + run
        parts.append(cs)
        run = cs[:, blk - 1:blk]
    return jnp.concatenate(parts, axis=-1)


def _route_kernel(lg_ref, rank_ref, gate_ref, *, cap):
    lg = lg_ref[0]
    e = jnp.exp(lg - jnp.max(lg, axis=0, keepdims=True))
    aff = e / jnp.sum(e, axis=0, keepdims=True)
    bits = pltpu.bitcast(aff, jnp.int32)
    thr = jnp.zeros((aff.shape[0], 1), jnp.int32)
    for bit in range(30, -1, -1):
        cand = thr | (1 << bit)
        cnt = jnp.sum(jnp.where(bits >= cand, 1.0, 0.0), axis=-1, keepdims=True)
        thr = jnp.where(cnt >= cap, cand, thr)
    above = bits > thr
    tied = bits == thr
    n_above = jnp.sum(jnp.where(above, 1.0, 0.0), axis=-1, keepdims=True)
    tied_f = jnp.where(tied, 1.0, 0.0)
    tied_rank = _lane_cumsum(tied_f) - tied_f
    sel = above | (tied & (tied_rank < cap - n_above))
    sel_f = jnp.where(sel, 1.0, 0.0)
    rank = _lane_cumsum(sel_f) - sel_f
    rank_ref[0] = jnp.where(sel, rank, -1.0).astype(jnp.int32)
    gate_ref[0] = jnp.where(sel, aff, 0.0)


def _route(logits_t, cap):
    b, ne, lq = logits_t.shape
    spec = pl.BlockSpec((1, ne, lq), lambda i: (i, 0, 0))
    return pl.pallas_call(
        functools.partial(_route_kernel, cap=cap),
        grid=(b,),
        in_specs=[spec],
        out_specs=[spec, spec],
        out_shape=[jax.ShapeDtypeStruct((b, ne, lq), jnp.int32), jax.ShapeDtypeStruct((b, ne, lq), F32)],
        compiler_params=_cparams("parallel"),
        name="moe_route",
    )(logits_t)


def _onehot_rows(rank_row, cap):
    r_i = lax.broadcasted_iota(jnp.int32, (cap, rank_row.shape[-1]), 0)
    return r_i == rank_row


def _gather_kernel(h_ref, rank_ref, gate_ref, xe_ref, gr_ref, *, cap):
    lq = h_ref.shape[1]
    ts = min(MOE_TOKEN_SLICE, lq)
    xe = gr = None
    for t in range(lq // ts):
        tok = slice(t * ts, (t + 1) * ts)
        sel = _onehot_rows(rank_ref[0, 0, :, tok], cap)
        px = _dg(jnp.where(sel, 1.0, 0.0).astype(BF16), h_ref[0, tok, :])
        pg = jnp.sum(jnp.where(sel, gate_ref[0, 0, :, tok], 0.0), axis=-1, keepdims=True)
        xe, gr = (px, pg) if xe is None else (xe + px, gr + pg)
    xe_ref[0, 0] = xe.astype(BF16)
    gr_ref[0, 0] = gr


def _gather(h, rank, gate, cap):
    b, lq, d = h.shape
    ne = rank.shape[1]
    rspec = pl.BlockSpec((1, 1, 1, lq), lambda i, e: (i, e, 0, 0))
    xspec = pl.BlockSpec((1, 1, cap, d), lambda i, e: (i, e, 0, 0))
    return pl.pallas_call(
        functools.partial(_gather_kernel, cap=cap),
        grid=(b, ne),
        in_specs=[pl.BlockSpec((1, lq, d), lambda i, e: (i, 0, 0)), rspec, rspec],
        out_specs=[xspec, pl.BlockSpec((1, 1, cap, 1), lambda i, e: (i, e, 0, 0))],
        out_shape=[jax.ShapeDtypeStruct((b, ne, cap, d), BF16), jax.ShapeDtypeStruct((b, ne, cap, 1), F32)],
        compiler_params=_cparams("parallel", "arbitrary"),
        name="moe_gather",
    )(h, rank.reshape(b, ne, 1, lq), gate.reshape(b, ne, 1, lq))


def _ffn_kernel(xe_ref, gr_ref, wg_ref, wu_ref, wd_ref, ye_ref, wgb_ref, wub_ref, wdb_ref):
    @pl.when(pl.program_id(1) == 0)
    def _():
        wgb_ref[...] = wg_ref[0].astype(BF16)
        wub_ref[...] = wu_ref[0].astype(BF16)
        wdb_ref[...] = wd_ref[0].astype(BF16)

    xe = xe_ref[0, 0]
    a = _dg(xe, wgb_ref[...])
    hid = a * _sigmoid(a) * _dg(xe, wub_ref[...])
    ye_ref[0, 0] = (_mm(hid, wdb_ref[...]) * gr_ref[0, 0]).astype(BF16)


def _expert_ffn(xe, gr, w_gate, w_up, w_down):
    b, ne, cap, d = xe.shape
    ff = w_gate.shape[2]
    xspec = pl.BlockSpec((1, 1, cap, d), lambda e, i: (i, e, 0, 0))
    wspec = lambda shape: pl.BlockSpec((1,) + shape, lambda e, i: (e, 0, 0))
    return pl.pallas_call(
        _ffn_kernel,
        grid=(ne, b),
        in_specs=[xspec, pl.BlockSpec((1, 1, cap, 1), lambda e, i: (i, e, 0, 0)),
                  wspec((d, ff)), wspec((d, ff)), wspec((ff, d))],
        out_specs=xspec,
        out_shape=jax.ShapeDtypeStruct((b, ne, cap, d), BF16),
        scratch_shapes=[pltpu.VMEM((d, ff), BF16), pltpu.VMEM((d, ff), BF16), pltpu.VMEM((ff, d), BF16)],
        compiler_params=_cparams("parallel", "arbitrary"),
        name="moe_ffn",
    )(xe, gr, w_gate, w_up, w_down)


def _combine_kernel(x_ref, ga_ref, rank_ref, ye_ref, o_ref, *, cap):
    g = pl.program_id(2)
    lq = x_ref.shape[1]
    ts = min(COMBINE_TOKEN_SLICE, lq)
    ye = ye_ref[0].reshape(COMBINE_EXPERTS * cap, ye_ref.shape[-1])
    for t in range(lq // ts):
        tok = slice(t * ts, (t + 1) * ts)
        p = jnp.concatenate([jnp.where(_onehot_rows(rank_ref[0, e, :, tok], cap), 1.0, 0.0).astype(BF16)
                             for e in range(COMBINE_EXPERTS)], axis=0)
        part = _dg(p, ye, TN)

        @pl.when(g == 0)
        def _():
            o_ref[0, tok, :] = part

        @pl.when(g > 0)
        def _():
            o_ref[0, tok, :] += part

    @pl.when(g == N_EXP // COMBINE_EXPERTS - 1)
    def _():
        o_ref[0] = x_ref[0] + ga_ref[0] * o_ref[0]


def _combine(x, mod, rank, ye, cap):
    b, lq, d = x.shape
    ne = rank.shape[1]
    dcol = d // 2
    xspec = pl.BlockSpec((1, lq, dcol), lambda i, j, e: (i, 0, j))
    yspec = pl.BlockSpec((1, COMBINE_EXPERTS, cap, dcol), lambda i, j, e: (i, e, 0, j))
    return pl.pallas_call(
        functools.partial(_combine_kernel, cap=cap),
        grid=(b, d // dcol, ne // COMBINE_EXPERTS),
        in_specs=[xspec, pl.BlockSpec((1, 1, dcol), lambda i, j, e: (i, 0, j)),
                  pl.BlockSpec((1, COMBINE_EXPERTS, 1, lq), lambda i, j, e: (i, e, 0, 0)), yspec],
        out_specs=xspec,
        out_shape=jax.ShapeDtypeStruct((b, lq, d), F32),
        compiler_params=_cparams("parallel", "parallel", "arbitrary"),
        name="moe_combine",
    )(x, mod[:, 5:6, :], rank.reshape(b, ne, 1, lq), ye)


def _split_w(w):
    hi = w.astype(BF16)
    return hi, (w - hi.astype(F32)).astype(BF16)


def _layout_w_in(w):
    d = w.shape[0]
    o = COLS_CONV
    q_nope = w[:, o:o + N_HEADS * MLA_NOPE]
    q_rope = w[:, o + N_HEADS * MLA_NOPE:o + N_HEADS * MLA_QK]
    ckv = w[:, o + N_HEADS * MLA_QK:o + N_HEADS * MLA_QK + KV_RANK]
    k_rope = w[:, o + N_HEADS * MLA_QK + KV_RANK:o + COLS_MLA_IN]
    pad = jnp.zeros((d, MLA_SLOT - MLA_QK), w.dtype)
    cols = [w[:, :o]]
    for h in range(N_HEADS):
        cols += [q_nope[:, h * MLA_NOPE:(h + 1) * MLA_NOPE], q_rope[:, h * MLA_ROPE:(h + 1) * MLA_ROPE], pad]
    cols += [jnp.zeros((d, MLA_NOPE), w.dtype), k_rope, pad, ckv, w[:, o + COLS_MLA_IN:]]
    return jnp.concatenate(cols, axis=1).astype(BF16)


def _layout_w_uk(w_uk):
    pad = jnp.zeros((w_uk.shape[0], MLA_SLOT - MLA_NOPE), w_uk.dtype)
    cols = []
    for h in range(N_HEADS):
        cols += [w_uk[:, h * MLA_NOPE:(h + 1) * MLA_NOPE], pad]
    return _split_w(jnp.concatenate(cols, axis=1))


def _slot_gain(g):
    return jnp.concatenate([g, jnp.zeros((MLA_SLOT - MLA_QK,), g.dtype)]).reshape(1, MLA_SLOT)


def _layout_lora(w2, a2):
    z = jnp.zeros_like(w2)
    return jnp.concatenate([jnp.concatenate([w2, z], axis=2), jnp.concatenate([z, a2], axis=2)], axis=1)


def _mla_tables(lq):
    rows = lq // GRID_W
    row = jnp.repeat(jnp.arange(rows, dtype=F32), GRID_W)
    col = jnp.tile(jnp.arange(GRID_W, dtype=F32), rows)
    axis_dim = MLA_ROPE // 2
    inv_axis = ROPE_BASE ** (-jnp.arange(0, axis_dim, 2, dtype=F32) / axis_dim)
    ar = row[:, None] * inv_axis[None, :]
    ac = col[:, None] * inv_axis[None, :]
    ones = jnp.ones((lq, MLA_NOPE), F32)
    tail = jnp.ones((lq, MLA_SLOT - MLA_QK), F32)
    cos = jnp.concatenate([ones, jnp.cos(ar), jnp.cos(ar), jnp.cos(ac), jnp.cos(ac), tail], axis=1)
    sin = jnp.concatenate([0 * ones, -jnp.sin(ar), jnp.sin(ar), -jnp.sin(ac), jnp.sin(ac), 0 * tail], axis=1)
    return cos, sin


def _ret_tables(lq):
    theta = 1.0 / (RET_THETA_BASE ** jnp.linspace(0.0, 1.0, HEAD_DIM // 2, dtype=F32))
    ang = jnp.arange(lq, dtype=F32)[:, None] * theta[None, :]
    cos = jnp.tile(jnp.cos(ang), (1, 2 * N_HEADS))
    sin = jnp.tile(jnp.concatenate([-jnp.sin(ang), jnp.sin(ang)], axis=1), (1, N_HEADS))
    return cos, sin


def _ret_decay_tables(c, reverse):
    log_gamma = jnp.log1p(-jnp.exp2(-5.0 - jnp.arange(N_HEADS, dtype=F32)))
    lanes = jnp.repeat(log_gamma, HEAD_DIM)[None, :]
    j = jnp.arange(c, dtype=F32)
    if reverse:
        xi = jnp.exp((c - j)[:, None] * lanes)
        zeta = jnp.exp(j[:, None] * lanes)
        diff = j[None, :] - j[:, None]
        mask = diff > 0
    else:
        xi = jnp.exp((j + 1.0)[:, None] * lanes)
        zeta = jnp.exp((c - 1.0 - j)[:, None] * lanes)
        diff = j[:, None] - j[None, :]
        mask = diff >= 0
    dmat = jnp.where(mask[None], jnp.exp(jnp.where(mask, diff, 0.0)[None] * log_gamma[:, None, None]), 0.0)
    dec = jnp.exp(c * lanes)
    return xi, zeta, dmat, dec


def _mixers_pre(x, mod, lw, tables):
    zc, zm, zr, zt = _inproj(x, mod, lw['g_norm1'], lw['w_in'])
    y_conv = _conv_mixer(zc, lw['conv_w'], lw['g_out_conv'])
    q, k, v = _mla_prep(zm, tables['mla_cos'], tables['mla_sin'], lw['g_kv_norm'], lw['w_uk'],
                        lw['w_uv'], lw['g_q'], lw['g_k'])
    rw = _rwkv_proj(zr, lw['rwkv_mu'], lw['rwkv_k_k'], lw['rwkv_k_a'], lw['rwkv_w0'], lw['rwkv_a0'], lw['lora'])
    return dict(y_conv=y_conv, q=q, k=k, v=v, rw=rw, zt=zt)


def _rwkv_stream(rw, s_fw, s_bw, lw, emit):
    r, v, kk, gl, ks, lw0, b0, kt0, lw1, b1, kt1 = rw
    yf, yb, s_fw, s_bw = _rwkv_scan((lw0, b0, kt0), (lw1, b1, kt1), kk, v, r, s_fw, s_bw)
    y = _rwkv_out(yf, yb, r, v, ks, gl, lw['rwkv_g2'], lw['rwkv_r_k'], lw['g_ln_x']) if emit else None
    return y, s_fw, s_bw


def _ret_stream(zt, r_fw, r_bw, lw, tables, emit):
    yf, yb, r_fw, r_bw = _ret_scan(zt, tables['ret_cos'], tables['ret_sin'], r_fw, r_bw)
    y = _ret_out(yf, yb, zt, lw['g_ret_norm']) if emit else None
    return y, r_fw, r_bw


def _channel_mix(ys, x, mod, lw):
    x, h, logits = _outproj(ys, x, mod, lw['w_out'], lw['g_norm2'], lw['w_router'], lw['b_router'])
    cap = CAP_FACTOR * x.shape[1] // N_EXP
    rank, gate = _route(jnp.swapaxes(logits, 1, 2), cap)
    xe, gr = _gather(h, rank, gate, cap)
    ye = _expert_ffn(xe, gr, lw['w_gate'], lw['w_up'], lw['w_down'])
    return _combine(x, mod, rank, ye, cap)


def kernel(x, c, ctx, c_ctx, w_ada, b_ada, g_norm1, g_norm2, w_in, w_out, conv_w, g_out_conv, g_kv_norm, w_uk, w_uv, g_q_norm, g_k_norm, g_out_mla, rwkv_mu, rwkv_w0, rwkv_w2, rwkv_a0, rwkv_a2, rwkv_g2, rwkv_k_k, rwkv_k_a, rwkv_r_k, g_ln_x, g_ret_norm, w_router, b_router, w_gate, w_up, w_down):
    depth = w_in.shape[0]
    bsz, lq, d = x.shape
    lc = ctx.shape[1]

    cc = jnp.zeros((16, d), F32).at[:bsz].set(c).at[bsz].set(c_ctx)
    mods = _mods(cc, w_ada, b_ada)

    cos_x, sin_x = _mla_tables(lq)
    rcos_x, rsin_x = _ret_tables(lq)
    tab_x = dict(mla_cos=cos_x, mla_sin=sin_x, ret_cos=rcos_x, ret_sin=rsin_x)
    tab_c = dict(mla_cos=jnp.ones((lc, MLA_SLOT), F32), mla_sin=jnp.zeros((lc, MLA_SLOT), F32),
                 ret_cos=jnp.ones((lc, GROUP_W), F32), ret_sin=jnp.zeros((lc, GROUP_W), F32))
    zero_state = jnp.zeros((bsz, GROUP_W, GROUP_W), F32)

    xc = ctx
    for l in range(depth):
        need_ctx = l < depth - 1
        lw = dict(
            g_norm1=g_norm1[l], g_norm2=g_norm2[l], w_in=_layout_w_in(w_in[l]), w_out=w_out[l].astype(BF16),
            conv_w=conv_w[l], g_out_conv=g_out_conv[l], g_kv_norm=g_kv_norm[l],
            w_uk=_layout_w_uk(w_uk[l]), w_uv=_split_w(w_uv[l]),
            g_q=_slot_gain(g_q_norm[l]), g_k=_slot_gain(g_k_norm[l]), g_out_mla=g_out_mla[l],
            rwkv_mu=rwkv_mu[l], rwkv_w0=rwkv_w0[l], rwkv_a0=rwkv_a0[l],
            lora=_layout_lora(rwkv_w2[l], rwkv_a2[l]), rwkv_g2=_split_w(rwkv_g2[l]),
            rwkv_k_k=rwkv_k_k[l], rwkv_k_a=rwkv_k_a[l], rwkv_r_k=rwkv_r_k[l], g_ln_x=g_ln_x[l],
            g_ret_norm=g_ret_norm[l], w_router=w_router[l], b_router=b_router[l],
            w_gate=w_gate[l], w_up=w_up[l], w_down=w_down[l])
        mod_x = mods[l, :bsz].reshape(bsz, 6, d)
        mod_c = jnp.broadcast_to(mods[l, bsz].reshape(1, 6, d), (bsz, 6, d))

        pc = _mixers_pre(xc, mod_c, lw, tab_c)
        px = _mixers_pre(x, mod_x, lw, tab_x)

        k_all = jnp.concatenate([pc['k'], px['k']], axis=1)
        v_all = jnp.concatenate([pc['v'], px['v']], axis=1)
        ym_x = _attention(px['q'], k_all, v_all, lw['g_out_mla'])
        yr_c, s_fw, s_bw = _rwkv_stream(pc['rw'], zero_state, zero_state, lw, need_ctx)
        yr_x, _, _ = _rwkv_stream(px['rw'], s_fw, s_bw, lw, True)
        yt_c, r_fw, r_bw = _ret_stream(pc['zt'], zero_state, zero_state, lw, tab_c, need_ctx)
        yt_x, _, _ = _ret_stream(px['zt'], r_fw, r_bw, lw, tab_x, True)

        x = _channel_mix((px['y_conv'], ym_x, yr_x, yt_x), x, mod_x, lw)
        if need_ctx:
            ym_c = _attention(pc['q'], pc['k'], pc['v'], lw['g_out_mla'])
            xc = _channel_mix((pc['y_conv'], ym_c, yr_c, yt_c), xc, mod_c, lw)
    return x
```

```python
import functools

import jax
import jax.numpy as jnp
from jax import lax
from jax.experimental import pallas as pl
from jax.experimental.pallas import tpu as pltpu

F32 = jnp.float32
BF16 = jnp.bfloat16
HI = lax.Precision.HIGHEST

D_MODEL = 1024
N_HEADS = 4
HEAD_DIM = 64
GROUP_W = N_HEADS * HEAD_DIM
NORM_EPS = 1e-6
MLA_NOPE = 64
MLA_ROPE = 32
MLA_QK = MLA_NOPE + MLA_ROPE
MLA_SLOT = 128
KV_RANK = 128
ROPE_BASE = 10000.0
DECAY_LORA = 64
ICLR_LORA = 64
GATE_LORA = 128
RWKV_GN_EPS = 64e-5
RWKV_CHUNK = 64
RWKV_INV_BASE = 16
RWKV_BATCH = 4
RET_CHUNK = 256
RET_THETA_BASE = 10000.0
N_EXP = 16
CAP_FACTOR = 2
GRID_W = 64

COLS_CONV = 3 * GROUP_W
COLS_MLA_IN = N_HEADS * MLA_QK + KV_RANK + MLA_ROPE
COLS_MLA = N_HEADS * MLA_SLOT + MLA_SLOT + KV_RANK
COLS_RWKV = 3 * GROUP_W + DECAY_LORA + ICLR_LORA + GATE_LORA
COLS_RET = 4 * GROUP_W

TOKEN_TILE = 256
MOE_TOKEN_SLICE = 1024
FFN_BATCH = 2
COMBINE_EXPERTS = 4
COMBINE_TOKEN_SLICE = 512
VMEM_LIMIT = 56 * 1024 * 1024

NN = (((1,), (0,)), ((), ()))
NT = (((1,), (1,)), ((), ()))
TN = (((0,), (0,)), ((), ()))


def _cparams(*sem):
    return pltpu.CompilerParams(dimension_semantics=sem, vmem_limit_bytes=VMEM_LIMIT)


def _dg(a, b, dims=NN, precision=None):
    return lax.dot_general(a, b, dims, precision=precision, preferred_element_type=F32)


def _split(a):
    hi = a.astype(BF16)
    return hi, (a - hi.astype(F32)).astype(BF16)


def _mm3(a, b, dims=NN):
    ah, al = a if isinstance(a, tuple) else _split(a)
    bh, bl = b if isinstance(b, tuple) else _split(b)
    return _dg(ah, bh, dims) + _dg(ah, bl, dims) + _dg(al, bh, dims)


def _mm(a, b, dims=NN):
    return _dg(a.astype(BF16), b.astype(BF16), dims)


def _row_chains(tm, body):
    sub = min(TOKEN_TILE, tm)
    _interleave(*[body(slice(t * sub, (t + 1) * sub)) for t in range(tm // sub)])


def _head_ones(n, width, scale=1.0):
    r = lax.broadcasted_iota(jnp.int32, (n, n), 0) // width
    c = lax.broadcasted_iota(jnp.int32, (n, n), 1) // width
    return jnp.where(r == c, scale, 0.0).astype(F32)


def _lane_head(shape, width):
    return lax.broadcasted_iota(jnp.int32, shape, len(shape) - 1) // width


def _sigmoid(x):
    return 1.0 / (1.0 + jnp.exp(-x))


def _interleave(*chains):
    results = [None] * len(chains)
    live = list(range(len(chains)))
    while live:
        for n in list(live):
            try:
                next(chains[n])
            except StopIteration as done:
                results[n] = done.value
                live.remove(n)
    return results


def _mods_kernel(cc_ref, w_ref, b_ref, o_ref):
    cc = cc_ref[...]
    o_ref[0] = _dg(cc * _sigmoid(cc), w_ref[0], NN, HI) + b_ref[0]


def _mods(cc, w_ada, b_ada):
    depth, d, n = w_ada.shape
    tn = 1536
    return pl.pallas_call(
        _mods_kernel,
        grid=(depth, n // tn),
        in_specs=[pl.BlockSpec(cc.shape, lambda l, j: (0, 0)),
                  pl.BlockSpec((1, d, tn), lambda l, j: (l, 0, j)),
                  pl.BlockSpec((1, 1, tn), lambda l, j: (l, 0, j))],
        out_specs=pl.BlockSpec((1, cc.shape[0], tn), lambda l, j: (l, 0, j)),
        out_shape=jax.ShapeDtypeStruct((depth, cc.shape[0], n), F32),
        compiler_params=_cparams("parallel", "parallel"),
        name="adaln_mods",
    )(cc, w_ada, b_ada.reshape(depth, 1, n))


def _inproj_kernel(x_ref, mod_ref, g_ref, w_ref, zc_ref, zm_ref, zr_ref, zt_ref):
    m = mod_ref[0]

    def rows(tok):
        x = x_ref[0, tok, :]
        xn = x * lax.rsqrt(jnp.mean(x * x, axis=-1, keepdims=True) + NORM_EPS) * g_ref[...]
        h = (xn * (1.0 + m[1:2]) + m[0:1]).astype(BF16)
        yield
        o = 0
        for ref, n in ((zc_ref, COLS_CONV), (zm_ref, COLS_MLA), (zr_ref, COLS_RWKV), (zt_ref, COLS_RET)):
            ref[0, tok, :] = _dg(h, w_ref[:, o:o + n])
            o += n
            yield

    _row_chains(x_ref.shape[1], rows)


def _inproj(x, mod, g, w):
    b, lq, d = x.shape
    tm = min(2 * TOKEN_TILE, lq)
    widths = (COLS_CONV, COLS_MLA, COLS_RWKV, COLS_RET)
    return pl.pallas_call(
        _inproj_kernel,
        grid=(b, lq // tm),
        in_specs=[pl.BlockSpec((1, tm, d), lambda i, j: (i, j, 0)),
                  pl.BlockSpec((1, 6, d), lambda i, j: (i, 0, 0)),
                  pl.BlockSpec((1, d), lambda i, j: (0, 0)),
                  pl.BlockSpec(w.shape, lambda i, j: (0, 0))],
        out_specs=[pl.BlockSpec((1, tm, n), lambda i, j: (i, j, 0)) for n in widths],
        out_shape=[jax.ShapeDtypeStruct((b, lq, n), F32) for n in widths],
        compiler_params=_cparams("parallel", "parallel"),
        name="inproj",
    )(x, mod, g.reshape(1, d), w)


def _halo_specs(tm, lq, width):
    r = tm // 8
    last = lq // 8 - 1
    return [pl.BlockSpec((1, tm, width), lambda i, j: (i, j, 0)),
            pl.BlockSpec((1, 8, width), lambda i, j: (i, jnp.maximum(j * r - 1, 0), 0)),
            pl.BlockSpec((1, 8, width), lambda i, j: (i, jnp.minimum((j + 1) * r, last), 0))]


def _shifted(t, prev_row, next_row):
    n = t.shape[0]
    rows = lax.broadcasted_iota(jnp.int32, t.shape, 0)
    t_prev = jnp.where(rows == 0, prev_row, pltpu.roll(t, 1, 0))
    t_next = jnp.where(rows == n - 1, next_row, pltpu.roll(t, n - 1, 0))
    return t_prev, t_next


def _conv_kernel(z_ref, zp_ref, zn_ref, cw_ref, g_ref, o_ref, *, nt):
    j = pl.program_id(1)
    w = GROUP_W
    z = z_ref[0]
    bgate = z[:, :w]
    u = z[:, w:2 * w] * z[:, 2 * w:]
    zp = zp_ref[0][7:8]
    zn = zn_ref[0][0:1]
    up = jnp.where(j > 0, zp[:, w:2 * w] * zp[:, 2 * w:], 0.0)
    un = jnp.where(j < nt - 1, zn[:, w:2 * w] * zn[:, 2 * w:], 0.0)
    u_prev, u_next = _shifted(u, up, un)
    cw = cw_ref[...]
    t = bgate * (cw[0:1] * u_prev + cw[1:2] * u + cw[2:3] * u_next)
    o_ref[0] = t * lax.rsqrt(jnp.mean(t * t, axis=-1, keepdims=True) + NORM_EPS) * g_ref[...]


def _conv_mixer(zc, conv_w, g_out):
    b, lq, width = zc.shape
    tm = min(TOKEN_TILE, lq)
    return pl.pallas_call(
        functools.partial(_conv_kernel, nt=lq // tm),
        grid=(b, lq // tm),
        in_specs=_halo_specs(tm, lq, width) + [
            pl.BlockSpec((3, GROUP_W), lambda i, j: (0, 0)),
            pl.BlockSpec((1, GROUP_W), lambda i, j: (0, 0))],
        out_specs=pl.BlockSpec((1, tm, GROUP_W), lambda i, j: (i, j, 0)),
        out_shape=jax.ShapeDtypeStruct((b, lq, GROUP_W), F32),
        compiler_params=_cparams("parallel", "parallel"),
        name="conv_mixer",
    )(zc, zc, zc, conv_w, g_out.reshape(1, GROUP_W))


def _tile4(t):
    return jnp.concatenate([t, t, t, t], axis=-1)


def _slot_norm_rope(t, g, cos, sin):
    parts = []
    for h in range(N_HEADS):
        s = t[:, h * MLA_SLOT:(h + 1) * MLA_SLOT]
        ms = jnp.sum(s * s, axis=-1, keepdims=True) * (1.0 / MLA_QK)
        parts.append(s * lax.rsqrt(ms + NORM_EPS) * g)
    tn = jnp.concatenate(parts, axis=-1)
    n = tn.shape[-1]
    lane = lax.broadcasted_iota(jnp.int32, tn.shape, 1)
    half = MLA_ROPE // 4
    partner = jnp.where(lane % (2 * half) < half, pltpu.roll(tn, n - half, 1), pltpu.roll(tn, half, 1))
    return tn * _tile4(cos) + partner * _tile4(sin)


def _mla_prep_kernel(z_ref, cos_ref, sin_ref, gkv_ref, wukh_ref, wukl_ref, wuvh_ref, wuvl_ref, gq_ref, gk_ref,
                     q_ref, k_ref, v_ref):
    nq = N_HEADS * MLA_SLOT

    def rows(tok):
        z = z_ref[0, tok, :]
        q_in = z[:, :nq]
        k_rope = z[:, nq:nq + MLA_SLOT]
        ckv = z[:, nq + MLA_SLOT:]
        ckv = _split(ckv * lax.rsqrt(jnp.mean(ckv * ckv, axis=-1, keepdims=True) + NORM_EPS) * gkv_ref[...])
        k_in = _mm3(ckv, (wukh_ref[...], wukl_ref[...])) + _tile4(k_rope)
        v_ref[0, tok, :] = _mm3(ckv, (wuvh_ref[...], wuvl_ref[...])).astype(BF16)
        yield
        cos = cos_ref[tok, :]
        sin = sin_ref[tok, :]
        q_ref[0, tok, :] = (_slot_norm_rope(q_in, gq_ref[...], cos, sin) * (MLA_QK ** -0.5)).astype(BF16)
        yield
        k_ref[0, tok, :] = _slot_norm_rope(k_in, gk_ref[...], cos, sin).astype(BF16)

    _row_chains(z_ref.shape[1], rows)


def _mla_prep(zm, cos, sin, g_kv, wuk, wuv, g_q, g_k):
    b, lq, width = zm.shape
    tm = min(2 * TOKEN_TILE, lq)
    nq = N_HEADS * MLA_SLOT
    const = lambda shape: pl.BlockSpec(shape, lambda i, j: (0, 0))
    tok = lambda i, j: (i, j, 0)
    return pl.pallas_call(
        _mla_prep_kernel,
        grid=(b, lq // tm),
        in_specs=[pl.BlockSpec((1, tm, width), tok),
                  pl.BlockSpec((tm, MLA_SLOT), lambda i, j: (j, 0)),
                  pl.BlockSpec((tm, MLA_SLOT), lambda i, j: (j, 0)),
                  const((1, KV_RANK)), const(wuk[0].shape), const(wuk[0].shape),
                  const(wuv[0].shape), const(wuv[0].shape),
                  const((1, MLA_SLOT)), const((1, MLA_SLOT))],
        out_specs=[pl.BlockSpec((1, tm, nq), tok), pl.BlockSpec((1, tm, nq), tok), pl.BlockSpec((1, tm, GROUP_W), tok)],
        out_shape=[jax.ShapeDtypeStruct((b, lq, nq), BF16)] * 2 + [jax.ShapeDtypeStruct((b, lq, GROUP_W), BF16)],
        compiler_params=_cparams("parallel", "parallel"),
        name="mla_prep",
    )(zm, cos, sin, g_kv.reshape(1, KV_RANK), *wuk, *wuv, g_q, g_k)


def _attn_head(q, k, v):
    s = _dg(q, k, NT)
    yield
    p = jnp.exp(s - jnp.max(s, axis=-1, keepdims=True))
    inv = 1.0 / jnp.sum(p, axis=-1, keepdims=True)
    return _mm(p, v) * inv


def _attn_kernel(q_ref, k_ref, v_ref, g_ref, o_ref):
    v = v_ref[0]
    tq = q_ref.shape[1]
    head = _lane_head((tq, GROUP_W), HEAD_DIM)
    slots = [slice(h * MLA_SLOT, (h + 1) * MLA_SLOT) for h in range(N_HEADS)]
    outs = []
    for pair in (slots[:2], slots[2:]):
        outs += _interleave(*[_attn_head(q_ref[0, :, sl], k_ref[0, :, sl], v) for sl in pair])
    out = outs[0]
    for h in range(1, N_HEADS):
        out = jnp.where(head == h, outs[h], out)
    o_ref[0] = out * lax.rsqrt(jnp.mean(out * out, axis=-1, keepdims=True) + NORM_EPS) * g_ref[...]


def _attention(q, k, v, g_out):
    b, lq, nq = q.shape
    lk = k.shape[1]
    tq = min(2 * TOKEN_TILE, lq)
    return pl.pallas_call(
        _attn_kernel,
        grid=(b, lq // tq),
        in_specs=[pl.BlockSpec((1, tq, nq), lambda i, j: (i, j, 0)),
                  pl.BlockSpec((1, lk, nq), lambda i, j: (i, 0, 0)),
                  pl.BlockSpec((1, lk, GROUP_W), lambda i, j: (i, 0, 0)),
                  pl.BlockSpec((1, GROUP_W), lambda i, j: (0, 0))],
        out_specs=pl.BlockSpec((1, tq, GROUP_W), lambda i, j: (i, j, 0)),
        out_shape=jax.ShapeDtypeStruct((b, lq, GROUP_W), F32),
        compiler_params=_cparams("parallel", "arbitrary"),
        name="mla_attention",
    )(q, k, v, g_out.reshape(1, GROUP_W))


def _rwkv_proj_kernel(z_ref, zp_ref, zn_ref, mu_ref, kk_ref, ka_ref, w0_ref, a0_ref, lora_ref,
                      r_ref, v_ref, kkn_ref, gl_ref, ks_ref,
                      lw0_ref, b0_ref, kt0_ref, lw1_ref, b1_ref, kt1_ref, *, nt):
    j = pl.program_id(1)
    w = GROUP_W
    z = z_ref[0]
    zp = jnp.where(j > 0, zp_ref[0][7:8], 0.0)
    zn = jnp.where(j < nt - 1, zn_ref[0][0:1], 0.0)
    z_prev, z_next = _shifted(z, zp, zn)
    z_mix = z + mu_ref[...] * (0.5 * (z_prev + z_next) - z)

    def rows(tok):
        zs = z_mix[tok]
        r = zs[:, :w]
        k = zs[:, w:2 * w]
        lora_in = zs[:, 3 * w:3 * w + DECAY_LORA + ICLR_LORA]
        kk = k * kk_ref[...]
        kk = kk * lax.rsqrt(_dg(kk * kk, _head_ones(w, HEAD_DIM), NN, HI) + 1e-12)
        lane = lax.broadcasted_iota(jnp.int32, lora_in.shape, 1)
        lora_in = jnp.where(lane < DECAY_LORA, jnp.tanh(lora_in), lora_in)
        r_ref[0, tok, :] = r
        v_ref[0, tok, :] = zs[:, 2 * w:3 * w]
        kkn_ref[0, tok, :] = kk
        gl_ref[0, tok, :] = zs[:, 3 * w + DECAY_LORA + ICLR_LORA:]
        ksum = jnp.zeros_like(k)
        for d, (lw_ref, b_ref, kt_ref) in enumerate(((lw0_ref, b0_ref, kt0_ref), (lw1_ref, b1_ref, kt1_ref))):
            lo = _dg(lora_in, lora_ref[d], NN, HI)
            yield
            t = -(w0_ref[d:d + 1] + lo[:, :w])
            softplus = jnp.maximum(t, 0.0) + jnp.log1p(jnp.exp(-jnp.abs(t)))
            lw_ref[0, tok, :] = -jnp.exp(-softplus - 0.5)
            a = _sigmoid(a0_ref[d:d + 1] + lo[:, w:])
            kt = k * (1.0 + (a - 1.0) * ka_ref[...])
            b_ref[0, tok, :] = kk * a
            kt_ref[0, tok, :] = kt
            ksum = ksum + kt
        ks_ref[0, tok, :] = ksum

    _row_chains(z.shape[0], rows)


def _rwkv_proj(zr, mu, k_k, k_a, w0, a0, lora):
    b, lq, width = zr.shape
    tm = min(2 * TOKEN_TILE, lq)
    w = GROUP_W
    const2 = lambda shape: pl.BlockSpec(shape, lambda i, j: (0,) * len(shape))
    out_w = (w, w, w, GATE_LORA, w, w, w, w, w, w, w)
    return pl.pallas_call(
        functools.partial(_rwkv_proj_kernel, nt=lq // tm),
        grid=(b, lq // tm),
        in_specs=_halo_specs(tm, lq, width) + [
            const2((1, width)), const2((1, w)), const2((1, w)), const2((2, w)), const2((2, w)),
            const2(lora.shape)],
        out_specs=[pl.BlockSpec((1, tm, n), lambda i, j: (i, j, 0)) for n in out_w],
        out_shape=[jax.ShapeDtypeStruct((b, lq, n), F32) for n in out_w],
        compiler_params=_cparams("parallel", "parallel"),
        name="rwkv_proj",
    )(zr, zr, zr, mu.reshape(1, width), k_k.reshape(1, w), k_a.reshape(1, w), w0, a0, lora)


def _stack_heads(t):
    head = _lane_head(t.shape, HEAD_DIM)
    return jnp.concatenate([jnp.where(head == h, t, 0.0) for h in range(N_HEADS)], axis=0)


def _unit_lower_inverse(n_mat, p_i, q_i, size):
    base = RWKV_INV_BASE
    same = lambda w: (p_i // w) == (q_i // w)
    m = jnp.where(same(base), -n_mat, 0.0)
    inv = jnp.where(p_i == q_i, 1.0, 0.0) + m
    pw = m
    span = 1
    while 2 * span < base:
        pw = _mm(pw, pw)
        yield
        inv = inv + _mm(inv, pw)
        yield
        span *= 2
    w = base
    while w < size:
        off = jnp.where(same(2 * w) & jnp.logical_not(same(w)), n_mat, 0.0)
        invb = inv.astype(BF16)
        left = _mm(invb, off)
        yield
        inv = inv - _mm(left, invb)
        yield
        w *= 2
    return inv


def _rwkv_chunk(lw, kk, b, kt, v, r, s, reverse):
    c = RWKV_CHUNK
    n = N_HEADS * c
    t_i = lax.broadcasted_iota(jnp.int32, (c, c), 0)
    s_i = lax.broadcasted_iota(jnp.int32, (c, c), 1)
    seen = (s_i >= t_i) if reverse else (s_i <= t_i)
    g = _dg(jnp.where(seen, 1.0, 0.0).astype(F32), lw, NN, HI)
    g_tot = g[0:1] if reverse else g[c - 1:c]
    e_neg = jnp.exp(-g)
    e_tot = jnp.exp(g_tot)
    bd = _stack_heads(b * e_neg)
    ktd = _stack_heads(kt * e_neg)
    vs = _stack_heads(v).astype(BF16)
    lhs = jnp.concatenate([_stack_heads(kk * jnp.exp(g - lw)), _stack_heads(r * jnp.exp(g))], axis=0).astype(BF16)
    gram = _mm(lhs, jnp.concatenate([bd, ktd], axis=0), NT)
    from_state = _mm(lhs, s, NT)
    yield

    p_i = lax.broadcasted_iota(jnp.int32, (n, n), 0)
    q_i = lax.broadcasted_iota(jnp.int32, (n, n), 1)
    strict = (q_i > p_i) if reverse else (q_i < p_i)
    incl = (q_i >= p_i) if reverse else (q_i <= p_i)
    n_ab = jnp.where(strict, gram[:n, :n], 0.0)
    n_ak = jnp.where(strict, gram[:n, n:], 0.0)
    a_rb = jnp.where(incl, gram[n:, :n], 0.0)
    a_rk = jnp.where(incl, gram[n:, n:], 0.0)
    from_v = _mm(jnp.concatenate([n_ak, a_rk], axis=0), vs)
    t_inv = yield from _unit_lower_inverse(n_ab, p_i, q_i, c)
    u = (-_mm(t_inv, from_state[:n] + from_v[:n])).astype(BF16)
    yield
    y = from_state[n:] + _mm(a_rb, u) + from_v[n:]
    y = y[0:c] + y[c:2 * c] + y[2 * c:3 * c] + y[3 * c:]
    s_new = s * e_tot + _mm(jnp.concatenate([u, vs], axis=0),
                            jnp.concatenate([bd * e_tot, ktd * e_tot], axis=0), TN)
    return y, s_new


def _rwkv_scan_kernel(lwf_ref, kkf_ref, bf_ref, ktf_ref, vf_ref, rf_ref,
                      lwb_ref, kkb_ref, bb_ref, ktb_ref, vb_ref, rb_ref, sf0_ref, sb0_ref,
                      yf_ref, yb_ref, sfo_ref, sbo_ref, sf_ref, sb_ref, *, nc):
    i = pl.program_id(1)

    @pl.when(i == 0)
    def _():
        sf_ref[...] = sf0_ref[...]
        sb_ref[...] = sb0_ref[...]

    chains = []
    for n in range(RWKV_BATCH):
        chains.append(_rwkv_chunk(lwf_ref[n], kkf_ref[n], bf_ref[n], ktf_ref[n], vf_ref[n], rf_ref[n], sf_ref[n], False))
        chains.append(_rwkv_chunk(lwb_ref[n], kkb_ref[n], bb_ref[n], ktb_ref[n], vb_ref[n], rb_ref[n], sb_ref[n], True))
    results = _interleave(*chains)
    for n in range(RWKV_BATCH):
        (yf, sf), (yb, sb) = results[2 * n], results[2 * n + 1]
        yf_ref[n] = yf
        yb_ref[n] = yb
        sf_ref[n] = sf
        sb_ref[n] = sb

    @pl.when(i == nc - 1)
    def _():
        sfo_ref[...] = sf_ref[...]
        sbo_ref[...] = sb_ref[...]


def _rwkv_scan(fw, bw, kk, v, r, s_fw, s_bw):
    b, lq, w = kk.shape
    c = RWKV_CHUNK
    nc = lq // c
    nb = RWKV_BATCH
    fspec = pl.BlockSpec((nb, c, w), lambda i, j: (i, j, 0))
    bspec = pl.BlockSpec((nb, c, w), lambda i, j: (i, nc - 1 - j, 0))
    sspec = pl.BlockSpec((nb, w, w), lambda i, j: (i, 0, 0))
    yshape = jax.ShapeDtypeStruct((b, lq, w), F32)
    sshape = jax.ShapeDtypeStruct((b, w, w), F32)
    return pl.pallas_call(
        functools.partial(_rwkv_scan_kernel, nc=nc),
        grid=(b // nb, nc),
        in_specs=[fspec] * 6 + [bspec] * 6 + [sspec, sspec],
        out_specs=[fspec, bspec, sspec, sspec],
        out_shape=[yshape, yshape, sshape, sshape],
        scratch_shapes=[pltpu.VMEM((nb, w, w), F32), pltpu.VMEM((nb, w, w), F32)],
        compiler_params=_cparams("parallel", "arbitrary"),
        name="rwkv_scan",
    )(fw[0], kk, fw[1], fw[2], v, r, bw[0], kk, bw[1], bw[2], v, r, s_fw, s_bw)


def _rwkv_out_kernel(yf_ref, yb_ref, r_ref, v_ref, ks_ref, gl_ref, g2h_ref, g2l_ref, rk_ref, gln_ref, o_ref):
    avg = _head_ones(GROUP_W, HEAD_DIM, 1.0 / HEAD_DIM)

    def rows(tok):
        y = yf_ref[0, tok, :] + yb_ref[0, tok, :]
        bonus = _dg(r_ref[0, tok, :] * ks_ref[0, tok, :] * rk_ref[...], _head_ones(GROUP_W, HEAD_DIM), NN, HI)
        gate = _mm3(_sigmoid(gl_ref[0, tok, :]), (g2h_ref[...], g2l_ref[...]))
        dlt = y - _dg(y, avg, NN, HI)
        yield
        var = _dg(dlt * dlt, avg, NN, HI)
        yield
        yn = dlt * lax.rsqrt(var + RWKV_GN_EPS) * gln_ref[...] + bonus * v_ref[0, tok, :]
        o_ref[0, tok, :] = yn * gate

    _row_chains(yf_ref.shape[1], rows)


def _rwkv_out(yf, yb, r, v, ks, gl, g2, r_k, g_ln):
    b, lq, w = yf.shape
    tm = min(2 * TOKEN_TILE, lq)
    tspec = pl.BlockSpec((1, tm, w), lambda i, j: (i, j, 0))
    const = lambda shape: pl.BlockSpec(shape, lambda i, j: (0, 0))
    return pl.pallas_call(
        _rwkv_out_kernel,
        grid=(b, lq // tm),
        in_specs=[tspec] * 5 + [pl.BlockSpec((1, tm, GATE_LORA), lambda i, j: (i, j, 0)),
                                const(g2[0].shape), const(g2[0].shape), const((1, w)), const((1, w))],
        out_specs=tspec,
        out_shape=jax.ShapeDtypeStruct((b, lq, w), F32),
        compiler_params=_cparams("parallel", "parallel"),
        name="rwkv_out",
    )(yf, yb, r, v, ks, gl, *g2, r_k.reshape(1, w), g_ln.reshape(1, w))


def _half_rotate(t, cos, sin):
    n = t.shape[-1]
    lane = lax.broadcasted_iota(jnp.int32, t.shape, 1)
    half = HEAD_DIM // 2
    partner = jnp.where(lane % HEAD_DIM < half, pltpu.roll(t, n - half, 1), pltpu.roll(t, half, 1))
    return t * cos + partner * sin


def _ret_chunk(z, cos, sin, xi, zeta, dmat_ref, dec, st):
    w = GROUP_W
    q = _half_rotate(z[:, :w] * (HEAD_DIM ** -0.5), cos, sin)
    k = _half_rotate(z[:, w:2 * w], cos, sin)
    vs = z[:, 2 * w:3 * w].astype(BF16)
    ks = k.astype(BF16)
    head = _lane_head(q.shape, HEAD_DIM)
    y = _mm(q * xi, st)
    upd = _mm(k * zeta, vs, TN)
    scores = [_mm(jnp.where(head == h, q, 0.0), ks, NT) * dmat_ref[h] for h in range(N_HEADS)]
    yield
    for h in range(N_HEADS):
        y = y + jnp.where(head == h, _mm(scores[h], vs), 0.0)
    return y, st * dec + upd * _head_ones(w, HEAD_DIM)


def _ret_scan_kernel(zf_ref, cosf_ref, sinf_ref, zb_ref, cosb_ref, sinb_ref,
                     xif_ref, zetaf_ref, dmatf_ref, xib_ref, zetab_ref, dmatb_ref, dec_ref, rf0_ref, rb0_ref,
                     yf_ref, yb_ref, rfo_ref, rbo_ref, stf_ref, stb_ref, *, nc):
    i = pl.program_id(1)

    @pl.when(i == 0)
    def _():
        stf_ref[...] = rf0_ref[0]
        stb_ref[...] = rb0_ref[0]

    dec = dec_ref[...]
    (yf, stf), (yb, stb) = _interleave(
        _ret_chunk(zf_ref[0], cosf_ref[...], sinf_ref[...], xif_ref[...], zetaf_ref[...], dmatf_ref, dec, stf_ref[...]),
        _ret_chunk(zb_ref[0], cosb_ref[...], sinb_ref[...], xib_ref[...], zetab_ref[...], dmatb_ref, dec, stb_ref[...]))
    yf_ref[0] = yf
    yb_ref[0] = yb
    stf_ref[...] = stf
    stb_ref[...] = stb

    @pl.when(i == nc - 1)
    def _():
        rfo_ref[0] = stf
        rbo_ref[0] = stb


def _ret_scan(zt, cos, sin, r_fw, r_bw):
    b, lq, width = zt.shape
    c = min(RET_CHUNK, lq)
    nc = lq // c
    w = GROUP_W
    xi_f, zeta_f, dmat_f, dec = _ret_decay_tables(c, False)
    xi_b, zeta_b, dmat_b, _ = _ret_decay_tables(c, True)
    fw3 = lambda i, j: (i, j, 0)
    bw3 = lambda i, j: (i, nc - 1 - j, 0)
    fw2 = lambda i, j: (j, 0)
    bw2 = lambda i, j: (nc - 1 - j, 0)
    const = lambda shape: pl.BlockSpec(shape, lambda i, j: (0,) * len(shape))
    sspec = pl.BlockSpec((1, w, w), lambda i, j: (i, 0, 0))
    tables = [const((c, w)), const((c, w)), const((N_HEADS, c, c))]
    yshape = jax.ShapeDtypeStruct((b, lq, w), F32)
    sshape = jax.ShapeDtypeStruct((b, w, w), F32)
    return pl.pallas_call(
        functools.partial(_ret_scan_kernel, nc=nc),
        grid=(b, nc),
        in_specs=[pl.BlockSpec((1, c, width), fw3), pl.BlockSpec((c, w), fw2), pl.BlockSpec((c, w), fw2),
                  pl.BlockSpec((1, c, width), bw3), pl.BlockSpec((c, w), bw2), pl.BlockSpec((c, w), bw2)]
        + tables + tables + [const((1, w)), sspec, sspec],
        out_specs=[pl.BlockSpec((1, c, w), fw3), pl.BlockSpec((1, c, w), bw3), sspec, sspec],
        out_shape=[yshape, yshape, sshape, sshape],
        scratch_shapes=[pltpu.VMEM((w, w), F32), pltpu.VMEM((w, w), F32)],
        compiler_params=_cparams("parallel", "arbitrary"),
        name="ret_scan",
    )(zt, cos, sin, zt, cos, sin, xi_f, zeta_f, dmat_f, xi_b, zeta_b, dmat_b, dec, r_fw, r_bw)


def _ret_out_kernel(yf_ref, yb_ref, z_ref, g_ref, o_ref):
    y = yf_ref[0] + yb_ref[0]
    ms = _dg(y * y, _head_ones(GROUP_W, HEAD_DIM, 1.0 / HEAD_DIM), NN, HI)
    gate = z_ref[0]
    o_ref[0] = gate * _sigmoid(gate) * (y * lax.rsqrt(ms + NORM_EPS) * g_ref[...])


def _ret_out(yf, yb, zt, g_norm):
    b, lq, w = yf.shape
    tm = min(TOKEN_TILE, lq)
    tspec = pl.BlockSpec((1, tm, w), lambda i, j: (i, j, 0))
    return pl.pallas_call(
        _ret_out_kernel,
        grid=(b, lq // tm),
        in_specs=[tspec, tspec, pl.BlockSpec((1, tm, w), lambda i, j: (i, j, 3)),
                  pl.BlockSpec((1, w), lambda i, j: (0, 0))],
        out_specs=tspec,
        out_shape=jax.ShapeDtypeStruct((b, lq, w), F32),
        compiler_params=_cparams("parallel", "parallel"),
        name="ret_out",
    )(yf, yb, zt, g_norm.reshape(1, w))


def _outproj_rows(tok, y_refs, x_ref, m, w_ref, g_ref, wr_ref, br_ref, xo_ref, h_ref, lg_ref):
    w = GROUP_W
    acc = None
    for n, ref in enumerate(y_refs):
        part = _mm(ref[0, tok, :], w_ref[n * w:(n + 1) * w, :])
        acc = part if acc is None else acc + part
    yield
    x = x_ref[0, tok, :] + m[2:3] * acc
    xo_ref[0, tok, :] = x
    xn = x * lax.rsqrt(jnp.mean(x * x, axis=-1, keepdims=True) + NORM_EPS) * g_ref[...]
    h = xn * (1.0 + m[4:5]) + m[3:4]
    h_ref[0, tok, :] = h.astype(BF16)
    yield
    lg_ref[0, tok, :] = _dg(h, wr_ref[...], NN, HI) + br_ref[...]


def _outproj_kernel(yc_ref, ym_ref, yr_ref, yt_ref, x_ref, mod_ref, w_ref, g_ref, wr_ref, br_ref,
                    xo_ref, h_ref, lg_ref):
    tm = x_ref.shape[1]
    sub = min(TOKEN_TILE, tm)
    m = mod_ref[0]
    _interleave(*[_outproj_rows(slice(t * sub, (t + 1) * sub), (yc_ref, ym_ref, yr_ref, yt_ref), x_ref, m,
                                w_ref, g_ref, wr_ref, br_ref, xo_ref, h_ref, lg_ref) for t in range(tm // sub)])


def _outproj(ys, x, mod, w_out, g2, w_r, b_r):
    b, lq, d = x.shape
    tm = min(2 * TOKEN_TILE, lq)
    yspec = pl.BlockSpec((1, tm, GROUP_W), lambda i, j: (i, j, 0))
    xspec = pl.BlockSpec((1, tm, d), lambda i, j: (i, j, 0))
    const = lambda shape: pl.BlockSpec(shape, lambda i, j: (0, 0))
    return pl.pallas_call(
        _outproj_kernel,
        grid=(b, lq // tm),
        in_specs=[yspec] * 4 + [xspec, pl.BlockSpec((1, 6, d), lambda i, j: (i, 0, 0)),
                                const(w_out.shape), const((1, d)),
                                const(w_r.shape), const((1, N_EXP))],
        out_specs=[xspec, xspec, pl.BlockSpec((1, tm, N_EXP), lambda i, j: (i, j, 0))],
        out_shape=[jax.ShapeDtypeStruct((b, lq, d), F32), jax.ShapeDtypeStruct((b, lq, d), BF16),
                   jax.ShapeDtypeStruct((b, lq, N_EXP), F32)],
        compiler_params=_cparams("parallel", "parallel"),
        name="outproj",
    )(*ys, x, mod, w_out, g2.reshape(1, d), w_r, b_r.reshape(1, N_EXP))


def _lane_cumsum(m):
    n = m.shape[-1]
    blk = 128
    r_i = lax.broadcasted_iota(jnp.int32, (blk, blk), 0)
    c_i = lax.broadcasted_iota(jnp.int32, (blk, blk), 1)
    tri = jnp.where(r_i <= c_i, 1.0, 0.0).astype(BF16)
    run = jnp.zeros((m.shape[0], 1), F32)
    parts = []
    for j in range(n // blk):
        cs = _dg(m[:, j * blk:(j + 1) * blk].astype(BF16), tri) + run
        parts.append(cs)
        run = cs[:, blk - 1:blk]
    return jnp.concatenate(parts, axis=-1)


def _route_kernel(lg_ref, rank_ref, gate_ref, *, cap):
    lg = lg_ref[0]
    e = jnp.exp(lg - jnp.max(lg, axis=0, keepdims=True))
    aff = e / jnp.sum(e, axis=0, keepdims=True)
    bits = pltpu.bitcast(aff, jnp.int32)
    thr = jnp.zeros((aff.shape[0], 1), jnp.int32)
    for bit in range(30, -1, -1):
        cand = thr | (1 << bit)
        cnt = jnp.sum(jnp.where(bits >= cand, 1.0, 0.0), axis=-1, keepdims=True)
        thr = jnp.where(cnt >= cap, cand, thr)
    above = bits > thr
    tied = bits == thr
    n_above = jnp.sum(jnp.where(above, 1.0, 0.0), axis=-1, keepdims=True)
    tied_f = jnp.where(tied, 1.0, 0.0)
    tied_rank = _lane_cumsum(tied_f) - tied_f
    sel = above | (tied & (tied_rank < cap - n_above))
    sel_f = jnp.where(sel, 1.0, 0.0)
    rank = _lane_cumsum(sel_f) - sel_f
    rank_ref[0] = jnp.where(sel, rank, -1.0).astype(jnp.int32)
    gate_ref[0] = jnp.where(sel, aff, 0.0)


def _route(logits_t, cap):
    b, ne, lq = logits_t.shape
    spec = pl.BlockSpec((1, ne, lq), lambda i: (i, 0, 0))
    return pl.pallas_call(
        functools.partial(_route_kernel, cap=cap),
        grid=(b,),
        in_specs=[spec],
        out_specs=[spec, spec],
        out_shape=[jax.ShapeDtypeStruct((b, ne, lq), jnp.int32), jax.ShapeDtypeStruct((b, ne, lq), F32)],
        compiler_params=_cparams("parallel"),
        name="moe_route",
    )(logits_t)


def _onehot_rows(rank_row, cap):
    r_i = lax.broadcasted_iota(jnp.int32, (cap, rank_row.shape[-1]), 0)
    return r_i == rank_row


def _gather_kernel(h_ref, rank_ref, gate_ref, xe_ref, gr_ref, *, cap):
    lq = h_ref.shape[1]
    ts = min(MOE_TOKEN_SLICE, lq)
    xe = gr = None
    for t in range(lq // ts):
        tok = slice(t * ts, (t + 1) * ts)
        sel = _onehot_rows(rank_ref[0, 0, :, tok], cap)
        px = _dg(jnp.where(sel, 1.0, 0.0).astype(BF16), h_ref[0, tok, :])
        pg = jnp.sum(jnp.where(sel, gate_ref[0, 0, :, tok], 0.0), axis=-1, keepdims=True)
        xe, gr = (px, pg) if xe is None else (xe + px, gr + pg)
    xe_ref[0, 0] = xe.astype(BF16)
    gr_ref[0, 0] = gr


def _gather(h, rank, gate, cap):
    b, lq, d = h.shape
    ne = rank.shape[1]
    rspec = pl.BlockSpec((1, 1, 1, lq), lambda i, e: (i, e, 0, 0))
    xspec = pl.BlockSpec((1, 1, cap, d), lambda i, e: (i, e, 0, 0))
    return pl.pallas_call(
        functools.partial(_gather_kernel, cap=cap),
        grid=(b, ne),
        in_specs=[pl.BlockSpec((1, lq, d), lambda i, e: (i, 0, 0)), rspec, rspec],
        out_specs=[xspec, pl.BlockSpec((1, 1, cap, 1), lambda i, e: (i, e, 0, 0))],
        out_shape=[jax.ShapeDtypeStruct((b, ne, cap, d), BF16), jax.ShapeDtypeStruct((b, ne, cap, 1), F32)],
        compiler_params=_cparams("parallel", "arbitrary"),
        name="moe_gather",
    )(h, rank.reshape(b, ne, 1, lq), gate.reshape(b, ne, 1, lq))


def _ffn_kernel(xe_ref, gr_ref, wg_ref, wu_ref, wd_ref, ye_ref, wgb_ref, wub_ref, wdb_ref):
    @pl.when(pl.program_id(1) == 0)
    def _():
        wgb_ref[...] = wg_ref[0, 0].astype(BF16)
        wub_ref[...] = wu_ref[0, 0].astype(BF16)
        wdb_ref[...] = wd_ref[0, 0].astype(BF16)

    nb, _, cap, d = xe_ref.shape
    xe = xe_ref[...].reshape(nb * cap, d)
    a = _dg(xe, wgb_ref[...])
    hid = a * _sigmoid(a) * _dg(xe, wub_ref[...])
    ye = _mm(hid, wdb_ref[...]) * gr_ref[...].reshape(nb * cap, 1)
    ye_ref[...] = ye.astype(BF16).reshape(nb, 1, cap, d)


def _expert_ffn(xe, gr, w_gate, w_up, w_down, layer):
    b, ne, cap, d = xe.shape
    ff = w_gate.shape[3]
    nb = FFN_BATCH
    xspec = pl.BlockSpec((nb, 1, cap, d), lambda e, i: (i, e, 0, 0))
    wspec = lambda shape: pl.BlockSpec((1, 1) + shape, lambda e, i: (layer, e, 0, 0))
    return pl.pallas_call(
        _ffn_kernel,
        grid=(ne, b // nb),
        in_specs=[xspec, pl.BlockSpec((nb, 1, cap, 1), lambda e, i: (i, e, 0, 0)),
                  wspec((d, ff)), wspec((d, ff)), wspec((ff, d))],
        out_specs=xspec,
        out_shape=jax.ShapeDtypeStruct((b, ne, cap, d), BF16),
        scratch_shapes=[pltpu.VMEM((d, ff), BF16), pltpu.VMEM((d, ff), BF16), pltpu.VMEM((ff, d), BF16)],
        compiler_params=_cparams("parallel", "arbitrary"),
        name="moe_ffn",
    )(xe, gr, w_gate, w_up, w_down)


def _combine_kernel(x_ref, ga_ref, rank_ref, ye_ref, o_ref, *, cap):
    g = pl.program_id(2)
    lq = x_ref.shape[1]
    ts = min(COMBINE_TOKEN_SLICE, lq)
    ye = ye_ref[0].reshape(COMBINE_EXPERTS * cap, ye_ref.shape[-1])
    for t in range(lq // ts):
        tok = slice(t * ts, (t + 1) * ts)
        p = jnp.concatenate([jnp.where(_onehot_rows(rank_ref[0, e, :, tok], cap), 1.0, 0.0).astype(BF16)
                             for e in range(COMBINE_EXPERTS)], axis=0)
        part = _dg(p, ye, TN)

        @pl.when(g == 0)
        def _():
            o_ref[0, tok, :] = part

        @pl.when(g > 0)
        def _():
            o_ref[0, tok, :] += part

    @pl.when(g == N_EXP // COMBINE_EXPERTS - 1)
    def _():
        o_ref[0] = x_ref[0] + ga_ref[0] * o_ref[0]


def _combine(x, mod, rank, ye, cap):
    b, lq, d = x.shape
    ne = rank.shape[1]
    lt = lq // 2 if lq >= 2 * COMBINE_TOKEN_SLICE else lq
    xspec = pl.BlockSpec((1, lt, d), lambda i, j, e: (i, j, 0))
    yspec = pl.BlockSpec((1, COMBINE_EXPERTS, cap, d), lambda i, j, e: (i, e, 0, 0))
    return pl.pallas_call(
        functools.partial(_combine_kernel, cap=cap),
        grid=(b, lq // lt, ne // COMBINE_EXPERTS),
        in_specs=[xspec, pl.BlockSpec((1, 1, d), lambda i, j, e: (i, 0, 0)),
                  pl.BlockSpec((1, COMBINE_EXPERTS, 1, lt), lambda i, j, e: (i, e, 0, j)), yspec],
        out_specs=xspec,
        out_shape=jax.ShapeDtypeStruct((b, lq, d), F32),
        compiler_params=_cparams("parallel", "parallel", "arbitrary"),
        name="moe_combine",
    )(x, mod[:, 5:6, :], rank.reshape(b, ne, 1, lq), ye)


def _split_w(w):
    hi = w.astype(BF16)
    return hi, (w - hi.astype(F32)).astype(BF16)


def _layout_w_in(w):
    d = w.shape[0]
    o = COLS_CONV
    q_nope = w[:, o:o + N_HEADS * MLA_NOPE]
    q_rope = w[:, o + N_HEADS * MLA_NOPE:o + N_HEADS * MLA_QK]
    ckv = w[:, o + N_HEADS * MLA_QK:o + N_HEADS * MLA_QK + KV_RANK]
    k_rope = w[:, o + N_HEADS * MLA_QK + KV_RANK:o + COLS_MLA_IN]
    pad = jnp.zeros((d, MLA_SLOT - MLA_QK), w.dtype)
    cols = []
    for h in range(N_HEADS):
        cols += [q_nope[:, h * MLA_NOPE:(h + 1) * MLA_NOPE], q_rope[:, h * MLA_ROPE:(h + 1) * MLA_ROPE], pad]
    cols += [jnp.zeros((d, MLA_NOPE), w.dtype), k_rope, pad, ckv]
    mla = jnp.concatenate(cols, axis=1)
    return jnp.concatenate([w[:, :o].astype(BF16), mla.astype(BF16), w[:, o + COLS_MLA_IN:].astype(BF16)], axis=1)


def _layout_w_uk(w_uk):
    pad = jnp.zeros((w_uk.shape[0], MLA_SLOT - MLA_NOPE), w_uk.dtype)
    cols = []
    for h in range(N_HEADS):
        cols += [w_uk[:, h * MLA_NOPE:(h + 1) * MLA_NOPE], pad]
    return _split_w(jnp.concatenate(cols, axis=1))


def _slot_gain(g):
    return jnp.concatenate([g, jnp.zeros((MLA_SLOT - MLA_QK,), g.dtype)]).reshape(1, MLA_SLOT)


def _layout_lora(w2, a2):
    z = jnp.zeros_like(w2)
    return jnp.concatenate([jnp.concatenate([w2, z], axis=2), jnp.concatenate([z, a2], axis=2)], axis=1)


def _mla_tables(lq):
    rows = lq // GRID_W
    row = jnp.repeat(jnp.arange(rows, dtype=F32), GRID_W)
    col = jnp.tile(jnp.arange(GRID_W, dtype=F32), rows)
    axis_dim = MLA_ROPE // 2
    inv_axis = ROPE_BASE ** (-jnp.arange(0, axis_dim, 2, dtype=F32) / axis_dim)
    ar = row[:, None] * inv_axis[None, :]
    ac = col[:, None] * inv_axis[None, :]
    ones = jnp.ones((lq, MLA_NOPE), F32)
    tail = jnp.ones((lq, MLA_SLOT - MLA_QK), F32)
    cos = jnp.concatenate([ones, jnp.cos(ar), jnp.cos(ar), jnp.cos(ac), jnp.cos(ac), tail], axis=1)
    sin = jnp.concatenate([0 * ones, -jnp.sin(ar), jnp.sin(ar), -jnp.sin(ac), jnp.sin(ac), 0 * tail], axis=1)
    return cos, sin


def _ret_tables(lq):
    theta = 1.0 / (RET_THETA_BASE ** jnp.linspace(0.0, 1.0, HEAD_DIM // 2, dtype=F32))
    ang = jnp.arange(lq, dtype=F32)[:, None] * theta[None, :]
    cos = jnp.tile(jnp.cos(ang), (1, 2 * N_HEADS))
    sin = jnp.tile(jnp.concatenate([-jnp.sin(ang), jnp.sin(ang)], axis=1), (1, N_HEADS))
    return cos, sin


def _ret_decay_tables(c, reverse):
    log_gamma = jnp.log1p(-jnp.exp2(-5.0 - jnp.arange(N_HEADS, dtype=F32)))
    lanes = jnp.repeat(log_gamma, HEAD_DIM)[None, :]
    j = jnp.arange(c, dtype=F32)
    if reverse:
        xi = jnp.exp((c - j)[:, None] * lanes)
        zeta = jnp.exp(j[:, None] * lanes)
        diff = j[None, :] - j[:, None]
        mask = diff > 0
    else:
        xi = jnp.exp((j + 1.0)[:, None] * lanes)
        zeta = jnp.exp((c - 1.0 - j)[:, None] * lanes)
        diff = j[:, None] - j[None, :]
        mask = diff >= 0
    dmat = jnp.where(mask[None], jnp.exp(jnp.where(mask, diff, 0.0)[None] * log_gamma[:, None, None]), 0.0)
    dec = jnp.exp(c * lanes)
    return xi, zeta, dmat, dec


def _mixers_pre(x, mod, lw, tables):
    zc, zm, zr, zt = _inproj(x, mod, lw['g_norm1'], lw['w_in'])
    y_conv = _conv_mixer(zc, lw['conv_w'], lw['g_out_conv'])
    q, k, v = _mla_prep(zm, tables['mla_cos'], tables['mla_sin'], lw['g_kv_norm'], lw['w_uk'],
                        lw['w_uv'], lw['g_q'], lw['g_k'])
    rw = _rwkv_proj(zr, lw['rwkv_mu'], lw['rwkv_k_k'], lw['rwkv_k_a'], lw['rwkv_w0'], lw['rwkv_a0'], lw['lora'])
    return dict(y_conv=y_conv, q=q, k=k, v=v, rw=rw, zt=zt)


def _rwkv_stream(rw, s_fw, s_bw, lw, emit):
    r, v, kk, gl, ks, lw0, b0, kt0, lw1, b1, kt1 = rw
    yf, yb, s_fw, s_bw = _rwkv_scan((lw0, b0, kt0), (lw1, b1, kt1), kk, v, r, s_fw, s_bw)
    y = _rwkv_out(yf, yb, r, v, ks, gl, lw['rwkv_g2'], lw['rwkv_r_k'], lw['g_ln_x']) if emit else None
    return y, s_fw, s_bw


def _ret_stream(zt, r_fw, r_bw, lw, tables, emit):
    yf, yb, r_fw, r_bw = _ret_scan(zt, tables['ret_cos'], tables['ret_sin'], r_fw, r_bw)
    y = _ret_out(yf, yb, zt, lw['g_ret_norm']) if emit else None
    return y, r_fw, r_bw


def _channel_mix(ys, x, mod, lw):
    x, h, logits = _outproj(ys, x, mod, lw['w_out'], lw['g_norm2'], lw['w_router'], lw['b_router'])
    cap = CAP_FACTOR * x.shape[1] // N_EXP
    rank, gate = _route(jnp.swapaxes(logits, 1, 2), cap)
    xe, gr = _gather(h, rank, gate, cap)
    ye = _expert_ffn(xe, gr, lw['w_gate'], lw['w_up'], lw['w_down'], lw['layer'])
    return _combine(x, mod, rank, ye, cap)


def kernel(x, c, ctx, c_ctx, w_ada, b_ada, g_norm1, g_norm2, w_in, w_out, conv_w, g_out_conv, g_kv_norm, w_uk, w_uv, g_q_norm, g_k_norm, g_out_mla, rwkv_mu, rwkv_w0, rwkv_w2, rwkv_a0, rwkv_a2, rwkv_g2, rwkv_k_k, rwkv_k_a, rwkv_r_k, g_ln_x, g_ret_norm, w_router, b_router, w_gate, w_up, w_down):
    depth = w_in.shape[0]
    bsz, lq, d = x.shape
    lc = ctx.shape[1]

    cc = jnp.zeros((16, d), F32).at[:bsz].set(c).at[bsz].set(c_ctx)
    mods = _mods(cc, w_ada, b_ada)

    cos_x, sin_x = _mla_tables(lq)
    rcos_x, rsin_x = _ret_tables(lq)
    tab_x = dict(mla_cos=cos_x, mla_sin=sin_x, ret_cos=rcos_x, ret_sin=rsin_x)
    tab_c = dict(mla_cos=jnp.ones((lc, MLA_SLOT), F32), mla_sin=jnp.zeros((lc, MLA_SLOT), F32),
                 ret_cos=jnp.ones((lc, GROUP_W), F32), ret_sin=jnp.zeros((lc, GROUP_W), F32))
    zero_state = jnp.zeros((bsz, GROUP_W, GROUP_W), F32)

    xc = ctx
    for l in range(depth):
        need_ctx = l < depth - 1
        lw = dict(
            g_norm1=g_norm1[l], g_norm2=g_norm2[l], w_in=_layout_w_in(w_in[l]), w_out=w_out[l].astype(BF16),
            conv_w=conv_w[l], g_out_conv=g_out_conv[l], g_kv_norm=g_kv_norm[l],
            w_uk=_layout_w_uk(w_uk[l]), w_uv=_split_w(w_uv[l]),
            g_q=_slot_gain(g_q_norm[l]), g_k=_slot_gain(g_k_norm[l]), g_out_mla=g_out_mla[l],
            rwkv_mu=rwkv_mu[l], rwkv_w0=rwkv_w0[l], rwkv_a0=rwkv_a0[l],
            lora=_layout_lora(rwkv_w2[l], rwkv_a2[l]), rwkv_g2=_split_w(rwkv_g2[l]),
            rwkv_k_k=rwkv_k_k[l], rwkv_k_a=rwkv_k_a[l], rwkv_r_k=rwkv_r_k[l], g_ln_x=g_ln_x[l],
            g_ret_norm=g_ret_norm[l], w_router=w_router[l], b_router=b_router[l],
            w_gate=w_gate, w_up=w_up, w_down=w_down, layer=l)
        mod_x = mods[l, :bsz].reshape(bsz, 6, d)
        mod_c = jnp.broadcast_to(mods[l, bsz].reshape(1, 6, d), (bsz, 6, d))

        pc = _mixers_pre(xc, mod_c, lw, tab_c)
        px = _mixers_pre(x, mod_x, lw, tab_x)

        k_all = jnp.concatenate([pc['k'], px['k']], axis=1)
        v_all = jnp.concatenate([pc['v'], px['v']], axis=1)
        ym_x = _attention(px['q'], k_all, v_all, lw['g_out_mla'])
        yr_c, s_fw, s_bw = _rwkv_stream(pc['rw'], zero_state, zero_state, lw, need_ctx)
        yr_x, _, _ = _rwkv_stream(px['rw'], s_fw, s_bw, lw, True)
        yt_c, r_fw, r_bw = _ret_stream(pc['zt'], zero_state, zero_state, lw, tab_c, need_ctx)
        yt_x, _, _ = _ret_stream(px['zt'], r_fw, r_bw, lw, tab_x, True)

        x = _channel_mix((px['y_conv'], ym_x, yr_x, yt_x), x, mod_x, lw)
        if need_ctx:
            ym_c = _attention(pc['q'], pc['k'], pc['v'], lw['g_out_mla'])
            xc = _channel_mix((pc['y_conv'], ym_c, yr_c, yt_c), xc, mod_c, lw)
    return x
```

```python
import functools

import jax
import jax.numpy as jnp
from jax import lax
from jax.experimental import pallas as pl
from jax.experimental.pallas import tpu as pltpu

F32 = jnp.float32
BF16 = jnp.bfloat16
HI = lax.Precision.HIGHEST

D_MODEL = 1024
N_HEADS = 4
HEAD_DIM = 64
GROUP_W = N_HEADS * HEAD_DIM
NORM_EPS = 1e-6
MLA_NOPE = 64
MLA_ROPE = 32
MLA_QK = MLA_NOPE + MLA_ROPE
MLA_SLOT = 128
KV_RANK = 128
ROPE_BASE = 10000.0
DECAY_LORA = 64
ICLR_LORA = 64
GATE_LORA = 128
RWKV_GN_EPS = 64e-5
RWKV_CHUNK = 64
RWKV_INV_BASE = 16
RWKV_BATCH = 4
RET_CHUNK = 256
RET_BATCH = 4
RET_THETA_BASE = 10000.0
N_EXP = 16
CAP_FACTOR = 2
GRID_W = 64

COLS_CONV = 3 * GROUP_W
COLS_MLA_IN = N_HEADS * MLA_QK + KV_RANK + MLA_ROPE
COLS_MLA = N_HEADS * MLA_SLOT + MLA_SLOT + KV_RANK
COLS_RWKV = 3 * GROUP_W + DECAY_LORA + ICLR_LORA + GATE_LORA
COLS_RET = 4 * GROUP_W

TOKEN_TILE = 256
MOE_TOKEN_SLICE = 1024
FFN_BATCH = 2
COMBINE_EXPERTS = 4
COMBINE_TOKEN_SLICE = 512
VMEM_LIMIT = 56 * 1024 * 1024

NN = (((1,), (0,)), ((), ()))
NT = (((1,), (1,)), ((), ()))
TN = (((0,), (0,)), ((), ()))


def _cparams(*sem):
    return pltpu.CompilerParams(dimension_semantics=sem, vmem_limit_bytes=VMEM_LIMIT)


def _dg(a, b, dims=NN, precision=None):
    return lax.dot_general(a, b, dims, precision=precision, preferred_element_type=F32)


def _split(a):
    hi = a.astype(BF16)
    return hi, (a - hi.astype(F32)).astype(BF16)


def _mm3(a, b, dims=NN):
    ah, al = a if isinstance(a, tuple) else _split(a)
    bh, bl = b if isinstance(b, tuple) else _split(b)
    return _dg(ah, bh, dims) + _dg(ah, bl, dims) + _dg(al, bh, dims)


def _mm(a, b, dims=NN):
    return _dg(a.astype(BF16), b.astype(BF16), dims)


def _row_chains(tm, body):
    sub = min(TOKEN_TILE, tm)
    _interleave(*[body(slice(t * sub, (t + 1) * sub)) for t in range(tm // sub)])


def _head_ones(n, width, scale=1.0):
    r = lax.broadcasted_iota(jnp.int32, (n, n), 0) // width
    c = lax.broadcasted_iota(jnp.int32, (n, n), 1) // width
    return jnp.where(r == c, scale, 0.0).astype(F32)


def _head_sum(a, scale=1.0):
    ah, al = _split(a)
    ones = _head_ones(a.shape[-1], HEAD_DIM).astype(BF16)
    return (_dg(ah, ones) + _dg(al, ones)) * scale


def _lane_head(shape, width):
    return lax.broadcasted_iota(jnp.int32, shape, len(shape) - 1) // width


def _sigmoid(x):
    return 1.0 / (1.0 + jnp.exp(-x))


def _interleave(*chains):
    results = [None] * len(chains)
    live = list(range(len(chains)))
    while live:
        for n in list(live):
            try:
                next(chains[n])
            except StopIteration as done:
                results[n] = done.value
                live.remove(n)
    return results


def _mods_kernel(cc_ref, w_ref, b_ref, o_ref):
    cc = cc_ref[...]
    o_ref[0] = _dg(cc * _sigmoid(cc), w_ref[0], NN, HI) + b_ref[0]


def _mods(cc, w_ada, b_ada):
    depth, d, n = w_ada.shape
    tn = 1536
    return pl.pallas_call(
        _mods_kernel,
        grid=(depth, n // tn),
        in_specs=[pl.BlockSpec(cc.shape, lambda l, j: (0, 0)),
                  pl.BlockSpec((1, d, tn), lambda l, j: (l, 0, j)),
                  pl.BlockSpec((1, 1, tn), lambda l, j: (l, 0, j))],
        out_specs=pl.BlockSpec((1, cc.shape[0], tn), lambda l, j: (l, 0, j)),
        out_shape=jax.ShapeDtypeStruct((depth, cc.shape[0], n), F32),
        compiler_params=_cparams("parallel", "parallel"),
        name="adaln_mods",
    )(cc, w_ada, b_ada.reshape(depth, 1, n))


def _inproj_kernel(x_ref, mod_ref, g_ref, w_ref, zc_ref, zm_ref, zr_ref, zt_ref):
    m = mod_ref[0]

    def rows(tok):
        x = x_ref[0, tok, :]
        xn = x * lax.rsqrt(jnp.mean(x * x, axis=-1, keepdims=True) + NORM_EPS) * g_ref[...]
        h = (xn * (1.0 + m[1:2]) + m[0:1]).astype(BF16)
        yield
        o = 0
        for ref, n in ((zc_ref, COLS_CONV), (zm_ref, COLS_MLA), (zr_ref, COLS_RWKV), (zt_ref, COLS_RET)):
            ref[0, tok, :] = _dg(h, w_ref[:, o:o + n])
            o += n
            yield

    _row_chains(x_ref.shape[1], rows)


def _inproj(x, mod, g, w):
    b, lq, d = x.shape
    tm = min(2 * TOKEN_TILE, lq)
    widths = (COLS_CONV, COLS_MLA, COLS_RWKV, COLS_RET)
    return pl.pallas_call(
        _inproj_kernel,
        grid=(b, lq // tm),
        in_specs=[pl.BlockSpec((1, tm, d), lambda i, j: (i, j, 0)),
                  pl.BlockSpec((1, 6, d), lambda i, j: (i, 0, 0)),
                  pl.BlockSpec((1, d), lambda i, j: (0, 0)),
                  pl.BlockSpec(w.shape, lambda i, j: (0, 0))],
        out_specs=[pl.BlockSpec((1, tm, n), lambda i, j: (i, j, 0)) for n in widths],
        out_shape=[jax.ShapeDtypeStruct((b, lq, n), F32) for n in widths],
        compiler_params=_cparams("parallel", "parallel"),
        name="inproj",
    )(x, mod, g.reshape(1, d), w)


def _halo_specs(tm, lq, width):
    r = tm // 8
    last = lq // 8 - 1
    return [pl.BlockSpec((1, tm, width), lambda i, j: (i, j, 0)),
            pl.BlockSpec((1, 8, width), lambda i, j: (i, jnp.maximum(j * r - 1, 0), 0)),
            pl.BlockSpec((1, 8, width), lambda i, j: (i, jnp.minimum((j + 1) * r, last), 0))]


def _shifted(t, prev_row, next_row):
    n = t.shape[0]
    rows = lax.broadcasted_iota(jnp.int32, t.shape, 0)
    t_prev = jnp.where(rows == 0, prev_row, pltpu.roll(t, 1, 0))
    t_next = jnp.where(rows == n - 1, next_row, pltpu.roll(t, n - 1, 0))
    return t_prev, t_next


def _conv_kernel(z_ref, zp_ref, zn_ref, cw_ref, g_ref, o_ref, *, nt):
    j = pl.program_id(1)
    w = GROUP_W
    z = z_ref[0]
    bgate = z[:, :w]
    u = z[:, w:2 * w] * z[:, 2 * w:]
    zp = zp_ref[0][7:8]
    zn = zn_ref[0][0:1]
    up = jnp.where(j > 0, zp[:, w:2 * w] * zp[:, 2 * w:], 0.0)
    un = jnp.where(j < nt - 1, zn[:, w:2 * w] * zn[:, 2 * w:], 0.0)
    u_prev, u_next = _shifted(u, up, un)
    cw = cw_ref[...]
    t = bgate * (cw[0:1] * u_prev + cw[1:2] * u + cw[2:3] * u_next)
    o_ref[0] = t * lax.rsqrt(jnp.mean(t * t, axis=-1, keepdims=True) + NORM_EPS) * g_ref[...]


def _conv_mixer(zc, conv_w, g_out):
    b, lq, width = zc.shape
    tm = min(4 * TOKEN_TILE, lq)
    return pl.pallas_call(
        functools.partial(_conv_kernel, nt=lq // tm),
        grid=(b, lq // tm),
        in_specs=_halo_specs(tm, lq, width) + [
            pl.BlockSpec((3, GROUP_W), lambda i, j: (0, 0)),
            pl.BlockSpec((1, GROUP_W), lambda i, j: (0, 0))],
        out_specs=pl.BlockSpec((1, tm, GROUP_W), lambda i, j: (i, j, 0)),
        out_shape=jax.ShapeDtypeStruct((b, lq, GROUP_W), F32),
        compiler_params=_cparams("parallel", "parallel"),
        name="conv_mixer",
    )(zc, zc, zc, conv_w, g_out.reshape(1, GROUP_W))


def _tile4(t):
    return jnp.concatenate([t, t, t, t], axis=-1)


def _slot_norm_rope(t, g, cos, sin):
    parts = []
    for h in range(N_HEADS):
        s = t[:, h * MLA_SLOT:(h + 1) * MLA_SLOT]
        ms = jnp.sum(s * s, axis=-1, keepdims=True) * (1.0 / MLA_QK)
        parts.append(s * lax.rsqrt(ms + NORM_EPS) * g)
    tn = jnp.concatenate(parts, axis=-1)
    n = tn.shape[-1]
    lane = lax.broadcasted_iota(jnp.int32, tn.shape, 1)
    half = MLA_ROPE // 4
    partner = jnp.where(lane % (2 * half) < half, pltpu.roll(tn, n - half, 1), pltpu.roll(tn, half, 1))
    return tn * _tile4(cos) + partner * _tile4(sin)


def _mla_prep_kernel(z_ref, cos_ref, sin_ref, gkv_ref, wukh_ref, wukl_ref, wuvh_ref, wuvl_ref, gq_ref, gk_ref,
                     q_ref, k_ref, v_ref):
    nq = N_HEADS * MLA_SLOT

    def rows(tok):
        z = z_ref[0, tok, :]
        q_in = z[:, :nq]
        k_rope = z[:, nq:nq + MLA_SLOT]
        ckv = z[:, nq + MLA_SLOT:]
        ckv = _split(ckv * lax.rsqrt(jnp.mean(ckv * ckv, axis=-1, keepdims=True) + NORM_EPS) * gkv_ref[...])
        k_in = _mm3(ckv, (wukh_ref[...], wukl_ref[...])) + _tile4(k_rope)
        v_ref[0, tok, :] = _mm3(ckv, (wuvh_ref[...], wuvl_ref[...])).astype(BF16)
        yield
        cos = cos_ref[tok, :]
        sin = sin_ref[tok, :]
        q_ref[0, tok, :] = (_slot_norm_rope(q_in, gq_ref[...], cos, sin) * (MLA_QK ** -0.5)).astype(BF16)
        yield
        k_ref[0, tok, :] = _slot_norm_rope(k_in, gk_ref[...], cos, sin).astype(BF16)

    _row_chains(z_ref.shape[1], rows)


def _mla_prep(zm, cos, sin, g_kv, wuk, wuv, g_q, g_k):
    b, lq, width = zm.shape
    tm = min(4 * TOKEN_TILE, lq)
    nq = N_HEADS * MLA_SLOT
    const = lambda shape: pl.BlockSpec(shape, lambda i, j: (0, 0))
    tok = lambda i, j: (i, j, 0)
    return pl.pallas_call(
        _mla_prep_kernel,
        grid=(b, lq // tm),
        in_specs=[pl.BlockSpec((1, tm, width), tok),
                  pl.BlockSpec((tm, MLA_SLOT), lambda i, j: (j, 0)),
                  pl.BlockSpec((tm, MLA_SLOT), lambda i, j: (j, 0)),
                  const((1, KV_RANK)), const(wuk[0].shape), const(wuk[0].shape),
                  const(wuv[0].shape), const(wuv[0].shape),
                  const((1, MLA_SLOT)), const((1, MLA_SLOT))],
        out_specs=[pl.BlockSpec((1, tm, nq), tok), pl.BlockSpec((1, tm, nq), tok), pl.BlockSpec((1, tm, GROUP_W), tok)],
        out_shape=[jax.ShapeDtypeStruct((b, lq, nq), BF16)] * 2 + [jax.ShapeDtypeStruct((b, lq, GROUP_W), BF16)],
        compiler_params=_cparams("parallel", "parallel"),
        name="mla_prep",
    )(zm, cos, sin, g_kv.reshape(1, KV_RANK), *wuk, *wuv, g_q, g_k)


def _attn_head(q, k, v):
    s = _dg(q, k, NT)
    yield
    p = jnp.exp(s - jnp.max(s, axis=-1, keepdims=True))
    inv = 1.0 / jnp.sum(p, axis=-1, keepdims=True)
    return _mm(p, v) * inv


def _attn_kernel(q_ref, k_ref, v_ref, g_ref, o_ref):
    v = v_ref[0]
    tq = q_ref.shape[1]
    head = _lane_head((tq, GROUP_W), HEAD_DIM)
    slots = [slice(h * MLA_SLOT, (h + 1) * MLA_SLOT) for h in range(N_HEADS)]
    outs = []
    for pair in (slots[:2], slots[2:]):
        outs += _interleave(*[_attn_head(q_ref[0, :, sl], k_ref[0, :, sl], v) for sl in pair])
    out = outs[0]
    for h in range(1, N_HEADS):
        out = jnp.where(head == h, outs[h], out)
    o_ref[0] = out * lax.rsqrt(jnp.mean(out * out, axis=-1, keepdims=True) + NORM_EPS) * g_ref[...]


def _attention(q, k, v, g_out):
    b, lq, nq = q.shape
    lk = k.shape[1]
    tq = min(2 * TOKEN_TILE, lq)
    return pl.pallas_call(
        _attn_kernel,
        grid=(b, lq // tq),
        in_specs=[pl.BlockSpec((1, tq, nq), lambda i, j: (i, j, 0)),
                  pl.BlockSpec((1, lk, nq), lambda i, j: (i, 0, 0)),
                  pl.BlockSpec((1, lk, GROUP_W), lambda i, j: (i, 0, 0)),
                  pl.BlockSpec((1, GROUP_W), lambda i, j: (0, 0))],
        out_specs=pl.BlockSpec((1, tq, GROUP_W), lambda i, j: (i, j, 0)),
        out_shape=jax.ShapeDtypeStruct((b, lq, GROUP_W), F32),
        compiler_params=_cparams("parallel", "arbitrary"),
        name="mla_attention",
    )(q, k, v, g_out.reshape(1, GROUP_W))


def _rwkv_proj_kernel(z_ref, zp_ref, zn_ref, mu_ref, kk_ref, ka_ref, w0_ref, a0_ref, lora_ref,
                      r_ref, v_ref, kkn_ref, gl_ref, ks_ref,
                      lw0_ref, b0_ref, kt0_ref, lw1_ref, b1_ref, kt1_ref, *, nt):
    j = pl.program_id(1)
    w = GROUP_W
    z = z_ref[0]
    zp = jnp.where(j > 0, zp_ref[0][7:8], 0.0)
    zn = jnp.where(j < nt - 1, zn_ref[0][0:1], 0.0)
    z_prev, z_next = _shifted(z, zp, zn)
    z_mix = z + mu_ref[...] * (0.5 * (z_prev + z_next) - z)

    def rows(tok):
        zs = z_mix[tok]
        r = zs[:, :w]
        k = zs[:, w:2 * w]
        lora_in = zs[:, 3 * w:3 * w + DECAY_LORA + ICLR_LORA]
        kk = k * kk_ref[...]
        kk = kk * lax.rsqrt(_head_sum(kk * kk) + 1e-12)
        lane = lax.broadcasted_iota(jnp.int32, lora_in.shape, 1)
        lora_in = jnp.where(lane < DECAY_LORA, jnp.tanh(lora_in), lora_in)
        r_ref[0, tok, :] = r
        v_ref[0, tok, :] = zs[:, 2 * w:3 * w]
        kkn_ref[0, tok, :] = kk
        gl_ref[0, tok, :] = zs[:, 3 * w + DECAY_LORA + ICLR_LORA:]
        ksum = jnp.zeros_like(k)
        for d, (lw_ref, b_ref, kt_ref) in enumerate(((lw0_ref, b0_ref, kt0_ref), (lw1_ref, b1_ref, kt1_ref))):
            lo = _mm3(lora_in, lora_ref[d])
            yield
            t = -(w0_ref[d:d + 1] + lo[:, :w])
            softplus = jnp.maximum(t, 0.0) + jnp.log1p(jnp.exp(-jnp.abs(t)))
            lw_ref[0, tok, :] = -jnp.exp(-softplus - 0.5)
            a = _sigmoid(a0_ref[d:d + 1] + lo[:, w:])
            kt = k * (1.0 + (a - 1.0) * ka_ref[...])
            b_ref[0, tok, :] = kk * a
            kt_ref[0, tok, :] = kt
            ksum = ksum + kt
        ks_ref[0, tok, :] = ksum

    _row_chains(z.shape[0], rows)


def _rwkv_proj(zr, mu, k_k, k_a, w0, a0, lora):
    b, lq, width = zr.shape
    tm = min(4 * TOKEN_TILE, lq)
    w = GROUP_W
    const2 = lambda shape: pl.BlockSpec(shape, lambda i, j: (0,) * len(shape))
    out_w = (w, w, w, GATE_LORA, w, w, w, w, w, w, w)
    return pl.pallas_call(
        functools.partial(_rwkv_proj_kernel, nt=lq // tm),
        grid=(b, lq // tm),
        in_specs=_halo_specs(tm, lq, width) + [
            const2((1, width)), const2((1, w)), const2((1, w)), const2((2, w)), const2((2, w)),
            const2(lora.shape)],
        out_specs=[pl.BlockSpec((1, tm, n), lambda i, j: (i, j, 0)) for n in out_w],
        out_shape=[jax.ShapeDtypeStruct((b, lq, n), F32) for n in out_w],
        compiler_params=_cparams("parallel", "parallel"),
        name="rwkv_proj",
    )(zr, zr, zr, mu.reshape(1, width), k_k.reshape(1, w), k_a.reshape(1, w), w0, a0, lora)


def _stack_heads(t):
    head = _lane_head(t.shape, HEAD_DIM)
    return jnp.concatenate([jnp.where(head == h, t, 0.0) for h in range(N_HEADS)], axis=0)


def _unit_lower_inverse(n_mat, p_i, q_i, size):
    base = RWKV_INV_BASE
    same = lambda w: (p_i // w) == (q_i // w)
    m = jnp.where(same(base), -n_mat, 0.0)
    inv = jnp.where(p_i == q_i, 1.0, 0.0) + m
    pw = m
    span = 1
    while 2 * span < base:
        pw = _mm(pw, pw)
        yield
        inv = inv + _mm(inv, pw)
        yield
        span *= 2
    w = base
    while w < size:
        off = jnp.where(same(2 * w) & jnp.logical_not(same(w)), n_mat, 0.0)
        invb = inv.astype(BF16)
        left = _mm(invb, off)
        yield
        inv = inv - _mm(left, invb)
        yield
        w *= 2
    return inv


def _rwkv_chunk(lw, kk, b, kt, v, r, s, reverse):
    c = RWKV_CHUNK
    n = N_HEADS * c
    t_i = lax.broadcasted_iota(jnp.int32, (c, c), 0)
    s_i = lax.broadcasted_iota(jnp.int32, (c, c), 1)
    seen = (s_i >= t_i) if reverse else (s_i <= t_i)
    g = _dg(jnp.where(seen, 1.0, 0.0).astype(F32), lw, NN, HI)
    g_tot = g[0:1] if reverse else g[c - 1:c]
    e_neg = jnp.exp(-g)
    e_tot = jnp.exp(g_tot)
    bd = _stack_heads(b * e_neg)
    ktd = _stack_heads(kt * e_neg)
    vs = _stack_heads(v).astype(BF16)
    lhs = jnp.concatenate([_stack_heads(kk * jnp.exp(g - lw)), _stack_heads(r * jnp.exp(g))], axis=0).astype(BF16)
    gram = _mm(lhs, jnp.concatenate([bd, ktd], axis=0), NT)
    from_state = _mm(lhs, s, NT)
    yield

    p_i = lax.broadcasted_iota(jnp.int32, (n, n), 0)
    q_i = lax.broadcasted_iota(jnp.int32, (n, n), 1)
    strict = (q_i > p_i) if reverse else (q_i < p_i)
    incl = (q_i >= p_i) if reverse else (q_i <= p_i)
    n_ab = jnp.where(strict, gram[:n, :n], 0.0)
    n_ak = jnp.where(strict, gram[:n, n:], 0.0)
    a_rb = jnp.where(incl, gram[n:, :n], 0.0)
    a_rk = jnp.where(incl, gram[n:, n:], 0.0)
    from_v = _mm(jnp.concatenate([n_ak, a_rk], axis=0), vs)
    t_inv = yield from _unit_lower_inverse(n_ab, p_i, q_i, c)
    u = (-_mm(t_inv, from_state[:n] + from_v[:n])).astype(BF16)
    yield
    y = from_state[n:] + _mm(a_rb, u) + from_v[n:]
    y = y[0:c] + y[c:2 * c] + y[2 * c:3 * c] + y[3 * c:]
    s_new = s * e_tot + _mm(jnp.concatenate([u, vs], axis=0),
                            jnp.concatenate([bd * e_tot, ktd * e_tot], axis=0), TN)
    return y, s_new


def _rwkv_scan_kernel(lwf_ref, kkf_ref, bf_ref, ktf_ref, vf_ref, rf_ref,
                      lwb_ref, kkb_ref, bb_ref, ktb_ref, vb_ref, rb_ref, sf0_ref, sb0_ref,
                      yf_ref, yb_ref, sfo_ref, sbo_ref, sf_ref, sb_ref, *, nc):
    i = pl.program_id(1)

    @pl.when(i == 0)
    def _():
        sf_ref[...] = sf0_ref[...]
        sb_ref[...] = sb0_ref[...]

    chains = []
    for n in range(RWKV_BATCH):
        chains.append(_rwkv_chunk(lwf_ref[n], kkf_ref[n], bf_ref[n], ktf_ref[n], vf_ref[n], rf_ref[n], sf_ref[n], False))
        chains.append(_rwkv_chunk(lwb_ref[n], kkb_ref[n], bb_ref[n], ktb_ref[n], vb_ref[n], rb_ref[n], sb_ref[n], True))
    results = _interleave(*chains)
    for n in range(RWKV_BATCH):
        (yf, sf), (yb, sb) = results[2 * n], results[2 * n + 1]
        yf_ref[n] = yf
        yb_ref[n] = yb
        sf_ref[n] = sf
        sb_ref[n] = sb

    @pl.when(i == nc - 1)
    def _():
        sfo_ref[...] = sf_ref[...]
        sbo_ref[...] = sb_ref[...]


def _rwkv_scan(fw, bw, kk, v, r, s_fw, s_bw):
    b, lq, w = kk.shape
    c = RWKV_CHUNK
    nc = lq // c
    nb = RWKV_BATCH
    fspec = pl.BlockSpec((nb, c, w), lambda i, j: (i, j, 0))
    bspec = pl.BlockSpec((nb, c, w), lambda i, j: (i, nc - 1 - j, 0))
    sspec = pl.BlockSpec((nb, w, w), lambda i, j: (i, 0, 0))
    yshape = jax.ShapeDtypeStruct((b, lq, w), F32)
    sshape = jax.ShapeDtypeStruct((b, w, w), F32)
    return pl.pallas_call(
        functools.partial(_rwkv_scan_kernel, nc=nc),
        grid=(b // nb, nc),
        in_specs=[fspec] * 6 + [bspec] * 6 + [sspec, sspec],
        out_specs=[fspec, bspec, sspec, sspec],
        out_shape=[yshape, yshape, sshape, sshape],
        scratch_shapes=[pltpu.VMEM((nb, w, w), F32), pltpu.VMEM((nb, w, w), F32)],
        compiler_params=_cparams("parallel", "arbitrary"),
        name="rwkv_scan",
    )(fw[0], kk, fw[1], fw[2], v, r, bw[0], kk, bw[1], bw[2], v, r, s_fw, s_bw)


def _rwkv_out_kernel(yf_ref, yb_ref, r_ref, v_ref, ks_ref, gl_ref, g2h_ref, g2l_ref, rk_ref, gln_ref, o_ref):
    def rows(tok):
        y = yf_ref[0, tok, :] + yb_ref[0, tok, :]
        bonus = _head_sum(r_ref[0, tok, :] * ks_ref[0, tok, :] * rk_ref[...])
        gate = _mm3(_sigmoid(gl_ref[0, tok, :]), (g2h_ref[...], g2l_ref[...]))
        dlt = y - _head_sum(y, 1.0 / HEAD_DIM)
        yield
        var = _head_sum(dlt * dlt, 1.0 / HEAD_DIM)
        yield
        yn = dlt * lax.rsqrt(var + RWKV_GN_EPS) * gln_ref[...] + bonus * v_ref[0, tok, :]
        o_ref[0, tok, :] = yn * gate

    _row_chains(yf_ref.shape[1], rows)


def _rwkv_out(yf, yb, r, v, ks, gl, g2, r_k, g_ln):
    b, lq, w = yf.shape
    tm = min(4 * TOKEN_TILE, lq)
    tspec = pl.BlockSpec((1, tm, w), lambda i, j: (i, j, 0))
    const = lambda shape: pl.BlockSpec(shape, lambda i, j: (0, 0))
    return pl.pallas_call(
        _rwkv_out_kernel,
        grid=(b, lq // tm),
        in_specs=[tspec] * 5 + [pl.BlockSpec((1, tm, GATE_LORA), lambda i, j: (i, j, 0)),
                                const(g2[0].shape), const(g2[0].shape), const((1, w)), const((1, w))],
        out_specs=tspec,
        out_shape=jax.ShapeDtypeStruct((b, lq, w), F32),
        compiler_params=_cparams("parallel", "parallel"),
        name="rwkv_out",
    )(yf, yb, r, v, ks, gl, *g2, r_k.reshape(1, w), g_ln.reshape(1, w))


def _half_rotate(t, cos, sin):
    n = t.shape[-1]
    lane = lax.broadcasted_iota(jnp.int32, t.shape, 1)
    half = HEAD_DIM // 2
    partner = jnp.where(lane % HEAD_DIM < half, pltpu.roll(t, n - half, 1), pltpu.roll(t, half, 1))
    return t * cos + partner * sin


def _ret_chunk(z, cos, sin, xi, zeta, dmat_ref, dec, st):
    w = GROUP_W
    q = _half_rotate(z[:, :w] * (HEAD_DIM ** -0.5), cos, sin)
    k = _half_rotate(z[:, w:2 * w], cos, sin)
    vs = z[:, 2 * w:3 * w].astype(BF16)
    ks = k.astype(BF16)
    head = _lane_head(q.shape, HEAD_DIM)
    y = _mm(q * xi, st)
    upd = _mm(k * zeta, vs, TN)
    scores = [_mm(jnp.where(head == h, q, 0.0), ks, NT) * dmat_ref[h] for h in range(N_HEADS)]
    yield
    for h in range(N_HEADS):
        y = y + jnp.where(head == h, _mm(scores[h], vs), 0.0)
    return y, st * dec + upd * _head_ones(w, HEAD_DIM)


def _ret_scan_kernel(zf_ref, cosf_ref, sinf_ref, zb_ref, cosb_ref, sinb_ref,
                     xif_ref, zetaf_ref, dmatf_ref, xib_ref, zetab_ref, dmatb_ref, dec_ref, rf0_ref, rb0_ref,
                     yf_ref, yb_ref, rfo_ref, rbo_ref, stf_ref, stb_ref, *, nc):
    i = pl.program_id(1)

    @pl.when(i == 0)
    def _():
        stf_ref[...] = rf0_ref[...]
        stb_ref[...] = rb0_ref[...]

    dec = dec_ref[...]
    chains = []
    for n in range(RET_BATCH):
        chains.append(_ret_chunk(zf_ref[n], cosf_ref[...], sinf_ref[...], xif_ref[...], zetaf_ref[...], dmatf_ref,
                                 dec, stf_ref[n]))
        chains.append(_ret_chunk(zb_ref[n], cosb_ref[...], sinb_ref[...], xib_ref[...], zetab_ref[...], dmatb_ref,
                                 dec, stb_ref[n]))
    results = _interleave(*chains)
    for n in range(RET_BATCH):
        (yf, stf), (yb, stb) = results[2 * n], results[2 * n + 1]
        yf_ref[n] = yf
        yb_ref[n] = yb
        stf_ref[n] = stf
        stb_ref[n] = stb

    @pl.when(i == nc - 1)
    def _():
        rfo_ref[...] = stf_ref[...]
        rbo_ref[...] = stb_ref[...]


def _ret_scan(zt, cos, sin, r_fw, r_bw):
    b, lq, width = zt.shape
    c = min(RET_CHUNK, lq)
    nc = lq // c
    w = GROUP_W
    xi_f, zeta_f, dmat_f, dec = _ret_decay_tables(c, False)
    xi_b, zeta_b, dmat_b, _ = _ret_decay_tables(c, True)
    fw3 = lambda i, j: (i, j, 0)
    bw3 = lambda i, j: (i, nc - 1 - j, 0)
    fw2 = lambda i, j: (j, 0)
    bw2 = lambda i, j: (nc - 1 - j, 0)
    nb = RET_BATCH
    const = lambda shape: pl.BlockSpec(shape, lambda i, j: (0,) * len(shape))
    sspec = pl.BlockSpec((nb, w, w), lambda i, j: (i, 0, 0))
    tables = [const((c, w)), const((c, w)), const((N_HEADS, c, c))]
    yshape = jax.ShapeDtypeStruct((b, lq, w), F32)
    sshape = jax.ShapeDtypeStruct((b, w, w), F32)
    return pl.pallas_call(
        functools.partial(_ret_scan_kernel, nc=nc),
        grid=(b // nb, nc),
        in_specs=[pl.BlockSpec((nb, c, width), fw3), pl.BlockSpec((c, w), fw2), pl.BlockSpec((c, w), fw2),
                  pl.BlockSpec((nb, c, width), bw3), pl.BlockSpec((c, w), bw2), pl.BlockSpec((c, w), bw2)]
        + tables + tables + [const((1, w)), sspec, sspec],
        out_specs=[pl.BlockSpec((nb, c, w), fw3), pl.BlockSpec((nb, c, w), bw3), sspec, sspec],
        out_shape=[yshape, yshape, sshape, sshape],
        scratch_shapes=[pltpu.VMEM((nb, w, w), F32), pltpu.VMEM((nb, w, w), F32)],
        compiler_params=_cparams("parallel", "arbitrary"),
        name="ret_scan",
    )(zt, cos, sin, zt, cos, sin, xi_f, zeta_f, dmat_f, xi_b, zeta_b, dmat_b, dec, r_fw, r_bw)


def _ret_out_kernel(yf_ref, yb_ref, z_ref, g_ref, o_ref):
    def rows(tok):
        y = yf_ref[0, tok, :] + yb_ref[0, tok, :]
        ms = _head_sum(y * y, 1.0 / HEAD_DIM)
        yield
        gate = z_ref[0, tok, :]
        o_ref[0, tok, :] = gate * _sigmoid(gate) * (y * lax.rsqrt(ms + NORM_EPS) * g_ref[...])

    _row_chains(yf_ref.shape[1], rows)


def _ret_out(yf, yb, zt, g_norm):
    b, lq, w = yf.shape
    tm = min(4 * TOKEN_TILE, lq)
    tspec = pl.BlockSpec((1, tm, w), lambda i, j: (i, j, 0))
    return pl.pallas_call(
        _ret_out_kernel,
        grid=(b, lq // tm),
        in_specs=[tspec, tspec, pl.BlockSpec((1, tm, w), lambda i, j: (i, j, 3)),
                  pl.BlockSpec((1, w), lambda i, j: (0, 0))],
        out_specs=tspec,
        out_shape=jax.ShapeDtypeStruct((b, lq, w), F32),
        compiler_params=_cparams("parallel", "parallel"),
        name="ret_out",
    )(yf, yb, zt, g_norm.reshape(1, w))


def _outproj_rows(tok, y_refs, x_ref, m, w_ref, g_ref, wr_ref, br_ref, xo_ref, h_ref, lg_ref):
    w = GROUP_W
    acc = None
    for n, ref in enumerate(y_refs):
        part = _mm(ref[0, tok, :], w_ref[n * w:(n + 1) * w, :])
        acc = part if acc is None else acc + part
    yield
    x = x_ref[0, tok, :] + m[2:3] * acc
    xo_ref[0, tok, :] = x
    xn = x * lax.rsqrt(jnp.mean(x * x, axis=-1, keepdims=True) + NORM_EPS) * g_ref[...]
    h = xn * (1.0 + m[4:5]) + m[3:4]
    h_ref[0, tok, :] = h.astype(BF16)
    yield
    lg_ref[0, tok, :] = _mm3(h, wr_ref[...]) + br_ref[...]


def _outproj_kernel(yc_ref, ym_ref, yr_ref, yt_ref, x_ref, mod_ref, w_ref, g_ref, wr_ref, br_ref,
                    xo_ref, h_ref, lg_ref):
    tm = x_ref.shape[1]
    sub = min(TOKEN_TILE, tm)
    m = mod_ref[0]
    _interleave(*[_outproj_rows(slice(t * sub, (t + 1) * sub), (yc_ref, ym_ref, yr_ref, yt_ref), x_ref, m,
                                w_ref, g_ref, wr_ref, br_ref, xo_ref, h_ref, lg_ref) for t in range(tm // sub)])


def _outproj(ys, x, mod, w_out, g2, w_r, b_r):
    b, lq, d = x.shape
    tm = min(2 * TOKEN_TILE, lq)
    yspec = pl.BlockSpec((1, tm, GROUP_W), lambda i, j: (i, j, 0))
    xspec = pl.BlockSpec((1, tm, d), lambda i, j: (i, j, 0))
    const = lambda shape: pl.BlockSpec(shape, lambda i, j: (0, 0))
    return pl.pallas_call(
        _outproj_kernel,
        grid=(b, lq // tm),
        in_specs=[yspec] * 4 + [xspec, pl.BlockSpec((1, 6, d), lambda i, j: (i, 0, 0)),
                                const(w_out.shape), const((1, d)),
                                const(w_r.shape), const((1, N_EXP))],
        out_specs=[xspec, xspec, pl.BlockSpec((1, tm, N_EXP), lambda i, j: (i, j, 0))],
        out_shape=[jax.ShapeDtypeStruct((b, lq, d), F32), jax.ShapeDtypeStruct((b, lq, d), BF16),
                   jax.ShapeDtypeStruct((b, lq, N_EXP), F32)],
        compiler_params=_cparams("parallel", "parallel"),
        name="outproj",
    )(*ys, x, mod, w_out, g2.reshape(1, d), w_r, b_r.reshape(1, N_EXP))


def _lane_cumsum(m):
    n = m.shape[-1]
    blk = 128
    r_i = lax.broadcasted_iota(jnp.int32, (blk, blk), 0)
    c_i = lax.broadcasted_iota(jnp.int32, (blk, blk), 1)
    tri = jnp.where(r_i <= c_i, 1.0, 0.0).astype(BF16)
    run = jnp.zeros((m.shape[0], 1), F32)
    parts = []
    for j in range(n // blk):
        cs = _dg(m[:, j * blk:(j + 1) * blk].astype(BF16), tri) + run
        parts.append(cs)
        run = cs[:, blk - 1:blk]
    return jnp.concatenate(parts, axis=-1)


def _route_kernel(lg_ref, rank_ref, gate_ref, *, cap):
    lg = lg_ref[0]
    e = jnp.exp(lg - jnp.max(lg, axis=0, keepdims=True))
    aff = e / jnp.sum(e, axis=0, keepdims=True)
    bits = pltpu.bitcast(aff, jnp.int32)
    thr = jnp.zeros((aff.shape[0], 1), jnp.int32)
    for bit in range(30, -1, -1):
        cand = thr | (1 << bit)
        cnt = jnp.sum(jnp.where(bits >= cand, 1.0, 0.0), axis=-1, keepdims=True)
        thr = jnp.where(cnt >= cap, cand, thr)
    above = bits > thr
    tied = bits == thr
    n_above = jnp.sum(jnp.where(above, 1.0, 0.0), axis=-1, keepdims=True)
    tied_f = jnp.where(tied, 1.0, 0.0)
    tied_rank = _lane_cumsum(tied_f) - tied_f
    sel = above | (tied & (tied_rank < cap - n_above))
    sel_f = jnp.where(sel, 1.0, 0.0)
    rank = _lane_cumsum(sel_f) - sel_f
    rank_ref[0] = jnp.where(sel, rank, -1.0).astype(jnp.int32)
    gate_ref[0] = jnp.where(sel, aff, 0.0)


def _route(logits_t, cap):
    b, ne, lq = logits_t.shape
    spec = pl.BlockSpec((1, ne, lq), lambda i: (i, 0, 0))
    return pl.pallas_call(
        functools.partial(_route_kernel, cap=cap),
        grid=(b,),
        in_specs=[spec],
        out_specs=[spec, spec],
        out_shape=[jax.ShapeDtypeStruct((b, ne, lq), jnp.int32), jax.ShapeDtypeStruct((b, ne, lq), F32)],
        compiler_params=_cparams("parallel"),
        name="moe_route",
    )(logits_t)


def _onehot_rows(rank_row, cap):
    r_i = lax.broadcasted_iota(jnp.int32, (cap, rank_row.shape[-1]), 0)
    return r_i == rank_row


def _gather_kernel(h_ref, rank_ref, gate_ref, xe_ref, gr_ref, *, cap):
    lq = h_ref.shape[1]
    ts = min(MOE_TOKEN_SLICE, lq)
    xe = gr = None
    for t in range(lq // ts):
        tok = slice(t * ts, (t + 1) * ts)
        sel = _onehot_rows(rank_ref[0, 0, :, tok], cap)
        px = _dg(jnp.where(sel, 1.0, 0.0).astype(BF16), h_ref[0, tok, :])
        pg = jnp.sum(jnp.where(sel, gate_ref[0, 0, :, tok], 0.0), axis=-1, keepdims=True)
        xe, gr = (px, pg) if xe is None else (xe + px, gr + pg)
    xe_ref[0, 0] = xe.astype(BF16)
    gr_ref[0, 0] = gr


def _gather(h, rank, gate, cap):
    b, lq, d = h.shape
    ne = rank.shape[1]
    rspec = pl.BlockSpec((1, 1, 1, lq), lambda i, e: (i, e, 0, 0))
    xspec = pl.BlockSpec((1, 1, cap, d), lambda i, e: (i, e, 0, 0))
    return pl.pallas_call(
        functools.partial(_gather_kernel, cap=cap),
        grid=(b, ne),
        in_specs=[pl.BlockSpec((1, lq, d), lambda i, e: (i, 0, 0)), rspec, rspec],
        out_specs=[xspec, pl.BlockSpec((1, 1, cap, 1), lambda i, e: (i, e, 0, 0))],
        out_shape=[jax.ShapeDtypeStruct((b, ne, cap, d), BF16), jax.ShapeDtypeStruct((b, ne, cap, 1), F32)],
        compiler_params=_cparams("parallel", "arbitrary"),
        name="moe_gather",
    )(h, rank.reshape(b, ne, 1, lq), gate.reshape(b, ne, 1, lq))


def _ffn_kernel(xe_ref, gr_ref, wg_ref, wu_ref, wd_ref, ye_ref, wgb_ref, wub_ref, wdb_ref):
    @pl.when(pl.program_id(1) == 0)
    def _():
        wgb_ref[...] = wg_ref[0, 0].astype(BF16)
        wub_ref[...] = wu_ref[0, 0].astype(BF16)
        wdb_ref[...] = wd_ref[0, 0].astype(BF16)

    nb, _, cap, d = xe_ref.shape
    xe = xe_ref[...].reshape(nb * cap, d)
    a = _dg(xe, wgb_ref[...])
    hid = a * _sigmoid(a) * _dg(xe, wub_ref[...])
    ye = _mm(hid, wdb_ref[...]) * gr_ref[...].reshape(nb * cap, 1)
    ye_ref[...] = ye.astype(BF16).reshape(nb, 1, cap, d)


def _expert_ffn(xe, gr, w_gate, w_up, w_down, layer):
    b, ne, cap, d = xe.shape
    ff = w_gate.shape[3]
    nb = FFN_BATCH
    xspec = pl.BlockSpec((nb, 1, cap, d), lambda e, i: (i, e, 0, 0))
    wspec = lambda shape: pl.BlockSpec((1, 1) + shape, lambda e, i: (layer, e, 0, 0))
    return pl.pallas_call(
        _ffn_kernel,
        grid=(ne, b // nb),
        in_specs=[xspec, pl.BlockSpec((nb, 1, cap, 1), lambda e, i: (i, e, 0, 0)),
                  wspec((d, ff)), wspec((d, ff)), wspec((ff, d))],
        out_specs=xspec,
        out_shape=jax.ShapeDtypeStruct((b, ne, cap, d), BF16),
        scratch_shapes=[pltpu.VMEM((d, ff), BF16), pltpu.VMEM((d, ff), BF16), pltpu.VMEM((ff, d), BF16)],
        compiler_params=_cparams("parallel", "arbitrary"),
        name="moe_ffn",
    )(xe, gr, w_gate, w_up, w_down)


def _combine_kernel(x_ref, ga_ref, rank_ref, ye_ref, o_ref, *, cap):
    g = pl.program_id(2)
    lq = x_ref.shape[1]
    ts = min(COMBINE_TOKEN_SLICE, lq)
    ye = ye_ref[0].reshape(COMBINE_EXPERTS * cap, ye_ref.shape[-1])
    for t in range(lq // ts):
        tok = slice(t * ts, (t + 1) * ts)
        p = jnp.concatenate([jnp.where(_onehot_rows(rank_ref[0, e, :, tok], cap), 1.0, 0.0).astype(BF16)
                             for e in range(COMBINE_EXPERTS)], axis=0)
        part = _dg(p, ye, TN)

        @pl.when(g == 0)
        def _():
            o_ref[0, tok, :] = part

        @pl.when(g > 0)
        def _():
            o_ref[0, tok, :] += part

    @pl.when(g == N_EXP // COMBINE_EXPERTS - 1)
    def _():
        o_ref[0] = x_ref[0] + ga_ref[0] * o_ref[0]


def _combine(x, mod, rank, ye, cap):
    b, lq, d = x.shape
    ne = rank.shape[1]
    lt = lq // 2 if lq >= 2 * COMBINE_TOKEN_SLICE else lq
    xspec = pl.BlockSpec((1, lt, d), lambda i, j, e: (i, j, 0))
    yspec = pl.BlockSpec((1, COMBINE_EXPERTS, cap, d), lambda i, j, e: (i, e, 0, 0))
    return pl.pallas_call(
        functools.partial(_combine_kernel, cap=cap),
        grid=(b, lq // lt, ne // COMBINE_EXPERTS),
        in_specs=[xspec, pl.BlockSpec((1, 1, d), lambda i, j, e: (i, 0, 0)),
                  pl.BlockSpec((1, COMBINE_EXPERTS, 1, lt), lambda i, j, e: (i, e, 0, j)), yspec],
        out_specs=xspec,
        out_shape=jax.ShapeDtypeStruct((b, lq, d), F32),
        compiler_params=_cparams("parallel", "parallel", "arbitrary"),
        name="moe_combine",
    )(x, mod[:, 5:6, :], rank.reshape(b, ne, 1, lq), ye)


def _split_w(w):
    hi = w.astype(BF16)
    return hi, (w - hi.astype(F32)).astype(BF16)


def _layout_w_in(w):
    d = w.shape[0]
    o = COLS_CONV
    q_nope = w[:, o:o + N_HEADS * MLA_NOPE]
    q_rope = w[:, o + N_HEADS * MLA_NOPE:o + N_HEADS * MLA_QK]
    ckv = w[:, o + N_HEADS * MLA_QK:o + N_HEADS * MLA_QK + KV_RANK]
    k_rope = w[:, o + N_HEADS * MLA_QK + KV_RANK:o + COLS_MLA_IN]
    pad = jnp.zeros((d, MLA_SLOT - MLA_QK), w.dtype)
    cols = []
    for h in range(N_HEADS):
        cols += [q_nope[:, h * MLA_NOPE:(h + 1) * MLA_NOPE], q_rope[:, h * MLA_ROPE:(h + 1) * MLA_ROPE], pad]
    cols += [jnp.zeros((d, MLA_NOPE), w.dtype), k_rope, pad, ckv]
    mla = jnp.concatenate(cols, axis=1)
    return jnp.concatenate([w[:, :o].astype(BF16), mla.astype(BF16), w[:, o + COLS_MLA_IN:].astype(BF16)], axis=1)


def _layout_w_uk(w_uk):
    pad = jnp.zeros((w_uk.shape[0], MLA_SLOT - MLA_NOPE), w_uk.dtype)
    cols = []
    for h in range(N_HEADS):
        cols += [w_uk[:, h * MLA_NOPE:(h + 1) * MLA_NOPE], pad]
    return _split_w(jnp.concatenate(cols, axis=1))


def _slot_gain(g):
    return jnp.concatenate([g, jnp.zeros((MLA_SLOT - MLA_QK,), g.dtype)]).reshape(1, MLA_SLOT)


def _layout_lora(w2, a2):
    z = jnp.zeros_like(w2)
    return jnp.concatenate([jnp.concatenate([w2, z], axis=2), jnp.concatenate([z, a2], axis=2)], axis=1)


def _mla_tables(lq):
    rows = lq // GRID_W
    row = jnp.repeat(jnp.arange(rows, dtype=F32), GRID_W)
    col = jnp.tile(jnp.arange(GRID_W, dtype=F32), rows)
    axis_dim = MLA_ROPE // 2
    inv_axis = ROPE_BASE ** (-jnp.arange(0, axis_dim, 2, dtype=F32) / axis_dim)
    ar = row[:, None] * inv_axis[None, :]
    ac = col[:, None] * inv_axis[None, :]
    ones = jnp.ones((lq, MLA_NOPE), F32)
    tail = jnp.ones((lq, MLA_SLOT - MLA_QK), F32)
    cos = jnp.concatenate([ones, jnp.cos(ar), jnp.cos(ar), jnp.cos(ac), jnp.cos(ac), tail], axis=1)
    sin = jnp.concatenate([0 * ones, -jnp.sin(ar), jnp.sin(ar), -jnp.sin(ac), jnp.sin(ac), 0 * tail], axis=1)
    return cos, sin


def _ret_tables(lq):
    theta = 1.0 / (RET_THETA_BASE ** jnp.linspace(0.0, 1.0, HEAD_DIM // 2, dtype=F32))
    ang = jnp.arange(lq, dtype=F32)[:, None] * theta[None, :]
    cos = jnp.tile(jnp.cos(ang), (1, 2 * N_HEADS))
    sin = jnp.tile(jnp.concatenate([-jnp.sin(ang), jnp.sin(ang)], axis=1), (1, N_HEADS))
    return cos, sin


def _ret_decay_tables(c, reverse):
    log_gamma = jnp.log1p(-jnp.exp2(-5.0 - jnp.arange(N_HEADS, dtype=F32)))
    lanes = jnp.repeat(log_gamma, HEAD_DIM)[None, :]
    j = jnp.arange(c, dtype=F32)
    if reverse:
        xi = jnp.exp((c - j)[:, None] * lanes)
        zeta = jnp.exp(j[:, None] * lanes)
        diff = j[None, :] - j[:, None]
        mask = diff > 0
    else:
        xi = jnp.exp((j + 1.0)[:, None] * lanes)
        zeta = jnp.exp((c - 1.0 - j)[:, None] * lanes)
        diff = j[:, None] - j[None, :]
        mask = diff >= 0
    dmat = jnp.where(mask[None], jnp.exp(jnp.where(mask, diff, 0.0)[None] * log_gamma[:, None, None]), 0.0)
    dec = jnp.exp(c * lanes)
    return xi, zeta, dmat, dec


def _mixers_pre(x, mod, lw, tables):
    zc, zm, zr, zt = _inproj(x, mod, lw['g_norm1'], lw['w_in'])
    y_conv = _conv_mixer(zc, lw['conv_w'], lw['g_out_conv'])
    q, k, v = _mla_prep(zm, tables['mla_cos'], tables['mla_sin'], lw['g_kv_norm'], lw['w_uk'],
                        lw['w_uv'], lw['g_q'], lw['g_k'])
    rw = _rwkv_proj(zr, lw['rwkv_mu'], lw['rwkv_k_k'], lw['rwkv_k_a'], lw['rwkv_w0'], lw['rwkv_a0'], lw['lora'])
    return dict(y_conv=y_conv, q=q, k=k, v=v, rw=rw, zt=zt)


def _rwkv_stream(rw, s_fw, s_bw, lw, emit):
    r, v, kk, gl, ks, lw0, b0, kt0, lw1, b1, kt1 = rw
    yf, yb, s_fw, s_bw = _rwkv_scan((lw0, b0, kt0), (lw1, b1, kt1), kk, v, r, s_fw, s_bw)
    y = _rwkv_out(yf, yb, r, v, ks, gl, lw['rwkv_g2'], lw['rwkv_r_k'], lw['g_ln_x']) if emit else None
    return y, s_fw, s_bw


def _ret_stream(zt, r_fw, r_bw, lw, tables, emit):
    yf, yb, r_fw, r_bw = _ret_scan(zt, tables['ret_cos'], tables['ret_sin'], r_fw, r_bw)
    y = _ret_out(yf, yb, zt, lw['g_ret_norm']) if emit else None
    return y, r_fw, r_bw


def _channel_mix(ys, x, mod, lw):
    x, h, logits = _outproj(ys, x, mod, lw['w_out'], lw['g_norm2'], lw['w_router'], lw['b_router'])
    cap = CAP_FACTOR * x.shape[1] // N_EXP
    rank, gate = _route(jnp.swapaxes(logits, 1, 2), cap)
    xe, gr = _gather(h, rank, gate, cap)
    ye = _expert_ffn(xe, gr, lw['w_gate'], lw['w_up'], lw['w_down'], lw['layer'])
    return _combine(x, mod, rank, ye, cap)


def kernel(x, c, ctx, c_ctx, w_ada, b_ada, g_norm1, g_norm2, w_in, w_out, conv_w, g_out_conv, g_kv_norm, w_uk, w_uv, g_q_norm, g_k_norm, g_out_mla, rwkv_mu, rwkv_w0, rwkv_w2, rwkv_a0, rwkv_a2, rwkv_g2, rwkv_k_k, rwkv_k_a, rwkv_r_k, g_ln_x, g_ret_norm, w_router, b_router, w_gate, w_up, w_down):
    depth = w_in.shape[0]
    bsz, lq, d = x.shape
    lc = ctx.shape[1]

    cc = jnp.zeros((16, d), F32).at[:bsz].set(c).at[bsz].set(c_ctx)
    mods = _mods(cc, w_ada, b_ada)

    cos_x, sin_x = _mla_tables(lq)
    rcos_x, rsin_x = _ret_tables(lq)
    tab_x = dict(mla_cos=cos_x, mla_sin=sin_x, ret_cos=rcos_x, ret_sin=rsin_x)
    tab_c = dict(mla_cos=jnp.ones((lc, MLA_SLOT), F32), mla_sin=jnp.zeros((lc, MLA_SLOT), F32),
                 ret_cos=jnp.ones((lc, GROUP_W), F32), ret_sin=jnp.zeros((lc, GROUP_W), F32))
    zero_state = jnp.zeros((bsz, GROUP_W, GROUP_W), F32)

    xc = ctx
    for l in range(depth):
        need_ctx = l < depth - 1
        lw = dict(
            g_norm1=g_norm1[l], g_norm2=g_norm2[l], w_in=_layout_w_in(w_in[l]), w_out=w_out[l].astype(BF16),
            conv_w=conv_w[l], g_out_conv=g_out_conv[l], g_kv_norm=g_kv_norm[l],
            w_uk=_layout_w_uk(w_uk[l]), w_uv=_split_w(w_uv[l]),
            g_q=_slot_gain(g_q_norm[l]), g_k=_slot_gain(g_k_norm[l]), g_out_mla=g_out_mla[l],
            rwkv_mu=rwkv_mu[l], rwkv_w0=rwkv_w0[l], rwkv_a0=rwkv_a0[l],
            lora=_layout_lora(rwkv_w2[l], rwkv_a2[l]), rwkv_g2=_split_w(rwkv_g2[l]),
            rwkv_k_k=rwkv_k_k[l], rwkv_k_a=rwkv_k_a[l], rwkv_r_k=rwkv_r_k[l], g_ln_x=g_ln_x[l],
            g_ret_norm=g_ret_norm[l], w_router=w_router[l], b_router=b_router[l],
            w_gate=w_gate, w_up=w_up, w_down=w_down, layer=l)
        mod_x = mods[l, :bsz].reshape(bsz, 6, d)
        mod_c = jnp.broadcast_to(mods[l, bsz].reshape(1, 6, d), (bsz, 6, d))

        pc = _mixers_pre(xc, mod_c, lw, tab_c)
        px = _mixers_pre(x, mod_x, lw, tab_x)

        k_all = jnp.concatenate([pc['k'], px['k']], axis=1)
        v_all = jnp.concatenate([pc['v'], px['v']], axis=1)
        ym_x = _attention(px['q'], k_all, v_all, lw['g_out_mla'])
        yr_c, s_fw, s_bw = _rwkv_stream(pc['rw'], zero_state, zero_state, lw, need_ctx)
        yr_x, _, _ = _rwkv_stream(px['rw'], s_fw, s_bw, lw, True)
        yt_c, r_fw, r_bw = _ret_stream(pc['zt'], zero_state, zero_state, lw, tab_c, need_ctx)
        yt_x, _, _ = _ret_stream(px['zt'], r_fw, r_bw, lw, tab_x, True)

        x = _channel_mix((px['y_conv'], ym_x, yr_x, yt_x), x, mod_x, lw)
        if need_ctx:
            ym_c = _attention(pc['q'], pc['k'], pc['v'], lw['g_out_mla'])
            xc = _channel_mix((pc['y_conv'], ym_c, yr_c, yt_c), xc, mod_c, lw)
    return x
```

```python
import functools

import jax
import jax.numpy as jnp
from jax import lax
from jax.experimental import pallas as pl
from jax.experimental.pallas import tpu as pltpu

F32 = jnp.float32
BF16 = jnp.bfloat16
HI = lax.Precision.HIGHEST

D_MODEL = 1024
N_HEADS = 4
HEAD_DIM = 64
GROUP_W = N_HEADS * HEAD_DIM
NORM_EPS = 1e-6
MLA_NOPE = 64
MLA_ROPE = 32
MLA_QK = MLA_NOPE + MLA_ROPE
MLA_SLOT = 128
KV_RANK = 128
ROPE_BASE = 10000.0
DECAY_LORA = 64
ICLR_LORA = 64
GATE_LORA = 128
RWKV_GN_EPS = 64e-5
RWKV_CHUNK = 64
RWKV_INV_BASE = 16
RWKV_BATCH = 4
RET_CHUNK = 256
RET_BATCH = 4
RET_THETA_BASE = 10000.0
N_EXP = 16
CAP_FACTOR = 2
GRID_W = 64

COLS_CONV = 3 * GROUP_W
COLS_MLA_IN = N_HEADS * MLA_QK + KV_RANK + MLA_ROPE
COLS_MLA = N_HEADS * MLA_SLOT + MLA_SLOT + KV_RANK
COLS_RWKV = 3 * GROUP_W + DECAY_LORA + ICLR_LORA + GATE_LORA
COLS_RET = 4 * GROUP_W

TOKEN_TILE = 256
MOE_TOKEN_SLICE = 1024
FFN_BATCH = 2
COMBINE_EXPERTS = 4
COMBINE_TOKEN_SLICE = 512
VMEM_LIMIT = 56 * 1024 * 1024

NN = (((1,), (0,)), ((), ()))
NT = (((1,), (1,)), ((), ()))
TN = (((0,), (0,)), ((), ()))


def _cparams(*sem):
    return pltpu.CompilerParams(dimension_semantics=sem, vmem_limit_bytes=VMEM_LIMIT)


def _dg(a, b, dims=NN, precision=None):
    return lax.dot_general(a, b, dims, precision=precision, preferred_element_type=F32)


def _split(a):
    hi = a.astype(BF16)
    return hi, (a - hi.astype(F32)).astype(BF16)


def _mm3(a, b, dims=NN):
    ah, al = a if isinstance(a, tuple) else _split(a)
    bh, bl = b if isinstance(b, tuple) else _split(b)
    return _dg(ah, bh, dims) + _dg(ah, bl, dims) + _dg(al, bh, dims)


def _mm(a, b, dims=NN):
    return _dg(a.astype(BF16), b.astype(BF16), dims)


def _row_chains(tm, body):
    sub = min(TOKEN_TILE, tm)
    _interleave(*[body(slice(t * sub, (t + 1) * sub)) for t in range(tm // sub)])


def _head_ones(n, width, scale=1.0):
    r = lax.broadcasted_iota(jnp.int32, (n, n), 0) // width
    c = lax.broadcasted_iota(jnp.int32, (n, n), 1) // width
    return jnp.where(r == c, scale, 0.0).astype(F32)


def _cumsum_rows(tri, a):
    hi = a.astype(BF16)
    rest = a - hi.astype(F32)
    mid = rest.astype(BF16)
    lo = (rest - mid.astype(F32)).astype(BF16)
    return _dg(tri, hi) + _dg(tri, mid) + _dg(tri, lo)


def _head_sum(a, scale=1.0):
    ah, al = _split(a)
    ones = _head_ones(a.shape[-1], HEAD_DIM).astype(BF16)
    return (_dg(ah, ones) + _dg(al, ones)) * scale


def _lane_head(shape, width):
    return lax.broadcasted_iota(jnp.int32, shape, len(shape) - 1) // width


def _sigmoid(x):
    return 1.0 / (1.0 + jnp.exp(-x))


def _interleave(*chains):
    results = [None] * len(chains)
    live = list(range(len(chains)))
    while live:
        for n in list(live):
            try:
                next(chains[n])
            except StopIteration as done:
                results[n] = done.value
                live.remove(n)
    return results


def _mods_kernel(cc_ref, w_ref, b_ref, o_ref):
    cc = cc_ref[...]
    o_ref[0] = _dg(cc * _sigmoid(cc), w_ref[0], NN, HI) + b_ref[0]


def _mods(cc, w_ada, b_ada):
    depth, d, n = w_ada.shape
    tn = 1536
    return pl.pallas_call(
        _mods_kernel,
        grid=(depth, n // tn),
        in_specs=[pl.BlockSpec(cc.shape, lambda l, j: (0, 0)),
                  pl.BlockSpec((1, d, tn), lambda l, j: (l, 0, j)),
                  pl.BlockSpec((1, 1, tn), lambda l, j: (l, 0, j))],
        out_specs=pl.BlockSpec((1, cc.shape[0], tn), lambda l, j: (l, 0, j)),
        out_shape=jax.ShapeDtypeStruct((depth, cc.shape[0], n), F32),
        compiler_params=_cparams("parallel", "parallel"),
        name="adaln_mods",
    )(cc, w_ada, b_ada.reshape(depth, 1, n))


def _inproj_kernel(x_ref, mod_ref, g_ref, w_ref, zc_ref, zm_ref, zr_ref, zt_ref):
    m = mod_ref[0]

    def rows(tok):
        x = x_ref[0, tok, :]
        xn = x * lax.rsqrt(jnp.mean(x * x, axis=-1, keepdims=True) + NORM_EPS) * g_ref[...]
        h = (xn * (1.0 + m[1:2]) + m[0:1]).astype(BF16)
        yield
        o = 0
        for ref, n in ((zc_ref, COLS_CONV), (zm_ref, COLS_MLA), (zr_ref, COLS_RWKV), (zt_ref, COLS_RET)):
            ref[0, tok, :] = _dg(h, w_ref[:, o:o + n])
            o += n
            yield

    _row_chains(x_ref.shape[1], rows)


def _inproj(x, mod, g, w):
    b, lq, d = x.shape
    tm = min(2 * TOKEN_TILE, lq)
    widths = (COLS_CONV, COLS_MLA, COLS_RWKV, COLS_RET)
    return pl.pallas_call(
        _inproj_kernel,
        grid=(b, lq // tm),
        in_specs=[pl.BlockSpec((1, tm, d), lambda i, j: (i, j, 0)),
                  pl.BlockSpec((1, 6, d), lambda i, j: (i, 0, 0)),
                  pl.BlockSpec((1, d), lambda i, j: (0, 0)),
                  pl.BlockSpec(w.shape, lambda i, j: (0, 0))],
        out_specs=[pl.BlockSpec((1, tm, n), lambda i, j: (i, j, 0)) for n in widths],
        out_shape=[jax.ShapeDtypeStruct((b, lq, n), F32) for n in widths],
        compiler_params=_cparams("parallel", "parallel"),
        name="inproj",
    )(x, mod, g.reshape(1, d), w)


def _halo_specs(tm, lq, width):
    r = tm // 8
    last = lq // 8 - 1
    return [pl.BlockSpec((1, tm, width), lambda i, j: (i, j, 0)),
            pl.BlockSpec((1, 8, width), lambda i, j: (i, jnp.maximum(j * r - 1, 0), 0)),
            pl.BlockSpec((1, 8, width), lambda i, j: (i, jnp.minimum((j + 1) * r, last), 0))]


def _shifted(t, prev_row, next_row):
    n = t.shape[0]
    rows = lax.broadcasted_iota(jnp.int32, t.shape, 0)
    t_prev = jnp.where(rows == 0, prev_row, pltpu.roll(t, 1, 0))
    t_next = jnp.where(rows == n - 1, next_row, pltpu.roll(t, n - 1, 0))
    return t_prev, t_next


def _conv_kernel(z_ref, zp_ref, zn_ref, cw_ref, g_ref, o_ref, *, nt):
    j = pl.program_id(1)
    w = GROUP_W
    z = z_ref[0]
    bgate = z[:, :w]
    u = z[:, w:2 * w] * z[:, 2 * w:]
    zp = zp_ref[0][7:8]
    zn = zn_ref[0][0:1]
    up = jnp.where(j > 0, zp[:, w:2 * w] * zp[:, 2 * w:], 0.0)
    un = jnp.where(j < nt - 1, zn[:, w:2 * w] * zn[:, 2 * w:], 0.0)
    u_prev, u_next = _shifted(u, up, un)
    cw = cw_ref[...]
    t = bgate * (cw[0:1] * u_prev + cw[1:2] * u + cw[2:3] * u_next)
    o_ref[0] = t * lax.rsqrt(jnp.mean(t * t, axis=-1, keepdims=True) + NORM_EPS) * g_ref[...]


def _conv_mixer(zc, conv_w, g_out):
    b, lq, width = zc.shape
    tm = min(4 * TOKEN_TILE, lq)
    return pl.pallas_call(
        functools.partial(_conv_kernel, nt=lq // tm),
        grid=(b, lq // tm),
        in_specs=_halo_specs(tm, lq, width) + [
            pl.BlockSpec((3, GROUP_W), lambda i, j: (0, 0)),
            pl.BlockSpec((1, GROUP_W), lambda i, j: (0, 0))],
        out_specs=pl.BlockSpec((1, tm, GROUP_W), lambda i, j: (i, j, 0)),
        out_shape=jax.ShapeDtypeStruct((b, lq, GROUP_W), F32),
        compiler_params=_cparams("parallel", "parallel"),
        name="conv_mixer",
    )(zc, zc, zc, conv_w, g_out.reshape(1, GROUP_W))


def _tile4(t):
    return jnp.concatenate([t, t, t, t], axis=-1)


def _slot_norm_rope(t, g, cos, sin):
    parts = []
    for h in range(N_HEADS):
        s = t[:, h * MLA_SLOT:(h + 1) * MLA_SLOT]
        ms = jnp.sum(s * s, axis=-1, keepdims=True) * (1.0 / MLA_QK)
        parts.append(s * lax.rsqrt(ms + NORM_EPS) * g)
    tn = jnp.concatenate(parts, axis=-1)
    n = tn.shape[-1]
    lane = lax.broadcasted_iota(jnp.int32, tn.shape, 1)
    half = MLA_ROPE // 4
    partner = jnp.where(lane % (2 * half) < half, pltpu.roll(tn, n - half, 1), pltpu.roll(tn, half, 1))
    return tn * _tile4(cos) + partner * _tile4(sin)


def _mla_prep_kernel(z_ref, cos_ref, sin_ref, gkv_ref, wukh_ref, wukl_ref, wuvh_ref, wuvl_ref, gq_ref, gk_ref,
                     q_ref, k_ref, v_ref):
    nq = N_HEADS * MLA_SLOT

    def rows(tok):
        z = z_ref[0, tok, :]
        q_in = z[:, :nq]
        k_rope = z[:, nq:nq + MLA_SLOT]
        ckv = z[:, nq + MLA_SLOT:]
        ckv = _split(ckv * lax.rsqrt(jnp.mean(ckv * ckv, axis=-1, keepdims=True) + NORM_EPS) * gkv_ref[...])
        k_in = _mm3(ckv, (wukh_ref[...], wukl_ref[...])) + _tile4(k_rope)
        v_ref[0, tok, :] = _mm3(ckv, (wuvh_ref[...], wuvl_ref[...])).astype(BF16)
        yield
        cos = cos_ref[tok, :]
        sin = sin_ref[tok, :]
        q_ref[0, tok, :] = (_slot_norm_rope(q_in, gq_ref[...], cos, sin) * (MLA_QK ** -0.5)).astype(BF16)
        yield
        k_ref[0, tok, :] = _slot_norm_rope(k_in, gk_ref[...], cos, sin).astype(BF16)

    _row_chains(z_ref.shape[1], rows)


def _mla_prep(zm, cos, sin, g_kv, wuk, wuv, g_q, g_k):
    b, lq, width = zm.shape
    tm = min(4 * TOKEN_TILE, lq)
    nq = N_HEADS * MLA_SLOT
    const = lambda shape: pl.BlockSpec(shape, lambda i, j: (0, 0))
    tok = lambda i, j: (i, j, 0)
    return pl.pallas_call(
        _mla_prep_kernel,
        grid=(b, lq // tm),
        in_specs=[pl.BlockSpec((1, tm, width), tok),
                  pl.BlockSpec((tm, MLA_SLOT), lambda i, j: (j, 0)),
                  pl.BlockSpec((tm, MLA_SLOT), lambda i, j: (j, 0)),
                  const((1, KV_RANK)), const(wuk[0].shape), const(wuk[0].shape),
                  const(wuv[0].shape), const(wuv[0].shape),
                  const((1, MLA_SLOT)), const((1, MLA_SLOT))],
        out_specs=[pl.BlockSpec((1, tm, nq), tok), pl.BlockSpec((1, tm, nq), tok), pl.BlockSpec((1, tm, GROUP_W), tok)],
        out_shape=[jax.ShapeDtypeStruct((b, lq, nq), BF16)] * 2 + [jax.ShapeDtypeStruct((b, lq, GROUP_W), BF16)],
        compiler_params=_cparams("parallel", "parallel"),
        name="mla_prep",
    )(zm, cos, sin, g_kv.reshape(1, KV_RANK), *wuk, *wuv, g_q, g_k)


def _attn_head(q, k, v):
    s = _dg(q, k, NT)
    yield
    p = jnp.exp(s - jnp.max(s, axis=-1, keepdims=True))
    inv = 1.0 / jnp.sum(p, axis=-1, keepdims=True)
    return _mm(p, v) * inv


def _attn_kernel(q_ref, k_ref, v_ref, g_ref, o_ref):
    v = v_ref[0]
    tq = q_ref.shape[1]
    head = _lane_head((tq, GROUP_W), HEAD_DIM)
    slots = [slice(h * MLA_SLOT, (h + 1) * MLA_SLOT) for h in range(N_HEADS)]
    outs = []
    for pair in (slots[:2], slots[2:]):
        outs += _interleave(*[_attn_head(q_ref[0, :, sl], k_ref[0, :, sl], v) for sl in pair])
    out = outs[0]
    for h in range(1, N_HEADS):
        out = jnp.where(head == h, outs[h], out)
    o_ref[0] = out * lax.rsqrt(jnp.mean(out * out, axis=-1, keepdims=True) + NORM_EPS) * g_ref[...]


def _attention(q, k, v, g_out):
    b, lq, nq = q.shape
    lk = k.shape[1]
    tq = min(2 * TOKEN_TILE, lq)
    return pl.pallas_call(
        _attn_kernel,
        grid=(b, lq // tq),
        in_specs=[pl.BlockSpec((1, tq, nq), lambda i, j: (i, j, 0)),
                  pl.BlockSpec((1, lk, nq), lambda i, j: (i, 0, 0)),
                  pl.BlockSpec((1, lk, GROUP_W), lambda i, j: (i, 0, 0)),
                  pl.BlockSpec((1, GROUP_W), lambda i, j: (0, 0))],
        out_specs=pl.BlockSpec((1, tq, GROUP_W), lambda i, j: (i, j, 0)),
        out_shape=jax.ShapeDtypeStruct((b, lq, GROUP_W), F32),
        compiler_params=_cparams("parallel", "arbitrary"),
        name="mla_attention",
    )(q, k, v, g_out.reshape(1, GROUP_W))


def _rwkv_proj_kernel(z_ref, zp_ref, zn_ref, mu_ref, kk_ref, ka_ref, w0_ref, a0_ref, lora_ref,
                      r_ref, v_ref, kkn_ref, gl_ref, ks_ref,
                      lw0_ref, b0_ref, kt0_ref, lw1_ref, b1_ref, kt1_ref, *, nt):
    j = pl.program_id(1)
    w = GROUP_W
    z = z_ref[0]
    zp = jnp.where(j > 0, zp_ref[0][7:8], 0.0)
    zn = jnp.where(j < nt - 1, zn_ref[0][0:1], 0.0)
    z_prev, z_next = _shifted(z, zp, zn)
    z_mix = z + mu_ref[...] * (0.5 * (z_prev + z_next) - z)

    def rows(tok):
        zs = z_mix[tok]
        r = zs[:, :w]
        k = zs[:, w:2 * w]
        lora_in = zs[:, 3 * w:3 * w + DECAY_LORA + ICLR_LORA]
        kk = k * kk_ref[...]
        kk = kk * lax.rsqrt(_head_sum(kk * kk) + 1e-12)
        lane = lax.broadcasted_iota(jnp.int32, lora_in.shape, 1)
        lora_in = jnp.where(lane < DECAY_LORA, jnp.tanh(lora_in), lora_in)
        r_ref[0, tok, :] = r
        v_ref[0, tok, :] = zs[:, 2 * w:3 * w]
        kkn_ref[0, tok, :] = kk
        gl_ref[0, tok, :] = zs[:, 3 * w + DECAY_LORA + ICLR_LORA:]
        ksum = jnp.zeros_like(k)
        for d, (lw_ref, b_ref, kt_ref) in enumerate(((lw0_ref, b0_ref, kt0_ref), (lw1_ref, b1_ref, kt1_ref))):
            lo = _mm3(lora_in, lora_ref[d])
            yield
            t = -(w0_ref[d:d + 1] + lo[:, :w])
            softplus = jnp.maximum(t, 0.0) + jnp.log1p(jnp.exp(-jnp.abs(t)))
            lw_ref[0, tok, :] = -jnp.exp(-softplus - 0.5)
            a = _sigmoid(a0_ref[d:d + 1] + lo[:, w:])
            kt = k * (1.0 + (a - 1.0) * ka_ref[...])
            b_ref[0, tok, :] = kk * a
            kt_ref[0, tok, :] = kt
            ksum = ksum + kt
        ks_ref[0, tok, :] = ksum

    _row_chains(z.shape[0], rows)


def _rwkv_proj(zr, mu, k_k, k_a, w0, a0, lora):
    b, lq, width = zr.shape
    tm = min(4 * TOKEN_TILE, lq)
    w = GROUP_W
    const2 = lambda shape: pl.BlockSpec(shape, lambda i, j: (0,) * len(shape))
    out_w = (w, w, w, GATE_LORA, w, w, w, w, w, w, w)
    return pl.pallas_call(
        functools.partial(_rwkv_proj_kernel, nt=lq // tm),
        grid=(b, lq // tm),
        in_specs=_halo_specs(tm, lq, width) + [
            const2((1, width)), const2((1, w)), const2((1, w)), const2((2, w)), const2((2, w)),
            const2(lora.shape)],
        out_specs=[pl.BlockSpec((1, tm, n), lambda i, j: (i, j, 0)) for n in out_w],
        out_shape=[jax.ShapeDtypeStruct((b, lq, n), F32) for n in out_w],
        compiler_params=_cparams("parallel", "parallel"),
        name="rwkv_proj",
    )(zr, zr, zr, mu.reshape(1, width), k_k.reshape(1, w), k_a.reshape(1, w), w0, a0, lora)


def _stack_heads(t):
    head = _lane_head(t.shape, HEAD_DIM)
    return jnp.concatenate([jnp.where(head == h, t, 0.0) for h in range(N_HEADS)], axis=0)


def _unit_lower_inverse(n_mat, p_i, q_i, size):
    base = RWKV_INV_BASE
    same = lambda w: (p_i // w) == (q_i // w)
    m = jnp.where(same(base), -n_mat, 0.0)
    inv = jnp.where(p_i == q_i, 1.0, 0.0) + m
    pw = m
    span = 1
    while 2 * span < base:
        pw = _mm(pw, pw)
        yield
        inv = inv + _mm(inv, pw)
        yield
        span *= 2
    w = base
    while w < size:
        off = jnp.where(same(2 * w) & jnp.logical_not(same(w)), n_mat, 0.0)
        invb = inv.astype(BF16)
        left = _mm(invb, off)
        yield
        inv = inv - _mm(left, invb)
        yield
        w *= 2
    return inv


def _rwkv_chunk(lw, kk, b, kt, v, r, s, reverse):
    c = RWKV_CHUNK
    n = N_HEADS * c
    t_i = lax.broadcasted_iota(jnp.int32, (c, c), 0)
    s_i = lax.broadcasted_iota(jnp.int32, (c, c), 1)
    seen = (s_i >= t_i) if reverse else (s_i <= t_i)
    g = _cumsum_rows(jnp.where(seen, 1.0, 0.0).astype(BF16), lw)
    g_tot = g[0:1] if reverse else g[c - 1:c]
    e_neg = jnp.exp(-g)
    e_tot = jnp.exp(g_tot)
    bd = _stack_heads(b * e_neg)
    ktd = _stack_heads(kt * e_neg)
    vs = _stack_heads(v).astype(BF16)
    lhs = jnp.concatenate([_stack_heads(kk * jnp.exp(g - lw)), _stack_heads(r * jnp.exp(g))], axis=0).astype(BF16)
    gram = _mm(lhs, jnp.concatenate([bd, ktd], axis=0), NT)
    from_state = _mm(lhs, s, NT)
    yield

    p_i = lax.broadcasted_iota(jnp.int32, (n, n), 0)
    q_i = lax.broadcasted_iota(jnp.int32, (n, n), 1)
    strict = (q_i > p_i) if reverse else (q_i < p_i)
    incl = (q_i >= p_i) if reverse else (q_i <= p_i)
    n_ab = jnp.where(strict, gram[:n, :n], 0.0)
    n_ak = jnp.where(strict, gram[:n, n:], 0.0)
    a_rb = jnp.where(incl, gram[n:, :n], 0.0)
    a_rk = jnp.where(incl, gram[n:, n:], 0.0)
    from_v = _mm(jnp.concatenate([n_ak, a_rk], axis=0), vs)
    t_inv = yield from _unit_lower_inverse(n_ab, p_i, q_i, c)
    u = (-_mm(t_inv, from_state[:n] + from_v[:n])).astype(BF16)
    yield
    y = from_state[n:] + _mm(a_rb, u) + from_v[n:]
    y = y[0:c] + y[c:2 * c] + y[2 * c:3 * c] + y[3 * c:]
    s_new = s * e_tot + _mm(jnp.concatenate([u, vs], axis=0),
                            jnp.concatenate([bd * e_tot, ktd * e_tot], axis=0), TN)
    return y, s_new


def _rwkv_scan_kernel(lwf_ref, kkf_ref, bf_ref, ktf_ref, vf_ref, rf_ref,
                      lwb_ref, kkb_ref, bb_ref, ktb_ref, vb_ref, rb_ref, sf0_ref, sb0_ref,
                      yf_ref, yb_ref, sfo_ref, sbo_ref, sf_ref, sb_ref, *, nc):
    i = pl.program_id(1)

    @pl.when(i == 0)
    def _():
        sf_ref[...] = sf0_ref[...]
        sb_ref[...] = sb0_ref[...]

    chains = []
    for n in range(RWKV_BATCH):
        chains.append(_rwkv_chunk(lwf_ref[n], kkf_ref[n], bf_ref[n], ktf_ref[n], vf_ref[n], rf_ref[n], sf_ref[n], False))
        chains.append(_rwkv_chunk(lwb_ref[n], kkb_ref[n], bb_ref[n], ktb_ref[n], vb_ref[n], rb_ref[n], sb_ref[n], True))
    results = _interleave(*chains)
    for n in range(RWKV_BATCH):
        (yf, sf), (yb, sb) = results[2 * n], results[2 * n + 1]
        yf_ref[n] = yf
        yb_ref[n] = yb
        sf_ref[n] = sf
        sb_ref[n] = sb

    @pl.when(i == nc - 1)
    def _():
        sfo_ref[...] = sf_ref[...]
        sbo_ref[...] = sb_ref[...]


def _rwkv_scan(fw, bw, kk, v, r, s_fw, s_bw):
    b, lq, w = kk.shape
    c = RWKV_CHUNK
    nc = lq // c
    nb = RWKV_BATCH
    fspec = pl.BlockSpec((nb, c, w), lambda i, j: (i, j, 0))
    bspec = pl.BlockSpec((nb, c, w), lambda i, j: (i, nc - 1 - j, 0))
    sspec = pl.BlockSpec((nb, w, w), lambda i, j: (i, 0, 0))
    yshape = jax.ShapeDtypeStruct((b, lq, w), F32)
    sshape = jax.ShapeDtypeStruct((b, w, w), F32)
    return pl.pallas_call(
        functools.partial(_rwkv_scan_kernel, nc=nc),
        grid=(b // nb, nc),
        in_specs=[fspec] * 6 + [bspec] * 6 + [sspec, sspec],
        out_specs=[fspec, bspec, sspec, sspec],
        out_shape=[yshape, yshape, sshape, sshape],
        scratch_shapes=[pltpu.VMEM((nb, w, w), F32), pltpu.VMEM((nb, w, w), F32)],
        compiler_params=_cparams("parallel", "arbitrary"),
        name="rwkv_scan",
    )(fw[0], kk, fw[1], fw[2], v, r, bw[0], kk, bw[1], bw[2], v, r, s_fw, s_bw)


def _rwkv_out_kernel(yf_ref, yb_ref, r_ref, v_ref, ks_ref, gl_ref, g2h_ref, g2l_ref, rk_ref, gln_ref, o_ref):
    def rows(tok):
        y = yf_ref[0, tok, :] + yb_ref[0, tok, :]
        bonus = _head_sum(r_ref[0, tok, :] * ks_ref[0, tok, :] * rk_ref[...])
        gate = _mm3(_sigmoid(gl_ref[0, tok, :]), (g2h_ref[...], g2l_ref[...]))
        dlt = y - _head_sum(y, 1.0 / HEAD_DIM)
        yield
        var = _head_sum(dlt * dlt, 1.0 / HEAD_DIM)
        yield
        yn = dlt * lax.rsqrt(var + RWKV_GN_EPS) * gln_ref[...] + bonus * v_ref[0, tok, :]
        o_ref[0, tok, :] = yn * gate

    _row_chains(yf_ref.shape[1], rows)


def _rwkv_out(yf, yb, r, v, ks, gl, g2, r_k, g_ln):
    b, lq, w = yf.shape
    tm = min(4 * TOKEN_TILE, lq)
    tspec = pl.BlockSpec((1, tm, w), lambda i, j: (i, j, 0))
    const = lambda shape: pl.BlockSpec(shape, lambda i, j: (0, 0))
    return pl.pallas_call(
        _rwkv_out_kernel,
        grid=(b, lq // tm),
        in_specs=[tspec] * 5 + [pl.BlockSpec((1, tm, GATE_LORA), lambda i, j: (i, j, 0)),
                                const(g2[0].shape), const(g2[0].shape), const((1, w)), const((1, w))],
        out_specs=tspec,
        out_shape=jax.ShapeDtypeStruct((b, lq, w), F32),
        compiler_params=_cparams("parallel", "parallel"),
        name="rwkv_out",
    )(yf, yb, r, v, ks, gl, *g2, r_k.reshape(1, w), g_ln.reshape(1, w))


def _half_rotate(t, cos, sin):
    n = t.shape[-1]
    lane = lax.broadcasted_iota(jnp.int32, t.shape, 1)
    half = HEAD_DIM // 2
    partner = jnp.where(lane % HEAD_DIM < half, pltpu.roll(t, n - half, 1), pltpu.roll(t, half, 1))
    return t * cos + partner * sin


def _ret_chunk(z, cos, sin, xi, zeta, dmat_ref, dec, st):
    w = GROUP_W
    q = _half_rotate(z[:, :w] * (HEAD_DIM ** -0.5), cos, sin)
    k = _half_rotate(z[:, w:2 * w], cos, sin)
    vs = z[:, 2 * w:3 * w].astype(BF16)
    ks = k.astype(BF16)
    head = _lane_head(q.shape, HEAD_DIM)
    y = _mm(q * xi, st)
    upd = _mm(k * zeta, vs, TN)
    scores = [_mm(jnp.where(head == h, q, 0.0), ks, NT) * dmat_ref[h] for h in range(N_HEADS)]
    yield
    for h in range(N_HEADS):
        y = y + jnp.where(head == h, _mm(scores[h], vs), 0.0)
    return y, st * dec + upd * _head_ones(w, HEAD_DIM)


def _ret_scan_kernel(zf_ref, cosf_ref, sinf_ref, zb_ref, cosb_ref, sinb_ref,
                     xif_ref, zetaf_ref, dmatf_ref, xib_ref, zetab_ref, dmatb_ref, dec_ref, rf0_ref, rb0_ref,
                     yf_ref, yb_ref, rfo_ref, rbo_ref, stf_ref, stb_ref, *, nc):
    i = pl.program_id(1)

    @pl.when(i == 0)
    def _():
        stf_ref[...] = rf0_ref[...]
        stb_ref[...] = rb0_ref[...]

    dec = dec_ref[...]
    chains = []
    for n in range(RET_BATCH):
        chains.append(_ret_chunk(zf_ref[n], cosf_ref[...], sinf_ref[...], xif_ref[...], zetaf_ref[...], dmatf_ref,
                                 dec, stf_ref[n]))
        chains.append(_ret_chunk(zb_ref[n], cosb_ref[...], sinb_ref[...], xib_ref[...], zetab_ref[...], dmatb_ref,
                                 dec, stb_ref[n]))
    results = _interleave(*chains)
    for n in range(RET_BATCH):
        (yf, stf), (yb, stb) = results[2 * n], results[2 * n + 1]
        yf_ref[n] = yf
        yb_ref[n] = yb
        stf_ref[n] = stf
        stb_ref[n] = stb

    @pl.when(i == nc - 1)
    def _():
        rfo_ref[...] = stf_ref[...]
        rbo_ref[...] = stb_ref[...]


def _ret_scan(zt, cos, sin, r_fw, r_bw):
    b, lq, width = zt.shape
    c = min(RET_CHUNK, lq)
    nc = lq // c
    w = GROUP_W
    xi_f, zeta_f, dmat_f, dec = _ret_decay_tables(c, False)
    xi_b, zeta_b, dmat_b, _ = _ret_decay_tables(c, True)
    fw3 = lambda i, j: (i, j, 0)
    bw3 = lambda i, j: (i, nc - 1 - j, 0)
    fw2 = lambda i, j: (j, 0)
    bw2 = lambda i, j: (nc - 1 - j, 0)
    nb = RET_BATCH
    const = lambda shape: pl.BlockSpec(shape, lambda i, j: (0,) * len(shape))
    sspec = pl.BlockSpec((nb, w, w), lambda i, j: (i, 0, 0))
    tables = [const((c, w)), const((c, w)), const((N_HEADS, c, c))]
    yshape = jax.ShapeDtypeStruct((b, lq, w), F32)
    sshape = jax.ShapeDtypeStruct((b, w, w), F32)
    return pl.pallas_call(
        functools.partial(_ret_scan_kernel, nc=nc),
        grid=(b // nb, nc),
        in_specs=[pl.BlockSpec((nb, c, width), fw3), pl.BlockSpec((c, w), fw2), pl.BlockSpec((c, w), fw2),
                  pl.BlockSpec((nb, c, width), bw3), pl.BlockSpec((c, w), bw2), pl.BlockSpec((c, w), bw2)]
        + tables + tables + [const((1, w)), sspec, sspec],
        out_specs=[pl.BlockSpec((nb, c, w), fw3), pl.BlockSpec((nb, c, w), bw3), sspec, sspec],
        out_shape=[yshape, yshape, sshape, sshape],
        scratch_shapes=[pltpu.VMEM((nb, w, w), F32), pltpu.VMEM((nb, w, w), F32)],
        compiler_params=_cparams("parallel", "arbitrary"),
        name="ret_scan",
    )(zt, cos, sin, zt, cos, sin, xi_f, zeta_f, dmat_f, xi_b, zeta_b, dmat_b, dec, r_fw, r_bw)


def _ret_out_kernel(yf_ref, yb_ref, z_ref, g_ref, o_ref):
    def rows(tok):
        y = yf_ref[0, tok, :] + yb_ref[0, tok, :]
        ms = _head_sum(y * y, 1.0 / HEAD_DIM)
        yield
        gate = z_ref[0, tok, :]
        o_ref[0, tok, :] = gate * _sigmoid(gate) * (y * lax.rsqrt(ms + NORM_EPS) * g_ref[...])

    _row_chains(yf_ref.shape[1], rows)


def _ret_out(yf, yb, zt, g_norm):
    b, lq, w = yf.shape
    tm = min(4 * TOKEN_TILE, lq)
    tspec = pl.BlockSpec((1, tm, w), lambda i, j: (i, j, 0))
    return pl.pallas_call(
        _ret_out_kernel,
        grid=(b, lq // tm),
        in_specs=[tspec, tspec, pl.BlockSpec((1, tm, w), lambda i, j: (i, j, 3)),
                  pl.BlockSpec((1, w), lambda i, j: (0, 0))],
        out_specs=tspec,
        out_shape=jax.ShapeDtypeStruct((b, lq, w), F32),
        compiler_params=_cparams("parallel", "parallel"),
        name="ret_out",
    )(yf, yb, zt, g_norm.reshape(1, w))


def _outproj_rows(tok, y_refs, x_ref, m, w_ref, g_ref, wr_ref, br_ref, xo_ref, h_ref, lg_ref):
    w = GROUP_W
    acc = None
    for n, ref in enumerate(y_refs):
        part = _mm(ref[0, tok, :], w_ref[n * w:(n + 1) * w, :])
        acc = part if acc is None else acc + part
    yield
    x = x_ref[0, tok, :] + m[2:3] * acc
    xo_ref[0, tok, :] = x
    xn = x * lax.rsqrt(jnp.mean(x * x, axis=-1, keepdims=True) + NORM_EPS) * g_ref[...]
    h = xn * (1.0 + m[4:5]) + m[3:4]
    h_ref[0, tok, :] = h.astype(BF16)
    yield
    lg_ref[0, tok, :] = _mm3(h, wr_ref[...]) + br_ref[...]


def _outproj_kernel(yc_ref, ym_ref, yr_ref, yt_ref, x_ref, mod_ref, w_ref, g_ref, wr_ref, br_ref,
                    xo_ref, h_ref, lg_ref):
    tm = x_ref.shape[1]
    sub = min(TOKEN_TILE, tm)
    m = mod_ref[0]
    _interleave(*[_outproj_rows(slice(t * sub, (t + 1) * sub), (yc_ref, ym_ref, yr_ref, yt_ref), x_ref, m,
                                w_ref, g_ref, wr_ref, br_ref, xo_ref, h_ref, lg_ref) for t in range(tm // sub)])


def _outproj(ys, x, mod, w_out, g2, w_r, b_r):
    b, lq, d = x.shape
    tm = min(2 * TOKEN_TILE, lq)
    yspec = pl.BlockSpec((1, tm, GROUP_W), lambda i, j: (i, j, 0))
    xspec = pl.BlockSpec((1, tm, d), lambda i, j: (i, j, 0))
    const = lambda shape: pl.BlockSpec(shape, lambda i, j: (0, 0))
    return pl.pallas_call(
        _outproj_kernel,
        grid=(b, lq // tm),
        in_specs=[yspec] * 4 + [xspec, pl.BlockSpec((1, 6, d), lambda i, j: (i, 0, 0)),
                                const(w_out.shape), const((1, d)),
                                const(w_r.shape), const((1, N_EXP))],
        out_specs=[xspec, xspec, pl.BlockSpec((1, tm, N_EXP), lambda i, j: (i, j, 0))],
        out_shape=[jax.ShapeDtypeStruct((b, lq, d), F32), jax.ShapeDtypeStruct((b, lq, d), BF16),
                   jax.ShapeDtypeStruct((b, lq, N_EXP), F32)],
        compiler_params=_cparams("parallel", "parallel"),
        name="outproj",
    )(*ys, x, mod, w_out, g2.reshape(1, d), w_r, b_r.reshape(1, N_EXP))


def _lane_cumsum(m):
    n = m.shape[-1]
    blk = 128
    r_i = lax.broadcasted_iota(jnp.int32, (blk, blk), 0)
    c_i = lax.broadcasted_iota(jnp.int32, (blk, blk), 1)
    tri = jnp.where(r_i <= c_i, 1.0, 0.0).astype(BF16)
    run = jnp.zeros((m.shape[0], 1), F32)
    parts = []
    for j in range(n // blk):
        cs = _dg(m[:, j * blk:(j + 1) * blk].astype(BF16), tri) + run
        parts.append(cs)
        run = cs[:, blk - 1:blk]
    return jnp.concatenate(parts, axis=-1)


def _route_kernel(lg_ref, rank_ref, gate_ref, *, cap):
    lg = lg_ref[0]
    e = jnp.exp(lg - jnp.max(lg, axis=0, keepdims=True))
    aff = e / jnp.sum(e, axis=0, keepdims=True)
    bits = pltpu.bitcast(aff, jnp.int32)
    thr = jnp.zeros((aff.shape[0], 1), jnp.int32)
    for bit in range(30, -1, -1):
        cand = thr | (1 << bit)
        cnt = jnp.sum(jnp.where(bits >= cand, 1.0, 0.0), axis=-1, keepdims=True)
        thr = jnp.where(cnt >= cap, cand, thr)
    above = bits > thr
    tied = bits == thr
    n_above = jnp.sum(jnp.where(above, 1.0, 0.0), axis=-1, keepdims=True)
    tied_f = jnp.where(tied, 1.0, 0.0)
    tied_rank = _lane_cumsum(tied_f) - tied_f
    sel = above | (tied & (tied_rank < cap - n_above))
    sel_f = jnp.where(sel, 1.0, 0.0)
    rank = _lane_cumsum(sel_f) - sel_f
    rank_ref[0] = jnp.where(sel, rank, -1.0).astype(jnp.int32)
    gate_ref[0] = jnp.where(sel, aff, 0.0)


def _route(logits_t, cap):
    b, ne, lq = logits_t.shape
    spec = pl.BlockSpec((1, ne, lq), lambda i: (i, 0, 0))
    return pl.pallas_call(
        functools.partial(_route_kernel, cap=cap),
        grid=(b,),
        in_specs=[spec],
        out_specs=[spec, spec],
        out_shape=[jax.ShapeDtypeStruct((b, ne, lq), jnp.int32), jax.ShapeDtypeStruct((b, ne, lq), F32)],
        compiler_params=_cparams("parallel"),
        name="moe_route",
    )(logits_t)


def _onehot_rows(rank_row, cap):
    r_i = lax.broadcasted_iota(jnp.int32, (cap, rank_row.shape[-1]), 0)
    return r_i == rank_row


def _gather_kernel(h_ref, rank_ref, gate_ref, xe_ref, gr_ref, *, cap):
    lq = h_ref.shape[1]
    ts = min(MOE_TOKEN_SLICE, lq)
    xe = gr = None
    for t in range(lq // ts):
        tok = slice(t * ts, (t + 1) * ts)
        sel = _onehot_rows(rank_ref[0, 0, :, tok], cap)
        px = _dg(jnp.where(sel, 1.0, 0.0).astype(BF16), h_ref[0, tok, :])
        pg = jnp.sum(jnp.where(sel, gate_ref[0, 0, :, tok], 0.0), axis=-1, keepdims=True)
        xe, gr = (px, pg) if xe is None else (xe + px, gr + pg)
    xe_ref[0, 0] = xe.astype(BF16)
    gr_ref[0, 0] = gr


def _gather(h, rank, gate, cap):
    b, lq, d = h.shape
    ne = rank.shape[1]
    rspec = pl.BlockSpec((1, 1, 1, lq), lambda i, e: (i, e, 0, 0))
    xspec = pl.BlockSpec((1, 1, cap, d), lambda i, e: (i, e, 0, 0))
    return pl.pallas_call(
        functools.partial(_gather_kernel, cap=cap),
        grid=(b, ne),
        in_specs=[pl.BlockSpec((1, lq, d), lambda i, e: (i, 0, 0)), rspec, rspec],
        out_specs=[xspec, pl.BlockSpec((1, 1, cap, 1), lambda i, e: (i, e, 0, 0))],
        out_shape=[jax.ShapeDtypeStruct((b, ne, cap, d), BF16), jax.ShapeDtypeStruct((b, ne, cap, 1), F32)],
        compiler_params=_cparams("parallel", "arbitrary"),
        name="moe_gather",
    )(h, rank.reshape(b, ne, 1, lq), gate.reshape(b, ne, 1, lq))


def _ffn_kernel(xe_ref, gr_ref, wg_ref, wu_ref, wd_ref, ye_ref, wgb_ref, wub_ref, wdb_ref):
    @pl.when(pl.program_id(1) == 0)
    def _():
        wgb_ref[...] = wg_ref[0, 0].astype(BF16)
        wub_ref[...] = wu_ref[0, 0].astype(BF16)
        wdb_ref[...] = wd_ref[0, 0].astype(BF16)

    nb, _, cap, d = xe_ref.shape
    xe = xe_ref[...].reshape(nb * cap, d)
    a = _dg(xe, wgb_ref[...])
    hid = a * _sigmoid(a) * _dg(xe, wub_ref[...])
    ye = _mm(hid, wdb_ref[...]) * gr_ref[...].reshape(nb * cap, 1)
    ye_ref[...] = ye.astype(BF16).reshape(nb, 1, cap, d)


def _expert_ffn(xe, gr, w_gate, w_up, w_down, layer):
    b, ne, cap, d = xe.shape
    ff = w_gate.shape[3]
    nb = FFN_BATCH
    xspec = pl.BlockSpec((nb, 1, cap, d), lambda e, i: (i, e, 0, 0))
    wspec = lambda shape: pl.BlockSpec((1, 1) + shape, lambda e, i: (layer, e, 0, 0))
    return pl.pallas_call(
        _ffn_kernel,
        grid=(ne, b // nb),
        in_specs=[xspec, pl.BlockSpec((nb, 1, cap, 1), lambda e, i: (i, e, 0, 0)),
                  wspec((d, ff)), wspec((d, ff)), wspec((ff, d))],
        out_specs=xspec,
        out_shape=jax.ShapeDtypeStruct((b, ne, cap, d), BF16),
        scratch_shapes=[pltpu.VMEM((d, ff), BF16), pltpu.VMEM((d, ff), BF16), pltpu.VMEM((ff, d), BF16)],
        compiler_params=_cparams("parallel", "arbitrary"),
        name="moe_ffn",
    )(xe, gr, w_gate, w_up, w_down)


def _combine_kernel(x_ref, ga_ref, rank_ref, ye_ref, o_ref, *, cap):
    g = pl.program_id(2)
    lq = x_ref.shape[1]
    ts = min(COMBINE_TOKEN_SLICE, lq)
    ye = ye_ref[0].reshape(COMBINE_EXPERTS * cap, ye_ref.shape[-1])
    for t in range(lq // ts):
        tok = slice(t * ts, (t + 1) * ts)
        p = jnp.concatenate([jnp.where(_onehot_rows(rank_ref[0, e, :, tok], cap), 1.0, 0.0).astype(BF16)
                             for e in range(COMBINE_EXPERTS)], axis=0)
        part = _dg(p, ye, TN)

        @pl.when(g == 0)
        def _():
            o_ref[0, tok, :] = part

        @pl.when(g > 0)
        def _():
            o_ref[0, tok, :] += part

    @pl.when(g == N_EXP // COMBINE_EXPERTS - 1)
    def _():
        o_ref[0] = x_ref[0] + ga_ref[0] * o_ref[0]


def _combine(x, mod, rank, ye, cap):
    b, lq, d = x.shape
    ne = rank.shape[1]
    lt = lq // 2 if lq >= 2 * COMBINE_TOKEN_SLICE else lq
    xspec = pl.BlockSpec((1, lt, d), lambda i, j, e: (i, j, 0))
    yspec = pl.BlockSpec((1, COMBINE_EXPERTS, cap, d), lambda i, j, e: (i, e, 0, 0))
    return pl.pallas_call(
        functools.partial(_combine_kernel, cap=cap),
        grid=(b, lq // lt, ne // COMBINE_EXPERTS),
        in_specs=[xspec, pl.BlockSpec((1, 1, d), lambda i, j, e: (i, 0, 0)),
                  pl.BlockSpec((1, COMBINE_EXPERTS, 1, lt), lambda i, j, e: (i, e, 0, j)), yspec],
        out_specs=xspec,
        out_shape=jax.ShapeDtypeStruct((b, lq, d), F32),
        compiler_params=_cparams("parallel", "parallel", "arbitrary"),
        name="moe_combine",
    )(x, mod[:, 5:6, :], rank.reshape(b, ne, 1, lq), ye)


def _split_w(w):
    hi = w.astype(BF16)
    return hi, (w - hi.astype(F32)).astype(BF16)


def _layout_w_in(w):
    d = w.shape[0]
    o = COLS_CONV
    q_nope = w[:, o:o + N_HEADS * MLA_NOPE]
    q_rope = w[:, o + N_HEADS * MLA_NOPE:o + N_HEADS * MLA_QK]
    ckv = w[:, o + N_HEADS * MLA_QK:o + N_HEADS * MLA_QK + KV_RANK]
    k_rope = w[:, o + N_HEADS * MLA_QK + KV_RANK:o + COLS_MLA_IN]
    pad = jnp.zeros((d, MLA_SLOT - MLA_QK), w.dtype)
    cols = []
    for h in range(N_HEADS):
        cols += [q_nope[:, h * MLA_NOPE:(h + 1) * MLA_NOPE], q_rope[:, h * MLA_ROPE:(h + 1) * MLA_ROPE], pad]
    cols += [jnp.zeros((d, MLA_NOPE), w.dtype), k_rope, pad, ckv]
    mla = jnp.concatenate(cols, axis=1)
    return jnp.concatenate([w[:, :o].astype(BF16), mla.astype(BF16), w[:, o + COLS_MLA_IN:].astype(BF16)], axis=1)


def _layout_w_uk(w_uk):
    pad = jnp.zeros((w_uk.shape[0], MLA_SLOT - MLA_NOPE), w_uk.dtype)
    cols = []
    for h in range(N_HEADS):
        cols += [w_uk[:, h * MLA_NOPE:(h + 1) * MLA_NOPE], pad]
    return _split_w(jnp.concatenate(cols, axis=1))


def _slot_gain(g):
    return jnp.concatenate([g, jnp.zeros((MLA_SLOT - MLA_QK,), g.dtype)]).reshape(1, MLA_SLOT)


def _layout_lora(w2, a2):
    z = jnp.zeros_like(w2)
    return jnp.concatenate([jnp.concatenate([w2, z], axis=2), jnp.concatenate([z, a2], axis=2)], axis=1)


def _mla_tables(lq):
    rows = lq // GRID_W
    row = jnp.repeat(jnp.arange(rows, dtype=F32), GRID_W)
    col = jnp.tile(jnp.arange(GRID_W, dtype=F32), rows)
    axis_dim = MLA_ROPE // 2
    inv_axis = ROPE_BASE ** (-jnp.arange(0, axis_dim, 2, dtype=F32) / axis_dim)
    ar = row[:, None] * inv_axis[None, :]
    ac = col[:, None] * inv_axis[None, :]
    ones = jnp.ones((lq, MLA_NOPE), F32)
    tail = jnp.ones((lq, MLA_SLOT - MLA_QK), F32)
    cos = jnp.concatenate([ones, jnp.cos(ar), jnp.cos(ar), jnp.cos(ac), jnp.cos(ac), tail], axis=1)
    sin = jnp.concatenate([0 * ones, -jnp.sin(ar), jnp.sin(ar), -jnp.sin(ac), jnp.sin(ac), 0 * tail], axis=1)
    return cos, sin


def _ret_tables(lq):
    theta = 1.0 / (RET_THETA_BASE ** jnp.linspace(0.0, 1.0, HEAD_DIM // 2, dtype=F32))
    ang = jnp.arange(lq, dtype=F32)[:, None] * theta[None, :]
    cos = jnp.tile(jnp.cos(ang), (1, 2 * N_HEADS))
    sin = jnp.tile(jnp.concatenate([-jnp.sin(ang), jnp.sin(ang)], axis=1), (1, N_HEADS))
    return cos, sin


def _ret_decay_tables(c, reverse):
    log_gamma = jnp.log1p(-jnp.exp2(-5.0 - jnp.arange(N_HEADS, dtype=F32)))
    lanes = jnp.repeat(log_gamma, HEAD_DIM)[None, :]
    j = jnp.arange(c, dtype=F32)
    if reverse:
        xi = jnp.exp((c - j)[:, None] * lanes)
        zeta = jnp.exp(j[:, None] * lanes)
        diff = j[None, :] - j[:, None]
        mask = diff > 0
    else:
        xi = jnp.exp((j + 1.0)[:, None] * lanes)
        zeta = jnp.exp((c - 1.0 - j)[:, None] * lanes)
        diff = j[:, None] - j[None, :]
        mask = diff >= 0
    dmat = jnp.where(mask[None], jnp.exp(jnp.where(mask, diff, 0.0)[None] * log_gamma[:, None, None]), 0.0)
    dec = jnp.exp(c * lanes)
    return xi, zeta, dmat, dec


def _mixers_pre(x, mod, lw, tables):
    zc, zm, zr, zt = _inproj(x, mod, lw['g_norm1'], lw['w_in'])
    y_conv = _conv_mixer(zc, lw['conv_w'], lw['g_out_conv'])
    q, k, v = _mla_prep(zm, tables['mla_cos'], tables['mla_sin'], lw['g_kv_norm'], lw['w_uk'],
                        lw['w_uv'], lw['g_q'], lw['g_k'])
    rw = _rwkv_proj(zr, lw['rwkv_mu'], lw['rwkv_k_k'], lw['rwkv_k_a'], lw['rwkv_w0'], lw['rwkv_a0'], lw['lora'])
    return dict(y_conv=y_conv, q=q, k=k, v=v, rw=rw, zt=zt)


def _rwkv_stream(rw, s_fw, s_bw, lw, emit):
    r, v, kk, gl, ks, lw0, b0, kt0, lw1, b1, kt1 = rw
    yf, yb, s_fw, s_bw = _rwkv_scan((lw0, b0, kt0), (lw1, b1, kt1), kk, v, r, s_fw, s_bw)
    y = _rwkv_out(yf, yb, r, v, ks, gl, lw['rwkv_g2'], lw['rwkv_r_k'], lw['g_ln_x']) if emit else None
    return y, s_fw, s_bw


def _ret_stream(zt, r_fw, r_bw, lw, tables, emit):
    yf, yb, r_fw, r_bw = _ret_scan(zt, tables['ret_cos'], tables['ret_sin'], r_fw, r_bw)
    y = _ret_out(yf, yb, zt, lw['g_ret_norm']) if emit else None
    return y, r_fw, r_bw


def _channel_mix(ys, x, mod, lw):
    x, h, logits = _outproj(ys, x, mod, lw['w_out'], lw['g_norm2'], lw['w_router'], lw['b_router'])
    cap = CAP_FACTOR * x.shape[1] // N_EXP
    rank, gate = _route(jnp.swapaxes(logits, 1, 2), cap)
    xe, gr = _gather(h, rank, gate, cap)
    ye = _expert_ffn(xe, gr, lw['w_gate'], lw['w_up'], lw['w_down'], lw['layer'])
    return _combine(x, mod, rank, ye, cap)


def kernel(x, c, ctx, c_ctx, w_ada, b_ada, g_norm1, g_norm2, w_in, w_out, conv_w, g_out_conv, g_kv_norm, w_uk, w_uv, g_q_norm, g_k_norm, g_out_mla, rwkv_mu, rwkv_w0, rwkv_w2, rwkv_a0, rwkv_a2, rwkv_g2, rwkv_k_k, rwkv_k_a, rwkv_r_k, g_ln_x, g_ret_norm, w_router, b_router, w_gate, w_up, w_down):
    depth = w_in.shape[0]
    bsz, lq, d = x.shape
    lc = ctx.shape[1]

    cc = jnp.zeros((16, d), F32).at[:bsz].set(c).at[bsz].set(c_ctx)
    mods = _mods(cc, w_ada, b_ada)

    cos_x, sin_x = _mla_tables(lq)
    rcos_x, rsin_x = _ret_tables(lq)
    tab_x = dict(mla_cos=cos_x, mla_sin=sin_x, ret_cos=rcos_x, ret_sin=rsin_x)
    tab_c = dict(mla_cos=jnp.ones((lc, MLA_SLOT), F32), mla_sin=jnp.zeros((lc, MLA_SLOT), F32),
                 ret_cos=jnp.ones((lc, GROUP_W), F32), ret_sin=jnp.zeros((lc, GROUP_W), F32))
    zero_state = jnp.zeros((bsz, GROUP_W, GROUP_W), F32)

    xc = ctx
    for l in range(depth):
        need_ctx = l < depth - 1
        lw = dict(
            g_norm1=g_norm1[l], g_norm2=g_norm2[l], w_in=_layout_w_in(w_in[l]), w_out=w_out[l].astype(BF16),
            conv_w=conv_w[l], g_out_conv=g_out_conv[l], g_kv_norm=g_kv_norm[l],
            w_uk=_layout_w_uk(w_uk[l]), w_uv=_split_w(w_uv[l]),
            g_q=_slot_gain(g_q_norm[l]), g_k=_slot_gain(g_k_norm[l]), g_out_mla=g_out_mla[l],
            rwkv_mu=rwkv_mu[l], rwkv_w0=rwkv_w0[l], rwkv_a0=rwkv_a0[l],
            lora=_layout_lora(rwkv_w2[l], rwkv_a2[l]), rwkv_g2=_split_w(rwkv_g2[l]),
            rwkv_k_k=rwkv_k_k[l], rwkv_k_a=rwkv_k_a[l], rwkv_r_k=rwkv_r_k[l], g_ln_x=g_ln_x[l],
            g_ret_norm=g_ret_norm[l], w_router=w_router[l], b_router=b_router[l],
            w_gate=w_gate, w_up=w_up, w_down=w_down, layer=l)
        mod_x = mods[l, :bsz].reshape(bsz, 6, d)
        mod_c = jnp.broadcast_to(mods[l, bsz].reshape(1, 6, d), (bsz, 6, d))

        pc = _mixers_pre(xc, mod_c, lw, tab_c)
        px = _mixers_pre(x, mod_x, lw, tab_x)

        k_all = jnp.concatenate([pc['k'], px['k']], axis=1)
        v_all = jnp.concatenate([pc['v'], px['v']], axis=1)
        ym_x = _attention(px['q'], k_all, v_all, lw['g_out_mla'])
        yr_c, s_fw, s_bw = _rwkv_stream(pc['rw'], zero_state, zero_state, lw, need_ctx)
        yr_x, _, _ = _rwkv_stream(px['rw'], s_fw, s_bw, lw, True)
        yt_c, r_fw, r_bw = _ret_stream(pc['zt'], zero_state, zero_state, lw, tab_c, need_ctx)
        yt_x, _, _ = _ret_stream(px['zt'], r_fw, r_bw, lw, tab_x, True)

        x = _channel_mix((px['y_conv'], ym_x, yr_x, yt_x), x, mod_x, lw)
        if need_ctx:
            ym_c = _attention(pc['q'], pc['k'], pc['v'], lw['g_out_mla'])
            xc = _channel_mix((pc['y_conv'], ym_c, yr_c, yt_c), xc, mod_c, lw)
    return x
```

```python
import functools

import jax
import jax.numpy as jnp
from jax import lax
from jax.experimental import pallas as pl
from jax.experimental.pallas import tpu as pltpu

F32 = jnp.float32
BF16 = jnp.bfloat16
HI = lax.Precision.HIGHEST

D_MODEL = 1024
N_HEADS = 4
HEAD_DIM = 64
GROUP_W = N_HEADS * HEAD_DIM
NORM_EPS = 1e-6
MLA_NOPE = 64
MLA_ROPE = 32
MLA_QK = MLA_NOPE + MLA_ROPE
MLA_SLOT = 128
KV_RANK = 128
ROPE_BASE = 10000.0
DECAY_LORA = 64
ICLR_LORA = 64
GATE_LORA = 128
RWKV_GN_EPS = 64e-5
RWKV_CHUNK = 64
RWKV_INV_BASE = 16
RWKV_BATCH = 4
RET_CHUNK = 256
RET_BATCH = 4
RET_THETA_BASE = 10000.0
N_EXP = 16
CAP_FACTOR = 2
GRID_W = 64

COLS_CONV = 3 * GROUP_W
COLS_MLA_IN = N_HEADS * MLA_QK + KV_RANK + MLA_ROPE
COLS_MLA = N_HEADS * MLA_SLOT + MLA_SLOT + KV_RANK
COLS_RWKV = 3 * GROUP_W + DECAY_LORA + ICLR_LORA + GATE_LORA
COLS_RET = 4 * GROUP_W

TOKEN_TILE = 256
MOE_TOKEN_SLICE = 1024
FFN_BATCH = 2
COMBINE_EXPERTS = 4
COMBINE_TOKEN_SLICE = 1024
VMEM_LIMIT = 56 * 1024 * 1024

NN = (((1,), (0,)), ((), ()))
NT = (((1,), (1,)), ((), ()))
TN = (((0,), (0,)), ((), ()))


def _cparams(*sem):
    return pltpu.CompilerParams(dimension_semantics=sem, vmem_limit_bytes=VMEM_LIMIT)


def _dg(a, b, dims=NN, precision=None):
    return lax.dot_general(a, b, dims, precision=precision, preferred_element_type=F32)


def _split(a):
    hi = a.astype(BF16)
    return hi, (a - hi.astype(F32)).astype(BF16)


def _mm3(a, b, dims=NN):
    ah, al = a if isinstance(a, tuple) else _split(a)
    bh, bl = b if isinstance(b, tuple) else _split(b)
    return _dg(ah, bh, dims) + _dg(ah, bl, dims) + _dg(al, bh, dims)


def _mm(a, b, dims=NN):
    return _dg(a.astype(BF16), b.astype(BF16), dims)


def _row_chains(tm, body):
    sub = min(TOKEN_TILE, tm)
    _interleave(*[body(slice(t * sub, (t + 1) * sub)) for t in range(tm // sub)])


def _head_ones(n, width, scale=1.0):
    r = lax.broadcasted_iota(jnp.int32, (n, n), 0) // width
    c = lax.broadcasted_iota(jnp.int32, (n, n), 1) // width
    return jnp.where(r == c, scale, 0.0).astype(F32)


def _cumsum_rows(tri, a):
    hi = a.astype(BF16)
    rest = a - hi.astype(F32)
    mid = rest.astype(BF16)
    lo = (rest - mid.astype(F32)).astype(BF16)
    return _dg(tri, hi) + _dg(tri, mid) + _dg(tri, lo)


def _head_sum(a, scale=1.0):
    ah, al = _split(a)
    ones = _head_ones(a.shape[-1], HEAD_DIM).astype(BF16)
    return (_dg(ah, ones) + _dg(al, ones)) * scale


def _lane_head(shape, width):
    return lax.broadcasted_iota(jnp.int32, shape, len(shape) - 1) // width


def _sigmoid(x):
    return 1.0 / (1.0 + jnp.exp(-x))


def _interleave(*chains):
    results = [None] * len(chains)
    live = list(range(len(chains)))
    while live:
        for n in list(live):
            try:
                next(chains[n])
            except StopIteration as done:
                results[n] = done.value
                live.remove(n)
    return results


def _mods_kernel(cc_ref, w_ref, b_ref, o_ref):
    cc = cc_ref[...]
    o_ref[0] = _dg(cc * _sigmoid(cc), w_ref[0], NN, HI) + b_ref[0]


def _mods(cc, w_ada, b_ada):
    depth, d, n = w_ada.shape
    tn = 1536
    return pl.pallas_call(
        _mods_kernel,
        grid=(depth, n // tn),
        in_specs=[pl.BlockSpec(cc.shape, lambda l, j: (0, 0)),
                  pl.BlockSpec((1, d, tn), lambda l, j: (l, 0, j)),
                  pl.BlockSpec((1, 1, tn), lambda l, j: (l, 0, j))],
        out_specs=pl.BlockSpec((1, cc.shape[0], tn), lambda l, j: (l, 0, j)),
        out_shape=jax.ShapeDtypeStruct((depth, cc.shape[0], n), F32),
        compiler_params=_cparams("parallel", "parallel"),
        name="adaln_mods",
    )(cc, w_ada, b_ada.reshape(depth, 1, n))


def _inproj_kernel(x_ref, mod_ref, g_ref, w_ref, zc_ref, zm_ref, zr_ref, zt_ref):
    m = mod_ref[0]

    def rows(tok):
        x = x_ref[0, tok, :]
        xn = x * lax.rsqrt(jnp.mean(x * x, axis=-1, keepdims=True) + NORM_EPS) * g_ref[...]
        h = (xn * (1.0 + m[1:2]) + m[0:1]).astype(BF16)
        yield
        o = 0
        for ref, n in ((zc_ref, COLS_CONV), (zm_ref, COLS_MLA), (zr_ref, COLS_RWKV), (zt_ref, COLS_RET)):
            ref[0, tok, :] = _dg(h, w_ref[:, o:o + n])
            o += n
            yield

    _row_chains(x_ref.shape[1], rows)


def _inproj(x, mod, g, w):
    b, lq, d = x.shape
    tm = min(2 * TOKEN_TILE, lq)
    widths = (COLS_CONV, COLS_MLA, COLS_RWKV, COLS_RET)
    return pl.pallas_call(
        _inproj_kernel,
        grid=(b, lq // tm),
        in_specs=[pl.BlockSpec((1, tm, d), lambda i, j: (i, j, 0)),
                  pl.BlockSpec((1, 6, d), lambda i, j: (i, 0, 0)),
                  pl.BlockSpec((1, d), lambda i, j: (0, 0)),
                  pl.BlockSpec(w.shape, lambda i, j: (0, 0))],
        out_specs=[pl.BlockSpec((1, tm, n), lambda i, j: (i, j, 0)) for n in widths],
        out_shape=[jax.ShapeDtypeStruct((b, lq, n), F32) for n in widths],
        compiler_params=_cparams("parallel", "parallel"),
        name="inproj",
    )(x, mod, g.reshape(1, d), w)


def _halo_specs(tm, lq, width):
    r = tm // 8
    last = lq // 8 - 1
    return [pl.BlockSpec((1, tm, width), lambda i, j: (i, j, 0)),
            pl.BlockSpec((1, 8, width), lambda i, j: (i, jnp.maximum(j * r - 1, 0), 0)),
            pl.BlockSpec((1, 8, width), lambda i, j: (i, jnp.minimum((j + 1) * r, last), 0))]


def _shifted(t, prev_row, next_row):
    n = t.shape[0]
    rows = lax.broadcasted_iota(jnp.int32, t.shape, 0)
    t_prev = jnp.where(rows == 0, prev_row, pltpu.roll(t, 1, 0))
    t_next = jnp.where(rows == n - 1, next_row, pltpu.roll(t, n - 1, 0))
    return t_prev, t_next


def _conv_kernel(z_ref, zp_ref, zn_ref, cw_ref, g_ref, o_ref, *, nt):
    j = pl.program_id(1)
    w = GROUP_W
    z = z_ref[0]
    bgate = z[:, :w]
    u = z[:, w:2 * w] * z[:, 2 * w:]
    zp = zp_ref[0][7:8]
    zn = zn_ref[0][0:1]
    up = jnp.where(j > 0, zp[:, w:2 * w] * zp[:, 2 * w:], 0.0)
    un = jnp.where(j < nt - 1, zn[:, w:2 * w] * zn[:, 2 * w:], 0.0)
    u_prev, u_next = _shifted(u, up, un)
    cw = cw_ref[...]
    t = bgate * (cw[0:1] * u_prev + cw[1:2] * u + cw[2:3] * u_next)
    o_ref[0] = t * lax.rsqrt(jnp.mean(t * t, axis=-1, keepdims=True) + NORM_EPS) * g_ref[...]


def _conv_mixer(zc, conv_w, g_out):
    b, lq, width = zc.shape
    tm = min(4 * TOKEN_TILE, lq)
    return pl.pallas_call(
        functools.partial(_conv_kernel, nt=lq // tm),
        grid=(b, lq // tm),
        in_specs=_halo_specs(tm, lq, width) + [
            pl.BlockSpec((3, GROUP_W), lambda i, j: (0, 0)),
            pl.BlockSpec((1, GROUP_W), lambda i, j: (0, 0))],
        out_specs=pl.BlockSpec((1, tm, GROUP_W), lambda i, j: (i, j, 0)),
        out_shape=jax.ShapeDtypeStruct((b, lq, GROUP_W), F32),
        compiler_params=_cparams("parallel", "parallel"),
        name="conv_mixer",
    )(zc, zc, zc, conv_w, g_out.reshape(1, GROUP_W))


def _tile4(t):
    return jnp.concatenate([t, t, t, t], axis=-1)


def _slot_norm_rope(t, g, cos, sin):
    parts = []
    for h in range(N_HEADS):
        s = t[:, h * MLA_SLOT:(h + 1) * MLA_SLOT]
        ms = jnp.sum(s * s, axis=-1, keepdims=True) * (1.0 / MLA_QK)
        parts.append(s * lax.rsqrt(ms + NORM_EPS) * g)
    tn = jnp.concatenate(parts, axis=-1)
    n = tn.shape[-1]
    lane = lax.broadcasted_iota(jnp.int32, tn.shape, 1)
    half = MLA_ROPE // 4
    partner = jnp.where(lane % (2 * half) < half, pltpu.roll(tn, n - half, 1), pltpu.roll(tn, half, 1))
    return tn * _tile4(cos) + partner * _tile4(sin)


def _mla_prep_kernel(z_ref, cos_ref, sin_ref, gkv_ref, wukh_ref, wukl_ref, wuvh_ref, wuvl_ref, gq_ref, gk_ref,
                     q_ref, k_ref, v_ref):
    nq = N_HEADS * MLA_SLOT

    def rows(tok):
        z = z_ref[0, tok, :]
        q_in = z[:, :nq]
        k_rope = z[:, nq:nq + MLA_SLOT]
        ckv = z[:, nq + MLA_SLOT:]
        ckv = _split(ckv * lax.rsqrt(jnp.mean(ckv * ckv, axis=-1, keepdims=True) + NORM_EPS) * gkv_ref[...])
        k_in = _mm3(ckv, (wukh_ref[...], wukl_ref[...])) + _tile4(k_rope)
        v_ref[0, tok, :] = _mm3(ckv, (wuvh_ref[...], wuvl_ref[...])).astype(BF16)
        yield
        cos = cos_ref[tok, :]
        sin = sin_ref[tok, :]
        q_ref[0, tok, :] = (_slot_norm_rope(q_in, gq_ref[...], cos, sin) * (MLA_QK ** -0.5)).astype(BF16)
        yield
        k_ref[0, tok, :] = _slot_norm_rope(k_in, gk_ref[...], cos, sin).astype(BF16)

    _row_chains(z_ref.shape[1], rows)


def _mla_prep(zm, cos, sin, g_kv, wuk, wuv, g_q, g_k):
    b, lq, width = zm.shape
    tm = min(4 * TOKEN_TILE, lq)
    nq = N_HEADS * MLA_SLOT
    const = lambda shape: pl.BlockSpec(shape, lambda i, j: (0, 0))
    tok = lambda i, j: (i, j, 0)
    return pl.pallas_call(
        _mla_prep_kernel,
        grid=(b, lq // tm),
        in_specs=[pl.BlockSpec((1, tm, width), tok),
                  pl.BlockSpec((tm, MLA_SLOT), lambda i, j: (j, 0)),
                  pl.BlockSpec((tm, MLA_SLOT), lambda i, j: (j, 0)),
                  const((1, KV_RANK)), const(wuk[0].shape), const(wuk[0].shape),
                  const(wuv[0].shape), const(wuv[0].shape),
                  const((1, MLA_SLOT)), const((1, MLA_SLOT))],
        out_specs=[pl.BlockSpec((1, tm, nq), tok), pl.BlockSpec((1, tm, nq), tok), pl.BlockSpec((1, tm, GROUP_W), tok)],
        out_shape=[jax.ShapeDtypeStruct((b, lq, nq), BF16)] * 2 + [jax.ShapeDtypeStruct((b, lq, GROUP_W), BF16)],
        compiler_params=_cparams("parallel", "parallel"),
        name="mla_prep",
    )(zm, cos, sin, g_kv.reshape(1, KV_RANK), *wuk, *wuv, g_q, g_k)


def _attn_head(q, k, v):
    s = _dg(q, k, NT)
    yield
    p = jnp.exp(s - jnp.max(s, axis=-1, keepdims=True))
    inv = 1.0 / jnp.sum(p, axis=-1, keepdims=True)
    return _mm(p, v) * inv


def _attn_kernel(q_ref, k_ref, v_ref, g_ref, o_ref):
    v = v_ref[0]
    tq = q_ref.shape[1]
    head = _lane_head((tq, GROUP_W), HEAD_DIM)
    slots = [slice(h * MLA_SLOT, (h + 1) * MLA_SLOT) for h in range(N_HEADS)]
    outs = []
    for pair in (slots[:2], slots[2:]):
        outs += _interleave(*[_attn_head(q_ref[0, :, sl], k_ref[0, :, sl], v) for sl in pair])
    out = outs[0]
    for h in range(1, N_HEADS):
        out = jnp.where(head == h, outs[h], out)
    o_ref[0] = out * lax.rsqrt(jnp.mean(out * out, axis=-1, keepdims=True) + NORM_EPS) * g_ref[...]


def _attention(q, k, v, g_out):
    b, lq, nq = q.shape
    lk = k.shape[1]
    tq = min(2 * TOKEN_TILE, lq)
    return pl.pallas_call(
        _attn_kernel,
        grid=(b, lq // tq),
        in_specs=[pl.BlockSpec((1, tq, nq), lambda i, j: (i, j, 0)),
                  pl.BlockSpec((1, lk, nq), lambda i, j: (i, 0, 0)),
                  pl.BlockSpec((1, lk, GROUP_W), lambda i, j: (i, 0, 0)),
                  pl.BlockSpec((1, GROUP_W), lambda i, j: (0, 0))],
        out_specs=pl.BlockSpec((1, tq, GROUP_W), lambda i, j: (i, j, 0)),
        out_shape=jax.ShapeDtypeStruct((b, lq, GROUP_W), F32),
        compiler_params=_cparams("parallel", "arbitrary"),
        name="mla_attention",
    )(q, k, v, g_out.reshape(1, GROUP_W))


def _rwkv_proj_kernel(z_ref, zp_ref, zn_ref, mu_ref, kk_ref, ka_ref, w0_ref, a0_ref, lora_ref,
                      r_ref, v_ref, kkn_ref, gl_ref, ks_ref,
                      lw0_ref, b0_ref, kt0_ref, lw1_ref, b1_ref, kt1_ref, *, nt):
    j = pl.program_id(1)
    w = GROUP_W
    z = z_ref[0]
    zp = jnp.where(j > 0, zp_ref[0][7:8], 0.0)
    zn = jnp.where(j < nt - 1, zn_ref[0][0:1], 0.0)
    z_prev, z_next = _shifted(z, zp, zn)
    z_mix = z + mu_ref[...] * (0.5 * (z_prev + z_next) - z)

    def rows(tok):
        zs = z_mix[tok]
        r = zs[:, :w]
        k = zs[:, w:2 * w]
        lora_in = zs[:, 3 * w:3 * w + DECAY_LORA + ICLR_LORA]
        kk = k * kk_ref[...]
        kk = kk * lax.rsqrt(_head_sum(kk * kk) + 1e-12)
        lane = lax.broadcasted_iota(jnp.int32, lora_in.shape, 1)
        lora_in = jnp.where(lane < DECAY_LORA, jnp.tanh(lora_in), lora_in)
        r_ref[0, tok, :] = r
        v_ref[0, tok, :] = zs[:, 2 * w:3 * w]
        kkn_ref[0, tok, :] = kk
        gl_ref[0, tok, :] = zs[:, 3 * w + DECAY_LORA + ICLR_LORA:]
        ksum = jnp.zeros_like(k)
        for d, (lw_ref, b_ref, kt_ref) in enumerate(((lw0_ref, b0_ref, kt0_ref), (lw1_ref, b1_ref, kt1_ref))):
            lo = _mm3(lora_in, lora_ref[d])
            yield
            t = -(w0_ref[d:d + 1] + lo[:, :w])
            softplus = jnp.maximum(t, 0.0) + jnp.log1p(jnp.exp(-jnp.abs(t)))
            lw_ref[0, tok, :] = -jnp.exp(-softplus - 0.5)
            a = _sigmoid(a0_ref[d:d + 1] + lo[:, w:])
            kt = k * (1.0 + (a - 1.0) * ka_ref[...])
            b_ref[0, tok, :] = kk * a
            kt_ref[0, tok, :] = kt
            ksum = ksum + kt
        ks_ref[0, tok, :] = ksum

    _row_chains(z.shape[0], rows)


def _rwkv_proj(zr, mu, k_k, k_a, w0, a0, lora):
    b, lq, width = zr.shape
    tm = min(4 * TOKEN_TILE, lq)
    w = GROUP_W
    const2 = lambda shape: pl.BlockSpec(shape, lambda i, j: (0,) * len(shape))
    out_w = (w, w, w, GATE_LORA, w, w, w, w, w, w, w)
    return pl.pallas_call(
        functools.partial(_rwkv_proj_kernel, nt=lq // tm),
        grid=(b, lq // tm),
        in_specs=_halo_specs(tm, lq, width) + [
            const2((1, width)), const2((1, w)), const2((1, w)), const2((2, w)), const2((2, w)),
            const2(lora.shape)],
        out_specs=[pl.BlockSpec((1, tm, n), lambda i, j: (i, j, 0)) for n in out_w],
        out_shape=[jax.ShapeDtypeStruct((b, lq, n), F32) for n in out_w],
        compiler_params=_cparams("parallel", "parallel"),
        name="rwkv_proj",
    )(zr, zr, zr, mu.reshape(1, width), k_k.reshape(1, w), k_a.reshape(1, w), w0, a0, lora)


def _stack_heads(t):
    head = _lane_head(t.shape, HEAD_DIM)
    return jnp.concatenate([jnp.where(head == h, t, 0.0) for h in range(N_HEADS)], axis=0)


def _unit_lower_inverse(n_mat, p_i, q_i, size):
    base = RWKV_INV_BASE
    same = lambda w: (p_i // w) == (q_i // w)
    m = jnp.where(same(base), -n_mat, 0.0)
    inv = jnp.where(p_i == q_i, 1.0, 0.0) + m
    pw = m
    span = 1
    while 2 * span < base:
        pw = _mm(pw, pw)
        yield
        inv = inv + _mm(inv, pw)
        yield
        span *= 2
    w = base
    while w < size:
        off = jnp.where(same(2 * w) & jnp.logical_not(same(w)), n_mat, 0.0)
        invb = inv.astype(BF16)
        left = _mm(invb, off)
        yield
        inv = inv - _mm(left, invb)
        yield
        w *= 2
    return inv


def _rwkv_chunk(lw, kk, b, kt, v, r, s, reverse):
    c = RWKV_CHUNK
    n = N_HEADS * c
    t_i = lax.broadcasted_iota(jnp.int32, (c, c), 0)
    s_i = lax.broadcasted_iota(jnp.int32, (c, c), 1)
    seen = (s_i >= t_i) if reverse else (s_i <= t_i)
    g = _cumsum_rows(jnp.where(seen, 1.0, 0.0).astype(BF16), lw)
    g_tot = g[0:1] if reverse else g[c - 1:c]
    e_neg = jnp.exp(-g)
    e_tot = jnp.exp(g_tot)
    bd = _stack_heads(b * e_neg)
    ktd = _stack_heads(kt * e_neg)
    vs = _stack_heads(v).astype(BF16)
    lhs = jnp.concatenate([_stack_heads(kk * jnp.exp(g - lw)), _stack_heads(r * jnp.exp(g))], axis=0).astype(BF16)
    gram = _mm(lhs, jnp.concatenate([bd, ktd], axis=0), NT)
    from_state = _mm(lhs, s, NT)
    yield

    p_i = lax.broadcasted_iota(jnp.int32, (n, n), 0)
    q_i = lax.broadcasted_iota(jnp.int32, (n, n), 1)
    strict = (q_i > p_i) if reverse else (q_i < p_i)
    incl = (q_i >= p_i) if reverse else (q_i <= p_i)
    n_ab = jnp.where(strict, gram[:n, :n], 0.0)
    n_ak = jnp.where(strict, gram[:n, n:], 0.0)
    a_rb = jnp.where(incl, gram[n:, :n], 0.0)
    a_rk = jnp.where(incl, gram[n:, n:], 0.0)
    from_v = _mm(jnp.concatenate([n_ak, a_rk], axis=0), vs)
    t_inv = yield from _unit_lower_inverse(n_ab, p_i, q_i, c)
    u = (-_mm(t_inv, from_state[:n] + from_v[:n])).astype(BF16)
    yield
    y = from_state[n:] + _mm(a_rb, u) + from_v[n:]
    y = y[0:c] + y[c:2 * c] + y[2 * c:3 * c] + y[3 * c:]
    s_new = s * e_tot + _mm(jnp.concatenate([u, vs], axis=0),
                            jnp.concatenate([bd * e_tot, ktd * e_tot], axis=0), TN)
    return y, s_new


def _rwkv_scan_kernel(lwf_ref, kkf_ref, bf_ref, ktf_ref, vf_ref, rf_ref,
                      lwb_ref, kkb_ref, bb_ref, ktb_ref, vb_ref, rb_ref, sf0_ref, sb0_ref,
                      yf_ref, yb_ref, sfo_ref, sbo_ref, sf_ref, sb_ref, *, nc):
    i = pl.program_id(1)

    @pl.when(i == 0)
    def _():
        sf_ref[...] = sf0_ref[...]
        sb_ref[...] = sb0_ref[...]

    chains = []
    for n in range(RWKV_BATCH):
        chains.append(_rwkv_chunk(lwf_ref[n], kkf_ref[n], bf_ref[n], ktf_ref[n], vf_ref[n], rf_ref[n], sf_ref[n], False))
        chains.append(_rwkv_chunk(lwb_ref[n], kkb_ref[n], bb_ref[n], ktb_ref[n], vb_ref[n], rb_ref[n], sb_ref[n], True))
    results = _interleave(*chains)
    for n in range(RWKV_BATCH):
        (yf, sf), (yb, sb) = results[2 * n], results[2 * n + 1]
        yf_ref[n] = yf
        yb_ref[n] = yb
        sf_ref[n] = sf
        sb_ref[n] = sb

    @pl.when(i == nc - 1)
    def _():
        sfo_ref[...] = sf_ref[...]
        sbo_ref[...] = sb_ref[...]


def _rwkv_scan(fw, bw, kk, v, r, s_fw, s_bw):
    b, lq, w = kk.shape
    c = RWKV_CHUNK
    nc = lq // c
    nb = RWKV_BATCH
    fspec = pl.BlockSpec((nb, c, w), lambda i, j: (i, j, 0))
    bspec = pl.BlockSpec((nb, c, w), lambda i, j: (i, nc - 1 - j, 0))
    sspec = pl.BlockSpec((nb, w, w), lambda i, j: (i, 0, 0))
    yshape = jax.ShapeDtypeStruct((b, lq, w), F32)
    sshape = jax.ShapeDtypeStruct((b, w, w), F32)
    return pl.pallas_call(
        functools.partial(_rwkv_scan_kernel, nc=nc),
        grid=(b // nb, nc),
        in_specs=[fspec] * 6 + [bspec] * 6 + [sspec, sspec],
        out_specs=[fspec, bspec, sspec, sspec],
        out_shape=[yshape, yshape, sshape, sshape],
        scratch_shapes=[pltpu.VMEM((nb, w, w), F32), pltpu.VMEM((nb, w, w), F32)],
        compiler_params=_cparams("parallel", "arbitrary"),
        name="rwkv_scan",
    )(fw[0], kk, fw[1], fw[2], v, r, bw[0], kk, bw[1], bw[2], v, r, s_fw, s_bw)


def _rwkv_out_kernel(yf_ref, yb_ref, r_ref, v_ref, ks_ref, gl_ref, g2h_ref, g2l_ref, rk_ref, gln_ref, o_ref):
    def rows(tok):
        y = yf_ref[0, tok, :] + yb_ref[0, tok, :]
        bonus = _head_sum(r_ref[0, tok, :] * ks_ref[0, tok, :] * rk_ref[...])
        gate = _mm3(_sigmoid(gl_ref[0, tok, :]), (g2h_ref[...], g2l_ref[...]))
        dlt = y - _head_sum(y, 1.0 / HEAD_DIM)
        yield
        var = _head_sum(dlt * dlt, 1.0 / HEAD_DIM)
        yield
        yn = dlt * lax.rsqrt(var + RWKV_GN_EPS) * gln_ref[...] + bonus * v_ref[0, tok, :]
        o_ref[0, tok, :] = yn * gate

    _row_chains(yf_ref.shape[1], rows)


def _rwkv_out(yf, yb, r, v, ks, gl, g2, r_k, g_ln):
    b, lq, w = yf.shape
    tm = min(4 * TOKEN_TILE, lq)
    tspec = pl.BlockSpec((1, tm, w), lambda i, j: (i, j, 0))
    const = lambda shape: pl.BlockSpec(shape, lambda i, j: (0, 0))
    return pl.pallas_call(
        _rwkv_out_kernel,
        grid=(b, lq // tm),
        in_specs=[tspec] * 5 + [pl.BlockSpec((1, tm, GATE_LORA), lambda i, j: (i, j, 0)),
                                const(g2[0].shape), const(g2[0].shape), const((1, w)), const((1, w))],
        out_specs=tspec,
        out_shape=jax.ShapeDtypeStruct((b, lq, w), F32),
        compiler_params=_cparams("parallel", "parallel"),
        name="rwkv_out",
    )(yf, yb, r, v, ks, gl, *g2, r_k.reshape(1, w), g_ln.reshape(1, w))


def _half_rotate(t, cos, sin):
    n = t.shape[-1]
    lane = lax.broadcasted_iota(jnp.int32, t.shape, 1)
    half = HEAD_DIM // 2
    partner = jnp.where(lane % HEAD_DIM < half, pltpu.roll(t, n - half, 1), pltpu.roll(t, half, 1))
    return t * cos + partner * sin


def _ret_chunk(z, cos, sin, xi, zeta, dmat_ref, dec, st):
    w = GROUP_W
    q = _half_rotate(z[:, :w] * (HEAD_DIM ** -0.5), cos, sin)
    k = _half_rotate(z[:, w:2 * w], cos, sin)
    vs = z[:, 2 * w:3 * w].astype(BF16)
    ks = k.astype(BF16)
    head = _lane_head(q.shape, HEAD_DIM)
    y = _mm(q * xi, st)
    upd = _mm(k * zeta, vs, TN)
    scores = [_mm(jnp.where(head == h, q, 0.0), ks, NT) * dmat_ref[h] for h in range(N_HEADS)]
    yield
    for h in range(N_HEADS):
        y = y + jnp.where(head == h, _mm(scores[h], vs), 0.0)
    return y, st * dec + upd * _head_ones(w, HEAD_DIM)


def _ret_scan_kernel(zf_ref, cosf_ref, sinf_ref, zb_ref, cosb_ref, sinb_ref,
                     xif_ref, zetaf_ref, dmatf_ref, xib_ref, zetab_ref, dmatb_ref, dec_ref, rf0_ref, rb0_ref,
                     yf_ref, yb_ref, rfo_ref, rbo_ref, stf_ref, stb_ref, *, nc):
    i = pl.program_id(1)

    @pl.when(i == 0)
    def _():
        stf_ref[...] = rf0_ref[...]
        stb_ref[...] = rb0_ref[...]

    dec = dec_ref[...]
    chains = []
    for n in range(RET_BATCH):
        chains.append(_ret_chunk(zf_ref[n], cosf_ref[...], sinf_ref[...], xif_ref[...], zetaf_ref[...], dmatf_ref,
                                 dec, stf_ref[n]))
        chains.append(_ret_chunk(zb_ref[n], cosb_ref[...], sinb_ref[...], xib_ref[...], zetab_ref[...], dmatb_ref,
                                 dec, stb_ref[n]))
    results = _interleave(*chains)
    for n in range(RET_BATCH):
        (yf, stf), (yb, stb) = results[2 * n], results[2 * n + 1]
        yf_ref[n] = yf
        yb_ref[n] = yb
        stf_ref[n] = stf
        stb_ref[n] = stb

    @pl.when(i == nc - 1)
    def _():
        rfo_ref[...] = stf_ref[...]
        rbo_ref[...] = stb_ref[...]


def _ret_scan(zt, cos, sin, r_fw, r_bw):
    b, lq, width = zt.shape
    c = min(RET_CHUNK, lq)
    nc = lq // c
    w = GROUP_W
    xi_f, zeta_f, dmat_f, dec = _ret_decay_tables(c, False)
    xi_b, zeta_b, dmat_b, _ = _ret_decay_tables(c, True)
    fw3 = lambda i, j: (i, j, 0)
    bw3 = lambda i, j: (i, nc - 1 - j, 0)
    fw2 = lambda i, j: (j, 0)
    bw2 = lambda i, j: (nc - 1 - j, 0)
    nb = RET_BATCH
    const = lambda shape: pl.BlockSpec(shape, lambda i, j: (0,) * len(shape))
    sspec = pl.BlockSpec((nb, w, w), lambda i, j: (i, 0, 0))
    tables = [const((c, w)), const((c, w)), const((N_HEADS, c, c))]
    yshape = jax.ShapeDtypeStruct((b, lq, w), F32)
    sshape = jax.ShapeDtypeStruct((b, w, w), F32)
    return pl.pallas_call(
        functools.partial(_ret_scan_kernel, nc=nc),
        grid=(b // nb, nc),
        in_specs=[pl.BlockSpec((nb, c, width), fw3), pl.BlockSpec((c, w), fw2), pl.BlockSpec((c, w), fw2),
                  pl.BlockSpec((nb, c, width), bw3), pl.BlockSpec((c, w), bw2), pl.BlockSpec((c, w), bw2)]
        + tables + tables + [const((1, w)), sspec, sspec],
        out_specs=[pl.BlockSpec((nb, c, w), fw3), pl.BlockSpec((nb, c, w), bw3), sspec, sspec],
        out_shape=[yshape, yshape, sshape, sshape],
        scratch_shapes=[pltpu.VMEM((nb, w, w), F32), pltpu.VMEM((nb, w, w), F32)],
        compiler_params=_cparams("parallel", "arbitrary"),
        name="ret_scan",
    )(zt, cos, sin, zt, cos, sin, xi_f, zeta_f, dmat_f, xi_b, zeta_b, dmat_b, dec, r_fw, r_bw)


def _ret_out_kernel(yf_ref, yb_ref, z_ref, g_ref, o_ref):
    def rows(tok):
        y = yf_ref[0, tok, :] + yb_ref[0, tok, :]
        ms = _head_sum(y * y, 1.0 / HEAD_DIM)
        yield
        gate = z_ref[0, tok, :]
        o_ref[0, tok, :] = gate * _sigmoid(gate) * (y * lax.rsqrt(ms + NORM_EPS) * g_ref[...])

    _row_chains(yf_ref.shape[1], rows)


def _ret_out(yf, yb, zt, g_norm):
    b, lq, w = yf.shape
    tm = min(4 * TOKEN_TILE, lq)
    tspec = pl.BlockSpec((1, tm, w), lambda i, j: (i, j, 0))
    return pl.pallas_call(
        _ret_out_kernel,
        grid=(b, lq // tm),
        in_specs=[tspec, tspec, pl.BlockSpec((1, tm, w), lambda i, j: (i, j, 3)),
                  pl.BlockSpec((1, w), lambda i, j: (0, 0))],
        out_specs=tspec,
        out_shape=jax.ShapeDtypeStruct((b, lq, w), F32),
        compiler_params=_cparams("parallel", "parallel"),
        name="ret_out",
    )(yf, yb, zt, g_norm.reshape(1, w))


def _outproj_rows(tok, y_refs, x_ref, m, w_ref, g_ref, wr_ref, br_ref, xo_ref, h_ref, lg_ref):
    w = GROUP_W
    acc = None
    for n, ref in enumerate(y_refs):
        part = _mm(ref[0, tok, :], w_ref[n * w:(n + 1) * w, :])
        acc = part if acc is None else acc + part
    yield
    x = x_ref[0, tok, :] + m[2:3] * acc
    xo_ref[0, tok, :] = x
    xn = x * lax.rsqrt(jnp.mean(x * x, axis=-1, keepdims=True) + NORM_EPS) * g_ref[...]
    h = xn * (1.0 + m[4:5]) + m[3:4]
    h_ref[0, tok, :] = h.astype(BF16)
    yield
    lg_ref[0, tok, :] = _mm3(h, wr_ref[...]) + br_ref[...]


def _outproj_kernel(yc_ref, ym_ref, yr_ref, yt_ref, x_ref, mod_ref, w_ref, g_ref, wr_ref, br_ref,
                    xo_ref, h_ref, lg_ref):
    tm = x_ref.shape[1]
    sub = min(TOKEN_TILE, tm)
    m = mod_ref[0]
    _interleave(*[_outproj_rows(slice(t * sub, (t + 1) * sub), (yc_ref, ym_ref, yr_ref, yt_ref), x_ref, m,
                                w_ref, g_ref, wr_ref, br_ref, xo_ref, h_ref, lg_ref) for t in range(tm // sub)])


def _outproj(ys, x, mod, w_out, g2, w_r, b_r):
    b, lq, d = x.shape
    tm = min(2 * TOKEN_TILE, lq)
    yspec = pl.BlockSpec((1, tm, GROUP_W), lambda i, j: (i, j, 0))
    xspec = pl.BlockSpec((1, tm, d), lambda i, j: (i, j, 0))
    const = lambda shape: pl.BlockSpec(shape, lambda i, j: (0, 0))
    return pl.pallas_call(
        _outproj_kernel,
        grid=(b, lq // tm),
        in_specs=[yspec] * 4 + [xspec, pl.BlockSpec((1, 6, d), lambda i, j: (i, 0, 0)),
                                const(w_out.shape), const((1, d)),
                                const(w_r.shape), const((1, N_EXP))],
        out_specs=[xspec, xspec, pl.BlockSpec((1, tm, N_EXP), lambda i, j: (i, j, 0))],
        out_shape=[jax.ShapeDtypeStruct((b, lq, d), F32), jax.ShapeDtypeStruct((b, lq, d), BF16),
                   jax.ShapeDtypeStruct((b, lq, N_EXP), F32)],
        compiler_params=_cparams("parallel", "parallel"),
        name="outproj",
    )(*ys, x, mod, w_out, g2.reshape(1, d), w_r, b_r.reshape(1, N_EXP))


def _lane_cumsum(m):
    n = m.shape[-1]
    blk = 128
    r_i = lax.broadcasted_iota(jnp.int32, (blk, blk), 0)
    c_i = lax.broadcasted_iota(jnp.int32, (blk, blk), 1)
    tri = jnp.where(r_i <= c_i, 1.0, 0.0).astype(BF16)
    run = jnp.zeros((m.shape[0], 1), F32)
    parts = []
    for j in range(n // blk):
        cs = _dg(m[:, j * blk:(j + 1) * blk].astype(BF16), tri) + run
        parts.append(cs)
        run = cs[:, blk - 1:blk]
    return jnp.concatenate(parts, axis=-1)


def _route_kernel(lg_ref, rank_ref, gate_ref, *, cap):
    lg = lg_ref[0]
    e = jnp.exp(lg - jnp.max(lg, axis=0, keepdims=True))
    aff = e / jnp.sum(e, axis=0, keepdims=True)
    bits = pltpu.bitcast(aff, jnp.int32)
    thr = jnp.zeros((aff.shape[0], 1), jnp.int32)
    for bit in range(30, -1, -1):
        cand = thr | (1 << bit)
        cnt = jnp.sum(jnp.where(bits >= cand, 1.0, 0.0), axis=-1, keepdims=True)
        thr = jnp.where(cnt >= cap, cand, thr)
    above = bits > thr
    tied = bits == thr
    n_above = jnp.sum(jnp.where(above, 1.0, 0.0), axis=-1, keepdims=True)
    tied_f = jnp.where(tied, 1.0, 0.0)
    tied_rank = _lane_cumsum(tied_f) - tied_f
    sel = above | (tied & (tied_rank < cap - n_above))
    sel_f = jnp.where(sel, 1.0, 0.0)
    rank = _lane_cumsum(sel_f) - sel_f
    rank_ref[0] = jnp.where(sel, rank, -1.0).astype(jnp.int32)
    gate_ref[0] = jnp.where(sel, aff, 0.0)


def _route(logits_t, cap):
    b, ne, lq = logits_t.shape
    spec = pl.BlockSpec((1, ne, lq), lambda i: (i, 0, 0))
    return pl.pallas_call(
        functools.partial(_route_kernel, cap=cap),
        grid=(b,),
        in_specs=[spec],
        out_specs=[spec, spec],
        out_shape=[jax.ShapeDtypeStruct((b, ne, lq), jnp.int32), jax.ShapeDtypeStruct((b, ne, lq), F32)],
        compiler_params=_cparams("parallel"),
        name="moe_route",
    )(logits_t)


def _onehot_rows(rank_row, cap):
    r_i = lax.broadcasted_iota(jnp.int32, (cap, rank_row.shape[-1]), 0)
    return r_i == rank_row


def _gather_kernel(h_ref, rank_ref, gate_ref, xe_ref, gr_ref, *, cap):
    lq = h_ref.shape[1]
    ts = min(MOE_TOKEN_SLICE, lq)
    xe = gr = None
    for t in range(lq // ts):
        tok = slice(t * ts, (t + 1) * ts)
        sel = _onehot_rows(rank_ref[0, 0, :, tok], cap)
        px = _dg(jnp.where(sel, 1.0, 0.0).astype(BF16), h_ref[0, tok, :])
        pg = jnp.sum(jnp.where(sel, gate_ref[0, 0, :, tok], 0.0), axis=-1, keepdims=True)
        xe, gr = (px, pg) if xe is None else (xe + px, gr + pg)
    xe_ref[0, 0] = xe.astype(BF16)
    gr_ref[0, 0] = gr


def _gather(h, rank, gate, cap):
    b, lq, d = h.shape
    ne = rank.shape[1]
    rspec = pl.BlockSpec((1, 1, 1, lq), lambda i, e: (i, e, 0, 0))
    xspec = pl.BlockSpec((1, 1, cap, d), lambda i, e: (i, e, 0, 0))
    return pl.pallas_call(
        functools.partial(_gather_kernel, cap=cap),
        grid=(b, ne),
        in_specs=[pl.BlockSpec((1, lq, d), lambda i, e: (i, 0, 0)), rspec, rspec],
        out_specs=[xspec, pl.BlockSpec((1, 1, cap, 1), lambda i, e: (i, e, 0, 0))],
        out_shape=[jax.ShapeDtypeStruct((b, ne, cap, d), BF16), jax.ShapeDtypeStruct((b, ne, cap, 1), F32)],
        compiler_params=_cparams("parallel", "arbitrary"),
        name="moe_gather",
    )(h, rank.reshape(b, ne, 1, lq), gate.reshape(b, ne, 1, lq))


def _ffn_kernel(xe_ref, gr_ref, wg_ref, wu_ref, wd_ref, ye_ref, wgb_ref, wub_ref, wdb_ref):
    @pl.when(pl.program_id(1) == 0)
    def _():
        wgb_ref[...] = wg_ref[0, 0].astype(BF16)
        wub_ref[...] = wu_ref[0, 0].astype(BF16)
        wdb_ref[...] = wd_ref[0, 0].astype(BF16)

    nb, _, cap, d = xe_ref.shape
    xe = xe_ref[...].reshape(nb * cap, d)
    a = _dg(xe, wgb_ref[...])
    hid = a * _sigmoid(a) * _dg(xe, wub_ref[...])
    ye = _mm(hid, wdb_ref[...]) * gr_ref[...].reshape(nb * cap, 1)
    ye_ref[...] = ye.astype(BF16).reshape(nb, 1, cap, d)


def _expert_ffn(xe, gr, w_gate, w_up, w_down, layer):
    b, ne, cap, d = xe.shape
    ff = w_gate.shape[3]
    nb = FFN_BATCH
    xspec = pl.BlockSpec((nb, 1, cap, d), lambda e, i: (i, e, 0, 0))
    wspec = lambda shape: pl.BlockSpec((1, 1) + shape, lambda e, i: (layer, e, 0, 0))
    return pl.pallas_call(
        _ffn_kernel,
        grid=(ne, b // nb),
        in_specs=[xspec, pl.BlockSpec((nb, 1, cap, 1), lambda e, i: (i, e, 0, 0)),
                  wspec((d, ff)), wspec((d, ff)), wspec((ff, d))],
        out_specs=xspec,
        out_shape=jax.ShapeDtypeStruct((b, ne, cap, d), BF16),
        scratch_shapes=[pltpu.VMEM((d, ff), BF16), pltpu.VMEM((d, ff), BF16), pltpu.VMEM((ff, d), BF16)],
        compiler_params=_cparams("parallel", "arbitrary"),
        name="moe_ffn",
    )(xe, gr, w_gate, w_up, w_down)


def _combine_kernel(x_ref, ga_ref, rank_ref, ye_ref, o_ref, *, cap):
    g = pl.program_id(2)
    lq = x_ref.shape[1]
    ts = min(COMBINE_TOKEN_SLICE, lq)
    ye = ye_ref[0].reshape(COMBINE_EXPERTS * cap, ye_ref.shape[-1])
    for t in range(lq // ts):
        tok = slice(t * ts, (t + 1) * ts)
        p = jnp.concatenate([jnp.where(_onehot_rows(rank_ref[0, e, :, tok], cap), 1.0, 0.0).astype(BF16)
                             for e in range(COMBINE_EXPERTS)], axis=0)
        part = _dg(p, ye, TN)

        @pl.when(g == 0)
        def _():
            o_ref[0, tok, :] = part

        @pl.when(g > 0)
        def _():
            o_ref[0, tok, :] += part

    @pl.when(g == N_EXP // COMBINE_EXPERTS - 1)
    def _():
        o_ref[0] = x_ref[0] + ga_ref[0] * o_ref[0]


def _combine(x, mod, rank, ye, cap):
    b, lq, d = x.shape
    ne = rank.shape[1]
    lt = lq // 2 if lq >= 2 * COMBINE_TOKEN_SLICE else lq
    xspec = pl.BlockSpec((1, lt, d), lambda i, j, e: (i, j, 0))
    yspec = pl.BlockSpec((1, COMBINE_EXPERTS, cap, d), lambda i, j, e: (i, e, 0, 0))
    return pl.pallas_call(
        functools.partial(_combine_kernel, cap=cap),
        grid=(b, lq // lt, ne // COMBINE_EXPERTS),
        in_specs=[xspec, pl.BlockSpec((1, 1, d), lambda i, j, e: (i, 0, 0)),
                  pl.BlockSpec((1, COMBINE_EXPERTS, 1, lt), lambda i, j, e: (i, e, 0, j)), yspec],
        out_specs=xspec,
        out_shape=jax.ShapeDtypeStruct((b, lq, d), F32),
        compiler_params=_cparams("parallel", "parallel", "arbitrary"),
        name="moe_combine",
    )(x, mod[:, 5:6, :], rank.reshape(b, ne, 1, lq), ye)


def _split_w(w):
    hi = w.astype(BF16)
    return hi, (w - hi.astype(F32)).astype(BF16)


def _layout_w_in(w):
    d = w.shape[0]
    o = COLS_CONV
    q_nope = w[:, o:o + N_HEADS * MLA_NOPE]
    q_rope = w[:, o + N_HEADS * MLA_NOPE:o + N_HEADS * MLA_QK]
    ckv = w[:, o + N_HEADS * MLA_QK:o + N_HEADS * MLA_QK + KV_RANK]
    k_rope = w[:, o + N_HEADS * MLA_QK + KV_RANK:o + COLS_MLA_IN]
    pad = jnp.zeros((d, MLA_SLOT - MLA_QK), w.dtype)
    cols = []
    for h in range(N_HEADS):
        cols += [q_nope[:, h * MLA_NOPE:(h + 1) * MLA_NOPE], q_rope[:, h * MLA_ROPE:(h + 1) * MLA_ROPE], pad]
    cols += [jnp.zeros((d, MLA_NOPE), w.dtype), k_rope, pad, ckv]
    mla = jnp.concatenate(cols, axis=1)
    return jnp.concatenate([w[:, :o].astype(BF16), mla.astype(BF16), w[:, o + COLS_MLA_IN:].astype(BF16)], axis=1)


def _layout_w_uk(w_uk):
    pad = jnp.zeros((w_uk.shape[0], MLA_SLOT - MLA_NOPE), w_uk.dtype)
    cols = []
    for h in range(N_HEADS):
        cols += [w_uk[:, h * MLA_NOPE:(h + 1) * MLA_NOPE], pad]
    return _split_w(jnp.concatenate(cols, axis=1))


def _slot_gain(g):
    return jnp.concatenate([g, jnp.zeros((MLA_SLOT - MLA_QK,), g.dtype)]).reshape(1, MLA_SLOT)


def _layout_lora(w2, a2):
    z = jnp.zeros_like(w2)
    return jnp.concatenate([jnp.concatenate([w2, z], axis=2), jnp.concatenate([z, a2], axis=2)], axis=1)


def _mla_tables(lq):
    rows = lq // GRID_W
    row = jnp.repeat(jnp.arange(rows, dtype=F32), GRID_W)
    col = jnp.tile(jnp.arange(GRID_W, dtype=F32), rows)
    axis_dim = MLA_ROPE // 2
    inv_axis = ROPE_BASE ** (-jnp.arange(0, axis_dim, 2, dtype=F32) / axis_dim)
    ar = row[:, None] * inv_axis[None, :]
    ac = col[:, None] * inv_axis[None, :]
    ones = jnp.ones((lq, MLA_NOPE), F32)
    tail = jnp.ones((lq, MLA_SLOT - MLA_QK), F32)
    cos = jnp.concatenate([ones, jnp.cos(ar), jnp.cos(ar), jnp.cos(ac), jnp.cos(ac), tail], axis=1)
    sin = jnp.concatenate([0 * ones, -jnp.sin(ar), jnp.sin(ar), -jnp.sin(ac), jnp.sin(ac), 0 * tail], axis=1)
    return cos, sin


def _ret_tables(lq):
    theta = 1.0 / (RET_THETA_BASE ** jnp.linspace(0.0, 1.0, HEAD_DIM // 2, dtype=F32))
    ang = jnp.arange(lq, dtype=F32)[:, None] * theta[None, :]
    cos = jnp.tile(jnp.cos(ang), (1, 2 * N_HEADS))
    sin = jnp.tile(jnp.concatenate([-jnp.sin(ang), jnp.sin(ang)], axis=1), (1, N_HEADS))
    return cos, sin


def _ret_decay_tables(c, reverse):
    log_gamma = jnp.log1p(-jnp.exp2(-5.0 - jnp.arange(N_HEADS, dtype=F32)))
    lanes = jnp.repeat(log_gamma, HEAD_DIM)[None, :]
    j = jnp.arange(c, dtype=F32)
    if reverse:
        xi = jnp.exp((c - j)[:, None] * lanes)
        zeta = jnp.exp(j[:, None] * lanes)
        diff = j[None, :] - j[:, None]
        mask = diff > 0
    else:
        xi = jnp.exp((j + 1.0)[:, None] * lanes)
        zeta = jnp.exp((c - 1.0 - j)[:, None] * lanes)
        diff = j[:, None] - j[None, :]
        mask = diff >= 0
    dmat = jnp.where(mask[None], jnp.exp(jnp.where(mask, diff, 0.0)[None] * log_gamma[:, None, None]), 0.0)
    dec = jnp.exp(c * lanes)
    return xi, zeta, dmat, dec


def _mixers_pre(x, mod, lw, tables):
    zc, zm, zr, zt = _inproj(x, mod, lw['g_norm1'], lw['w_in'])
    y_conv = _conv_mixer(zc, lw['conv_w'], lw['g_out_conv'])
    q, k, v = _mla_prep(zm, tables['mla_cos'], tables['mla_sin'], lw['g_kv_norm'], lw['w_uk'],
                        lw['w_uv'], lw['g_q'], lw['g_k'])
    rw = _rwkv_proj(zr, lw['rwkv_mu'], lw['rwkv_k_k'], lw['rwkv_k_a'], lw['rwkv_w0'], lw['rwkv_a0'], lw['lora'])
    return dict(y_conv=y_conv, q=q, k=k, v=v, rw=rw, zt=zt)


def _rwkv_stream(rw, s_fw, s_bw, lw, emit):
    r, v, kk, gl, ks, lw0, b0, kt0, lw1, b1, kt1 = rw
    yf, yb, s_fw, s_bw = _rwkv_scan((lw0, b0, kt0), (lw1, b1, kt1), kk, v, r, s_fw, s_bw)
    y = _rwkv_out(yf, yb, r, v, ks, gl, lw['rwkv_g2'], lw['rwkv_r_k'], lw['g_ln_x']) if emit else None
    return y, s_fw, s_bw


def _ret_stream(zt, r_fw, r_bw, lw, tables, emit):
    yf, yb, r_fw, r_bw = _ret_scan(zt, tables['ret_cos'], tables['ret_sin'], r_fw, r_bw)
    y = _ret_out(yf, yb, zt, lw['g_ret_norm']) if emit else None
    return y, r_fw, r_bw


def _channel_mix(ys, x, mod, lw):
    x, h, logits = _outproj(ys, x, mod, lw['w_out'], lw['g_norm2'], lw['w_router'], lw['b_router'])
    cap = CAP_FACTOR * x.shape[1] // N_EXP
    rank, gate = _route(jnp.swapaxes(logits, 1, 2), cap)
    xe, gr = _gather(h, rank, gate, cap)
    ye = _expert_ffn(xe, gr, lw['w_gate'], lw['w_up'], lw['w_down'], lw['layer'])
    return _combine(x, mod, rank, ye, cap)


def kernel(x, c, ctx, c_ctx, w_ada, b_ada, g_norm1, g_norm2, w_in, w_out, conv_w, g_out_conv, g_kv_norm, w_uk, w_uv, g_q_norm, g_k_norm, g_out_mla, rwkv_mu, rwkv_w0, rwkv_w2, rwkv_a0, rwkv_a2, rwkv_g2, rwkv_k_k, rwkv_k_a, rwkv_r_k, g_ln_x, g_ret_norm, w_router, b_router, w_gate, w_up, w_down):
    depth = w_in.shape[0]
    bsz, lq, d = x.shape
    lc = ctx.shape[1]

    cc = jnp.zeros((16, d), F32).at[:bsz].set(c).at[bsz].set(c_ctx)
    mods = _mods(cc, w_ada, b_ada)

    cos_x, sin_x = _mla_tables(lq)
    rcos_x, rsin_x = _ret_tables(lq)
    tab_x = dict(mla_cos=cos_x, mla_sin=sin_x, ret_cos=rcos_x, ret_sin=rsin_x)
    tab_c = dict(mla_cos=jnp.ones((lc, MLA_SLOT), F32), mla_sin=jnp.zeros((lc, MLA_SLOT), F32),
                 ret_cos=jnp.ones((lc, GROUP_W), F32), ret_sin=jnp.zeros((lc, GROUP_W), F32))
    zero_state = jnp.zeros((bsz, GROUP_W, GROUP_W), F32)

    xc = ctx
    for l in range(depth):
        need_ctx = l < depth - 1
        lw = dict(
            g_norm1=g_norm1[l], g_norm2=g_norm2[l], w_in=_layout_w_in(w_in[l]), w_out=w_out[l].astype(BF16),
            conv_w=conv_w[l], g_out_conv=g_out_conv[l], g_kv_norm=g_kv_norm[l],
            w_uk=_layout_w_uk(w_uk[l]), w_uv=_split_w(w_uv[l]),
            g_q=_slot_gain(g_q_norm[l]), g_k=_slot_gain(g_k_norm[l]), g_out_mla=g_out_mla[l],
            rwkv_mu=rwkv_mu[l], rwkv_w0=rwkv_w0[l], rwkv_a0=rwkv_a0[l],
            lora=_layout_lora(rwkv_w2[l], rwkv_a2[l]), rwkv_g2=_split_w(rwkv_g2[l]),
            rwkv_k_k=rwkv_k_k[l], rwkv_k_a=rwkv_k_a[l], rwkv_r_k=rwkv_r_k[l], g_ln_x=g_ln_x[l],
            g_ret_norm=g_ret_norm[l], w_router=w_router[l], b_router=b_router[l],
            w_gate=w_gate, w_up=w_up, w_down=w_down, layer=l)
        mod_x = mods[l, :bsz].reshape(bsz, 6, d)
        mod_c = jnp.broadcast_to(mods[l, bsz].reshape(1, 6, d), (bsz, 6, d))

        pc = _mixers_pre(xc, mod_c, lw, tab_c)
        px = _mixers_pre(x, mod_x, lw, tab_x)

        k_all = jnp.concatenate([pc['k'], px['k']], axis=1)
        v_all = jnp.concatenate([pc['v'], px['v']], axis=1)
        ym_x = _attention(px['q'], k_all, v_all, lw['g_out_mla'])
        yr_c, s_fw, s_bw = _rwkv_stream(pc['rw'], zero_state, zero_state, lw, need_ctx)
        yr_x, _, _ = _rwkv_stream(px['rw'], s_fw, s_bw, lw, True)
        yt_c, r_fw, r_bw = _ret_stream(pc['zt'], zero_state, zero_state, lw, tab_c, need_ctx)
        yt_x, _, _ = _ret_stream(px['zt'], r_fw, r_bw, lw, tab_x, True)

        x = _channel_mix((px['y_conv'], ym_x, yr_x, yt_x), x, mod_x, lw)
        if need_ctx:
            ym_c = _attention(pc['q'], pc['k'], pc['v'], lw['g_out_mla'])
            xc = _channel_mix((pc['y_conv'], ym_c, yr_c, yt_c), xc, mod_c, lw)
    return x
```
